```python
import jax
import jax.numpy as jnp
from jax import lax
import numpy as np

D_MODEL = 2048
BATCH = 4
SEQ = 2048
DEPTH = 1
DEC_BATCH = 128
DEC_SEQ = 4
PAST_LEN = 2048
PAGE_SIZE = 128

HEAD_DIM = 64
DIL_GROUPS = ((128, 1), (512, 4), (2048, 16))
HEADS_PER_GROUP = 4
N_ATT_HEADS = HEADS_PER_GROUP * len(DIL_GROUPS)
ATT_WIDTH = N_ATT_HEADS * HEAD_DIM
ATT_OUT = HEADS_PER_GROUP * HEAD_DIM
BAND_BLOCK = 128
RWKV_HEADS = 12
RWKV_HEAD_DIM = 64
RWKV_WIDTH = RWKV_HEADS * RWKV_HEAD_DIM
DECAY_LORA = 96
AAA_LORA = 96
GATE_LORA = 256
GN_EPS = 64e-5
MEM_TOKENS = 256
MEM_HEADS = 4
MEM_HEAD_DIM = 128
MEM_WIDTH = MEM_HEADS * MEM_HEAD_DIM
PROJ_WIDTH = 3 * ATT_WIDTH + 3 * RWKV_WIDTH + MEM_WIDTH
BRANCH_WIDTH = ATT_OUT + RWKV_WIDTH + MEM_WIDTH
N_BRANCHES = 3
N_EXPERTS = 32
TOP_K = 4
D_EXPERT = 2048
SWIGLU_LIMIT = 7.0
SWIGLU_ALPHA = 1.702
LN_EPS = 1e-5
DEEPNORM_ALPHA = (2.0 * DEPTH) ** 0.25
DEEPNORM_BETA = (8.0 * DEPTH) ** -0.25
NEG_INF = -1e30

kernel_name = "hybrid_dilated_rwkv7_memory_moe_step"


def layer_norm(x, g, b):
    xf = x.astype(jnp.float32)
    mu = jnp.mean(xf, -1, keepdims=True)
    var = jnp.mean(jnp.square(xf - mu), -1, keepdims=True)
    return ((xf - mu) * lax.rsqrt(var + LN_EPS) * g + b).astype(x.dtype)


def alibi_slopes():
    h = jnp.arange(1, N_ATT_HEADS + 1, dtype=jnp.float32)
    return jnp.exp2(-8.0 * h / N_ATT_HEADS)


def project_in(u, lp):
    B, T, _ = u.shape
    p = u @ lp["w_in"]
    heads = lambda t: t.reshape(B, T, N_ATT_HEADS, HEAD_DIM)
    qa = heads(p[..., :ATT_WIDTH])
    ka = heads(p[..., ATT_WIDTH:2 * ATT_WIDTH])
    va = heads(p[..., 2 * ATT_WIDTH:3 * ATT_WIDTH])
    rkv = p[..., 3 * ATT_WIDTH:3 * ATT_WIDTH + 3 * RWKV_WIDTH]
    qm = p[..., 3 * ATT_WIDTH + 3 * RWKV_WIDTH:].reshape(B, T, MEM_HEADS, MEM_HEAD_DIM)
    return qa, ka, va, rkv, qm


def dilated_attention_prompt(q, k, v, dil, n_back, slopes):
    B, T, H, E = q.shape
    L = T // dil
    nb = -(-L // BAND_BLOCK)
    Lp = nb * BAND_BLOCK

    def by_residue(a):
        return a.astype(jnp.float32).reshape(B, L, dil, H, E).transpose(0, 2, 1, 3, 4)

    pad_q = ((0, 0), (0, 0), (0, Lp - L), (0, 0), (0, 0))
    pad_kv = ((0, 0), (0, 0), (BAND_BLOCK, Lp - L), (0, 0), (0, 0))
    qb = jnp.pad(by_residue(q), pad_q).reshape(B, dil, nb, BAND_BLOCK, H, E)

    def key_blocks(a):
        a = jnp.pad(by_residue(a), pad_kv).reshape(B, dil, nb + 1, BAND_BLOCK, H, E)
        return jnp.concatenate([a[:, :, :-1], a[:, :, 1:]], axis=3)

    kb, vb = key_blocks(k), key_blocks(v)
    qi = jnp.arange(BAND_BLOCK)[:, None]
    kj = jnp.arange(2 * BAND_BLOCK)[None, :]
    dist = qi + BAND_BLOCK - kj
    key_row = (jnp.arange(nb) * BAND_BLOCK - BAND_BLOCK)[:, None, None] + kj[None]
    valid = (dist >= 0) & (dist <= n_back) & (key_row >= 0)
    bias = -slopes[:, None, None] * (dist * dil).astype(jnp.float32)
    s = jnp.einsum("brnqhe,brnkhe->brnhqk", qb, kb) * (E ** -0.5) + bias[None, None, None]
    s = jnp.where(valid[None, None, :, None], s, NEG_INF)
    m = jnp.max(s, -1, keepdims=True)
    p = jnp.exp(s - m)
    den = jnp.sum(p, -1, keepdims=True)
    o = jnp.einsum("brnhqk,brnkhe->brnhqe", p, vb) / den
    lse = (m + jnp.log(den))[..., 0]
    o = o.transpose(0, 2, 4, 1, 3, 5).reshape(B, Lp, dil, H, E)[:, :L].reshape(B, T, H, E)
    lse = lse.transpose(0, 2, 4, 1, 3).reshape(B, Lp, dil, H)[:, :L].reshape(B, T, H)
    return o, lse


def dilated_attention_step(q, k_buf, v_buf, k_new, v_new, dil, n_back, slopes):
    Lb = k_buf.shape[1]
    T, E = q.shape[1], q.shape[3]
    k_all = jnp.concatenate([k_buf.astype(k_new.dtype), k_new], axis=1).astype(jnp.float32)
    v_all = jnp.concatenate([v_buf.astype(v_new.dtype), v_new], axis=1).astype(jnp.float32)
    j = jnp.arange(n_back + 1)
    idx = Lb + jnp.arange(T)[:, None] - j[None, :] * dil
    valid = idx >= 0
    idx = jnp.maximum(idx, 0)
    kg = k_all[:, idx]
    vg = v_all[:, idx]
    bias = -slopes[:, None] * (j * dil).astype(jnp.float32)[None, :]
    s = jnp.einsum("bthe,btjhe->bhtj", q.astype(jnp.float32), kg) * (E ** -0.5) + bias[None, :, None, :]
    s = jnp.where(valid[None, None], s, NEG_INF)
    m = jnp.max(s, -1, keepdims=True)
    p = jnp.exp(s - m)
    den = jnp.sum(p, -1, keepdims=True)
    o = jnp.einsum("bhtj,btjhe->bhte", p, vg) / den
    lse = (m + jnp.log(den))[..., 0]
    return o.transpose(0, 2, 1, 3), lse.transpose(0, 2, 1)


def combine_by_denominator(outs, lses):
    o = jnp.stack(outs, 0)
    wts = jax.nn.softmax(jnp.stack(lses, 0), axis=0)
    return jnp.sum(wts[..., None] * o, axis=0)


def rwkv_recurrence(s0, r, w, k, v, a, b):
    def step(S, inp):
        r_t, w_t, k_t, v_t, a_t, b_t = inp
        sa = jnp.einsum("bhvk,bhk->bhv", S, a_t)
        S = S * w_t[:, :, None, :] + sa[..., None] * b_t[:, :, None, :] + v_t[..., None] * k_t[:, :, None, :]
        return S, jnp.einsum("bhvk,bhk->bhv", S, r_t)
    xs = tuple(jnp.swapaxes(t, 0, 1) for t in (r, w, k, v, a, b))
    S, o = lax.scan(step, s0.astype(jnp.float32), xs)
    return S, jnp.swapaxes(o, 0, 1)


def rwkv_branch(u, u_last, s0, p_rkv, lp):
    B, T, _ = u.shape
    f32 = jnp.float32
    w_rkv = lp["w_in"][:, 3 * ATT_WIDTH:3 * ATT_WIDTH + 3 * RWKV_WIDTH]
    u_last = u_last.astype(u.dtype)
    u_prev = jnp.concatenate([u_last[:, None], u[:, :-1]], axis=1)
    p_prev = jnp.concatenate([(u_last @ w_rkv)[:, None], p_rkv[:, :-1]], axis=1)
    rkv = p_rkv + (p_prev - p_rkv) * lp["mu_rkv"]
    r, k, v = jnp.split(rkv, 3, axis=-1)
    du = u_prev - u
    mu = lp["mu_wag"]
    xw = u + du * mu[0]
    xa = u + du * mu[1]
    xg = u + du * mu[2]
    w_log = -jax.nn.softplus(-(lp["w0"] + jnp.tanh(xw @ lp["w1"]) @ lp["w2"]).astype(f32)) - 0.5
    decay = jnp.exp(-jnp.exp(w_log))
    a = jax.nn.sigmoid((lp["a0"] + (xa @ lp["a1"]) @ lp["a2"]).astype(f32))
    g = jax.nn.sigmoid(xg @ lp["g1"]) @ lp["g2"]
    heads = lambda t: t.astype(f32).reshape(B, T, RWKV_HEADS, RWKV_HEAD_DIM)
    kk = heads(k * lp["k_k"])
    kk = kk / jnp.maximum(jnp.sqrt(jnp.sum(kk * kk, -1, keepdims=True)), 1e-12)
    k = heads(k.astype(f32) * (1.0 + (a - 1.0) * lp["k_a"]))
    r, v, decay, a = heads(r), heads(v), heads(decay), heads(a)
    s_new, o = rwkv_recurrence(s0, r, decay, k, v, -kk, kk * a)
    mean = jnp.mean(o, -1, keepdims=True)
    var = jnp.mean(jnp.square(o - mean), -1, keepdims=True)
    on = ((o - mean) * lax.rsqrt(var + GN_EPS)).reshape(B, T, RWKV_WIDTH) * lp["gn_g"] + lp["gn_b"]
    bonus = (jnp.sum(r * k * lp["r_k"], -1, keepdims=True) * v).reshape(B, T, RWKV_WIDTH)
    return (on + bonus) * g, s_new


def memory_attention(q, mk, mv):
    s = jnp.einsum("bthe,bmhe->bhtm", q.astype(jnp.float32), mk.astype(jnp.float32)) * (MEM_HEAD_DIM ** -0.5)
    p = jax.nn.softmax(s, axis=-1)
    return jnp.einsum("bhtm,bmhe->bthe", p, mv.astype(jnp.float32))


def merge_branches(u, o_att, o_rwkv, o_mem, lp):
    B, T, D = u.shape
    dt = u.dtype
    wb = lp["w_branch"]
    y_att = o_att.reshape(B, T, ATT_OUT).astype(dt) @ wb[:ATT_OUT]
    y_rwkv = o_rwkv.astype(dt) @ wb[ATT_OUT:ATT_OUT + RWKV_WIDTH]
    y_mem = o_mem.reshape(B, T, MEM_WIDTH).astype(dt) @ wb[ATT_OUT + RWKV_WIDTH:]
    gate = jax.nn.sigmoid((u @ lp["w_gate"] + lp["b_gate"]).astype(jnp.float32)).astype(dt)
    gate = gate.reshape(B, T, N_BRANCHES, D)
    mixed = gate[:, :, 0] * y_att + gate[:, :, 1] * y_rwkv + gate[:, :, 2] * y_mem
    return mixed @ lp["w_out"]


def mix_prompt(u, mem, lp):
    B, T, _ = u.shape
    qa, ka, va, rkv, qm = project_in(u, lp)
    slopes = alibi_slopes()
    outs, lses, windows = [], [], []
    for g, (win, dil) in enumerate(DIL_GROUPS):
        hs = slice(g * HEADS_PER_GROUP, (g + 1) * HEADS_PER_GROUP)
        o, lse = dilated_attention_prompt(qa[:, :, hs], ka[:, :, hs], va[:, :, hs], dil, win // dil, slopes[hs])
        outs.append(o)
        lses.append(lse)
        keep = min(win, T)
        windows.append(jnp.stack([ka[:, T - keep:, hs], va[:, T - keep:, hs]], axis=2))
    o_att = combine_by_denominator(outs, lses)
    u0 = jnp.zeros((B, D_MODEL), u.dtype)
    s0 = jnp.zeros((B, RWKV_HEADS, RWKV_HEAD_DIM, RWKV_HEAD_DIM), jnp.float32)
    o_rwkv, wkv = rwkv_branch(u, u0, s0, rkv, lp)
    mem_kv = (mem @ lp["w_mem_kv"]).reshape(B, MEM_TOKENS, 2, MEM_HEADS, MEM_HEAD_DIM)
    o_mem = memory_attention(qm, mem_kv[:, :, 0], mem_kv[:, :, 1])
    return merge_branches(u, o_att, o_rwkv, o_mem, lp), windows, wkv, mem_kv


def mix_sample(u, win_caches, mem_kv, s0, u_last, lp):
    qa, ka, va, rkv, qm = project_in(u, lp)
    slopes = alibi_slopes()
    outs, lses, new_rows = [], [], []
    for g, (win, dil) in enumerate(DIL_GROUPS):
        hs = slice(g * HEADS_PER_GROUP, (g + 1) * HEADS_PER_GROUP)
        buf = win_caches[g]
        o, lse = dilated_attention_step(qa[:, :, hs], buf[:, :, 0], buf[:, :, 1],
                                        ka[:, :, hs], va[:, :, hs], dil, win // dil, slopes[hs])
        outs.append(o)
        lses.append(lse)
        new_rows.append(jnp.stack([ka[:, :, hs], va[:, :, hs]], axis=2))
    o_att = combine_by_denominator(outs, lses)
    o_rwkv, wkv = rwkv_branch(u, u_last, s0, rkv, lp)
    o_mem = memory_attention(qm, mem_kv[:, :, 0], mem_kv[:, :, 1])
    return merge_branches(u, o_att, o_rwkv, o_mem, lp), new_rows, wkv


def moe_ffn(x, w_router, b_router, w_gate_up, b_gate_up, w_down, b_down):
    logits = (x @ w_router + b_router).astype(jnp.float32)
    top_logit, top_idx = lax.top_k(logits, TOP_K)
    top_w = jax.nn.softmax(top_logit, axis=-1)
    gates = jnp.einsum("nk,nke->en", top_w, jax.nn.one_hot(top_idx, N_EXPERTS, dtype=jnp.float32))

    def expert(acc, ex):
        wgu, bgu, wdn, bdn, g = ex
        h = x @ wgu + bgu
        glu = jnp.minimum(h[:, :D_EXPERT], SWIGLU_LIMIT)
        up = jnp.clip(h[:, D_EXPERT:], -SWIGLU_LIMIT, SWIGLU_LIMIT)
        act = (up + 1.0) * glu * jax.nn.sigmoid(SWIGLU_ALPHA * glu)
        return acc + g[:, None] * (act @ wdn + bdn).astype(jnp.float32), None

    acc, _ = lax.scan(expert, jnp.zeros(x.shape, jnp.float32), (w_gate_up, b_gate_up, w_down, b_down, gates))
    return acc.astype(x.dtype)


def setup_inputs(seed: int = 0) -> dict:
    key = jax.random.key(seed)
    keys = iter(jax.random.split(key, 48))
    f32 = jnp.float32
    D, L, RW, E, F = D_MODEL, DEPTH, RWKV_WIDTH, N_EXPERTS, D_EXPERT

    def nrm(shape, scale):
        return scale * jax.random.normal(next(keys), shape, f32)

    def unif(shape, lo, hi):
        return jax.random.uniform(next(keys), shape, f32, lo, hi)

    inp = {}
    inp["x_prompt"] = nrm((BATCH, SEQ, D), 1.0)
    inp["x_sample"] = nrm((DEC_BATCH, DEC_SEQ, D), 1.0)
    for g, (win, _) in enumerate(DIL_GROUPS):
        inp["cache_win%d_kv" % g] = nrm((L, DEC_BATCH, min(win, PAST_LEN), 2, HEADS_PER_GROUP, HEAD_DIM), 1.0)
    inp["cache_mem_kv"] = nrm((L, DEC_BATCH, MEM_TOKENS, 2, MEM_HEADS, MEM_HEAD_DIM), 1.0)
    inp["state_wkv"] = nrm((L, DEC_BATCH, RWKV_HEADS, RWKV_HEAD_DIM, RWKV_HEAD_DIM), 0.3)
    inp["state_shift"] = nrm((L, DEC_BATCH, D), 1.0)
    inp["mem_prompt"] = nrm((BATCH, MEM_TOKENS, D), 1.0)
    inp["w_in"] = nrm((L, D, PROJ_WIDTH), D ** -0.5)
    inp["w_gate"] = nrm((L, D, N_BRANCHES * D), D ** -0.5)
    inp["b_gate"] = nrm((L, N_BRANCHES * D), 0.02)
    inp["w_branch"] = jnp.concatenate([nrm((L, ATT_OUT, D), ATT_OUT ** -0.5),
                                       nrm((L, RW, D), RW ** -0.5),
                                       nrm((L, MEM_WIDTH, D), MEM_WIDTH ** -0.5)], axis=1)
    inp["w_out"] = nrm((L, D, D), DEEPNORM_BETA * D ** -0.5)
    inp["w_mem_kv"] = nrm((L, D, 2 * MEM_WIDTH), D ** -0.5)
    inp["mu_rkv"] = unif((L, 3 * RW), 0.0, 1.0)
    inp["mu_wag"] = unif((L, 3, D), 0.0, 1.0)
    inp["w0"] = unif((L, RW), -6.5, -1.5)
    inp["w1"] = nrm((L, D, DECAY_LORA), D ** -0.5)
    inp["w2"] = nrm((L, DECAY_LORA, RW), 0.1 * DECAY_LORA ** -0.5)
    inp["a0"] = nrm((L, RW), 0.1)
    inp["a1"] = nrm((L, D, AAA_LORA), D ** -0.5)
    inp["a2"] = nrm((L, AAA_LORA, RW), AAA_LORA ** -0.5)
    inp["g1"] = nrm((L, D, GATE_LORA), D ** -0.5)
    inp["g2"] = nrm((L, GATE_LORA, RW), GATE_LORA ** -0.5)
    inp["k_k"] = 0.85 + nrm((L, RW), 0.02)
    inp["k_a"] = 1.0 + nrm((L, RW), 0.02)
    inp["r_k"] = nrm((L, RWKV_HEADS, RWKV_HEAD_DIM), 0.1)
    inp["gn_g"] = 1.0 + nrm((L, RW), 0.02)
    inp["gn_b"] = nrm((L, RW), 0.02)
    inp["ln1_g"] = 1.0 + nrm((L, D), 0.02)
    inp["ln1_b"] = nrm((L, D), 0.02)
    inp["w_router"] = nrm((L, D, E), D ** -0.5)
    inp["b_router"] = nrm((L, E), 0.01)
    inp["w_gate_up"] = nrm((L, E, D, 2 * F), D ** -0.5)
    inp["b_gate_up"] = nrm((L, E, 2 * F), 0.01)
    inp["w_down"] = nrm((L, E, F, D), DEEPNORM_BETA * F ** -0.5)
    inp["b_down"] = nrm((L, E, D), 0.01)
    inp["ln2_g"] = 1.0 + nrm((L, D), 0.02)
    inp["ln2_b"] = nrm((L, D), 0.02)
    return inp


def reference(x_prompt, x_sample, cache_win0_kv, cache_win1_kv, cache_win2_kv, cache_mem_kv,
              state_wkv, state_shift, mem_prompt,
              w_in, w_gate, b_gate, w_branch, w_out, w_mem_kv,
              mu_rkv, mu_wag, w0, w1, w2, a0, a1, a2, g1, g2, k_k, k_a, r_k, gn_g, gn_b,
              ln1_g, ln1_b, w_router, b_router, w_gate_up, b_gate_up, w_down, b_down, ln2_g, ln2_b):
    xp, xs = x_prompt, x_sample
    bp, tp, d = xp.shape
    bs, ts, _ = xs.shape
    n_prompt = bp * tp
    p_win = [[], [], []]
    s_win = [[], [], []]
    p_wkv, p_shift, p_mem, s_wkv, s_shift = [], [], [], [], []
    for l in range(DEPTH):
        lp = {"w_in": w_in[l], "w_gate": w_gate[l], "b_gate": b_gate[l], "w_branch": w_branch[l],
              "w_out": w_out[l], "w_mem_kv": w_mem_kv[l], "mu_rkv": mu_rkv[l], "mu_wag": mu_wag[l],
              "w0": w0[l], "w1": w1[l], "w2": w2[l], "a0": a0[l], "a1": a1[l], "a2": a2[l],
              "g1": g1[l], "g2": g2[l], "k_k": k_k[l], "k_a": k_a[l], "r_k": r_k[l],
              "gn_g": gn_g[l], "gn_b": gn_b[l]}
        mix_p, win_p, wkv_p, mem_kv_p = mix_prompt(xp, mem_prompt, lp)
        mix_s, win_s, wkv_s = mix_sample(xs, (cache_win0_kv[l], cache_win1_kv[l], cache_win2_kv[l]),
                                         cache_mem_kv[l], state_wkv[l], state_shift[l], lp)
        for g in range(len(DIL_GROUPS)):
            p_win[g].append(win_p[g])
            s_win[g].append(win_s[g].astype(cache_win0_kv.dtype))
        p_wkv.append(wkv_p.astype(xp.dtype))
        s_wkv.append(wkv_s.astype(state_wkv.dtype))
        p_shift.append(xp[:, -1])
        s_shift.append(xs[:, -1].astype(state_shift.dtype))
        p_mem.append(mem_kv_p)
        hp = layer_norm(DEEPNORM_ALPHA * xp + mix_p, ln1_g[l], ln1_b[l])
        hs = layer_norm(DEEPNORM_ALPHA * xs + mix_s, ln1_g[l], ln1_b[l])
        tokens = jnp.concatenate([hp.reshape(-1, d), hs.reshape(-1, d)], axis=0)
        ffn = moe_ffn(tokens, w_router[l], b_router[l], w_gate_up[l], b_gate_up[l], w_down[l], b_down[l])
        xp = layer_norm(DEEPNORM_ALPHA * hp + ffn[:n_prompt].reshape(bp, tp, d), ln2_g[l], ln2_b[l])
        xs = layer_norm(DEEPNORM_ALPHA * hs + ffn[n_prompt:].reshape(bs, ts, d), ln2_g[l], ln2_b[l])
    y_prompt, y_sample = xp, xs
    new_win0_prompt = jnp.stack(p_win[0], 0)
    new_win1_prompt = jnp.stack(p_win[1], 0)
    new_win2_prompt = jnp.stack(p_win[2], 0)
    new_wkv_prompt = jnp.stack(p_wkv, 0)
    new_shift_prompt = jnp.stack(p_shift, 0)
    new_mem_kv_prompt = jnp.stack(p_mem, 0)
    new_win0_sample = jnp.stack(s_win[0], 0)
    new_win1_sample = jnp.stack(s_win[1], 0)
    new_win2_sample = jnp.stack(s_win[2], 0)
    new_wkv_sample = jnp.stack(s_wkv, 0)
    new_shift_sample = jnp.stack(s_shift, 0)
    return (y_prompt, y_sample, new_win0_prompt, new_win1_prompt, new_win2_prompt, new_wkv_prompt,
            new_shift_prompt, new_mem_kv_prompt, new_win0_sample, new_win1_sample, new_win2_sample,
            new_wkv_sample, new_shift_sample)
```

```python
import functools

import jax
import jax.numpy as jnp
from jax import lax
from jax.experimental import pallas as pl
from jax.experimental.pallas import tpu as pltpu

D_MODEL = 2048
DEPTH = 1
HEAD_DIM = 64
DIL_GROUPS = ((128, 1), (512, 4), (2048, 16))
HEADS_PER_GROUP = 4
N_ATT_HEADS = HEADS_PER_GROUP * len(DIL_GROUPS)
ATT_WIDTH = N_ATT_HEADS * HEAD_DIM
ATT_OUT = HEADS_PER_GROUP * HEAD_DIM
BAND_BLOCK = 128
RWKV_HEADS = 12
RWKV_HEAD_DIM = 64
RWKV_WIDTH = RWKV_HEADS * RWKV_HEAD_DIM
GN_EPS = 64e-5
MEM_TOKENS = 256
MEM_HEADS = 4
MEM_HEAD_DIM = 128
MEM_WIDTH = MEM_HEADS * MEM_HEAD_DIM
N_BRANCHES = 3
N_EXPERTS = 32
TOP_K = 4
D_EXPERT = 2048
SWIGLU_LIMIT = 7.0
SWIGLU_ALPHA = 1.702
LN_EPS = 1e-5
DEEPNORM_ALPHA = (2.0 * DEPTH) ** 0.25
NEG_INF = -1e30

MOE_TILE_M = 256
MOE_F_CHUNK = 512
MOE_D_CHUNK = 512
VMEM_LIMIT_BYTES = 56 * 1024 * 1024


def layer_norm(x, g, b):
    xf = x.astype(jnp.float32)
    mu = jnp.mean(xf, -1, keepdims=True)
    var = jnp.mean(jnp.square(xf - mu), -1, keepdims=True)
    return ((xf - mu) * lax.rsqrt(var + LN_EPS) * g + b).astype(x.dtype)


def alibi_slopes():
    h = jnp.arange(1, N_ATT_HEADS + 1, dtype=jnp.float32)
    return jnp.exp2(-8.0 * h / N_ATT_HEADS)


def project_in(u, lp):
    B, T, _ = u.shape
    p = u @ lp["w_in"]
    heads = lambda t: t.reshape(B, T, N_ATT_HEADS, HEAD_DIM)
    qa = heads(p[..., :ATT_WIDTH])
    ka = heads(p[..., ATT_WIDTH:2 * ATT_WIDTH])
    va = heads(p[..., 2 * ATT_WIDTH:3 * ATT_WIDTH])
    rkv = p[..., 3 * ATT_WIDTH:3 * ATT_WIDTH + 3 * RWKV_WIDTH]
    qm = p[..., 3 * ATT_WIDTH + 3 * RWKV_WIDTH:].reshape(B, T, MEM_HEADS, MEM_HEAD_DIM)
    return qa, ka, va, rkv, qm


def dilated_attention_prompt(q, k, v, dil, n_back, slopes):
    B, T, H, E = q.shape
    L = T // dil
    nb = -(-L // BAND_BLOCK)
    Lp = nb * BAND_BLOCK

    def by_residue(a):
        return a.astype(jnp.float32).reshape(B, L, dil, H, E).transpose(0, 2, 1, 3, 4)

    pad_q = ((0, 0), (0, 0), (0, Lp - L), (0, 0), (0, 0))
    pad_kv = ((0, 0), (0, 0), (BAND_BLOCK, Lp - L), (0, 0), (0, 0))
    qb = jnp.pad(by_residue(q), pad_q).reshape(B, dil, nb, BAND_BLOCK, H, E)

    def key_blocks(a):
        a = jnp.pad(by_residue(a), pad_kv).reshape(B, dil, nb + 1, BAND_BLOCK, H, E)
        return jnp.concatenate([a[:, :, :-1], a[:, :, 1:]], axis=3)

    kb, vb = key_blocks(k), key_blocks(v)
    qi = jnp.arange(BAND_BLOCK)[:, None]
    kj = jnp.arange(2 * BAND_BLOCK)[None, :]
    dist = qi + BAND_BLOCK - kj
    key_row = (jnp.arange(nb) * BAND_BLOCK - BAND_BLOCK)[:, None, None] + kj[None]
    valid = (dist >= 0) & (dist <= n_back) & (key_row >= 0)
    bias = -slopes[:, None, None] * (dist * dil).astype(jnp.float32)
    s = jnp.einsum("brnqhe,brnkhe->brnhqk", qb, kb) * (E ** -0.5) + bias[None, None, None]
    s = jnp.where(valid[None, None, :, None], s, NEG_INF)
    m = jnp.max(s, -1, keepdims=True)
    p = jnp.exp(s - m)
    den = jnp.sum(p, -1, keepdims=True)
    o = jnp.einsum("brnhqk,brnkhe->brnhqe", p, vb) / den
    lse = (m + jnp.log(den))[..., 0]
    o = o.transpose(0, 2, 4, 1, 3, 5).reshape(B, Lp, dil, H, E)[:, :L].reshape(B, T, H, E)
    lse = lse.transpose(0, 2, 4, 1, 3).reshape(B, Lp, dil, H)[:, :L].reshape(B, T, H)
    return o, lse


def dilated_attention_step(q, k_buf, v_buf, k_new, v_new, dil, n_back, slopes):
    Lb = k_buf.shape[1]
    T, E = q.shape[1], q.shape[3]
    k_all = jnp.concatenate([k_buf.astype(k_new.dtype), k_new], axis=1).astype(jnp.float32)
    v_all = jnp.concatenate([v_buf.astype(v_new.dtype), v_new], axis=1).astype(jnp.float32)
    j = jnp.arange(n_back + 1)
    idx = Lb + jnp.arange(T)[:, None] - j[None, :] * dil
    valid = idx >= 0
    idx = jnp.maximum(idx, 0)
    kg = k_all[:, idx]
    vg = v_all[:, idx]
    bias = -slopes[:, None] * (j * dil).astype(jnp.float32)[None, :]
    s = jnp.einsum("bthe,btjhe->bhtj", q.astype(jnp.float32), kg) * (E ** -0.5) + bias[None, :, None, :]
    s = jnp.where(valid[None, None], s, NEG_INF)
    m = jnp.max(s, -1, keepdims=True)
    p = jnp.exp(s - m)
    den = jnp.sum(p, -1, keepdims=True)
    o = jnp.einsum("bhtj,btjhe->bhte", p, vg) / den
    lse = (m + jnp.log(den))[..., 0]
    return o.transpose(0, 2, 1, 3), lse.transpose(0, 2, 1)


def combine_by_denominator(outs, lses):
    o = jnp.stack(outs, 0)
    wts = jax.nn.softmax(jnp.stack(lses, 0), axis=0)
    return jnp.sum(wts[..., None] * o, axis=0)


def rwkv_recurrence(s0, r, w, k, v, a, b):
    def step(S, inp):
        r_t, w_t, k_t, v_t, a_t, b_t = inp
        sa = jnp.einsum("bhvk,bhk->bhv", S, a_t)
        S = S * w_t[:, :, None, :] + sa[..., None] * b_t[:, :, None, :] + v_t[..., None] * k_t[:, :, None, :]
        return S, jnp.einsum("bhvk,bhk->bhv", S, r_t)
    xs = tuple(jnp.swapaxes(t, 0, 1) for t in (r, w, k, v, a, b))
    S, o = lax.scan(step, s0.astype(jnp.float32), xs)
    return S, jnp.swapaxes(o, 0, 1)


def rwkv_branch(u, u_last, s0, p_rkv, lp):
    B, T, _ = u.shape
    f32 = jnp.float32
    w_rkv = lp["w_in"][:, 3 * ATT_WIDTH:3 * ATT_WIDTH + 3 * RWKV_WIDTH]
    u_last = u_last.astype(u.dtype)
    u_prev = jnp.concatenate([u_last[:, None], u[:, :-1]], axis=1)
    p_prev = jnp.concatenate([(u_last @ w_rkv)[:, None], p_rkv[:, :-1]], axis=1)
    rkv = p_rkv + (p_prev - p_rkv) * lp["mu_rkv"]
    r, k, v = jnp.split(rkv, 3, axis=-1)
    du = u_prev - u
    mu = lp["mu_wag"]
    xw = u + du * mu[0]
    xa = u + du * mu[1]
    xg = u + du * mu[2]
    w_log = -jax.nn.softplus(-(lp["w0"] + jnp.tanh(xw @ lp["w1"]) @ lp["w2"]).astype(f32)) - 0.5
    decay = jnp.exp(-jnp.exp(w_log))
    a = jax.nn.sigmoid((lp["a0"] + (xa @ lp["a1"]) @ lp["a2"]).astype(f32))
    g = jax.nn.sigmoid(xg @ lp["g1"]) @ lp["g2"]
    heads = lambda t: t.astype(f32).reshape(B, T, RWKV_HEADS, RWKV_HEAD_DIM)
    kk = heads(k * lp["k_k"])
    kk = kk / jnp.maximum(jnp.sqrt(jnp.sum(kk * kk, -1, keepdims=True)), 1e-12)
    k = heads(k.astype(f32) * (1.0 + (a - 1.0) * lp["k_a"]))
    r, v, decay, a = heads(r), heads(v), heads(decay), heads(a)
    s_new, o = rwkv_recurrence(s0, r, decay, k, v, -kk, kk * a)
    mean = jnp.mean(o, -1, keepdims=True)
    var = jnp.mean(jnp.square(o - mean), -1, keepdims=True)
    on = ((o - mean) * lax.rsqrt(var + GN_EPS)).reshape(B, T, RWKV_WIDTH) * lp["gn_g"] + lp["gn_b"]
    bonus = (jnp.sum(r * k * lp["r_k"], -1, keepdims=True) * v).reshape(B, T, RWKV_WIDTH)
    return (on + bonus) * g, s_new


def memory_attention(q, mk, mv):
    s = jnp.einsum("bthe,bmhe->bhtm", q.astype(jnp.float32), mk.astype(jnp.float32)) * (MEM_HEAD_DIM ** -0.5)
    p = jax.nn.softmax(s, axis=-1)
    return jnp.einsum("bhtm,bmhe->bthe", p, mv.astype(jnp.float32))


def merge_branches(u, o_att, o_rwkv, o_mem, lp):
    B, T, D = u.shape
    dt = u.dtype
    wb = lp["w_branch"]
    y_att = o_att.reshape(B, T, ATT_OUT).astype(dt) @ wb[:ATT_OUT]
    y_rwkv = o_rwkv.astype(dt) @ wb[ATT_OUT:ATT_OUT + RWKV_WIDTH]
    y_mem = o_mem.reshape(B, T, MEM_WIDTH).astype(dt) @ wb[ATT_OUT + RWKV_WIDTH:]
    gate = jax.nn.sigmoid((u @ lp["w_gate"] + lp["b_gate"]).astype(jnp.float32)).astype(dt)
    gate = gate.reshape(B, T, N_BRANCHES, D)
    mixed = gate[:, :, 0] * y_att + gate[:, :, 1] * y_rwkv + gate[:, :, 2] * y_mem
    return mixed @ lp["w_out"]


def mix_prompt(u, mem, lp):
    B, T, _ = u.shape
    qa, ka, va, rkv, qm = project_in(u, lp)
    slopes = alibi_slopes()
    outs, lses, windows = [], [], []
    for g, (win, dil) in enumerate(DIL_GROUPS):
        hs = slice(g * HEADS_PER_GROUP, (g + 1) * HEADS_PER_GROUP)
        o, lse = dilated_attention_prompt(qa[:, :, hs], ka[:, :, hs], va[:, :, hs], dil, win // dil, slopes[hs])
        outs.append(o)
        lses.append(lse)
        keep = min(win, T)
        windows.append(jnp.stack([ka[:, T - keep:, hs], va[:, T - keep:, hs]], axis=2))
    o_att = combine_by_denominator(outs, lses)
    u0 = jnp.zeros((B, D_MODEL), u.dtype)
    s0 = jnp.zeros((B, RWKV_HEADS, RWKV_HEAD_DIM, RWKV_HEAD_DIM), jnp.float32)
    o_rwkv, wkv = rwkv_branch(u, u0, s0, rkv, lp)
    mem_kv = (mem @ lp["w_mem_kv"]).reshape(B, MEM_TOKENS, 2, MEM_HEADS, MEM_HEAD_DIM)
    o_mem = memory_attention(qm, mem_kv[:, :, 0], mem_kv[:, :, 1])
    return merge_branches(u, o_att, o_rwkv, o_mem, lp), windows, wkv, mem_kv


def mix_sample(u, win_caches, mem_kv, s0, u_last, lp):
    qa, ka, va, rkv, qm = project_in(u, lp)
    slopes = alibi_slopes()
    outs, lses, new_rows = [], [], []
    for g, (win, dil) in enumerate(DIL_GROUPS):
        hs = slice(g * HEADS_PER_GROUP, (g + 1) * HEADS_PER_GROUP)
        buf = win_caches[g]
        o, lse = dilated_attention_step(qa[:, :, hs], buf[:, :, 0], buf[:, :, 1],
                                        ka[:, :, hs], va[:, :, hs], dil, win // dil, slopes[hs])
        outs.append(o)
        lses.append(lse)
        new_rows.append(jnp.stack([ka[:, :, hs], va[:, :, hs]], axis=2))
    o_att = combine_by_denominator(outs, lses)
    o_rwkv, wkv = rwkv_branch(u, u_last, s0, rkv, lp)
    o_mem = memory_attention(qm, mem_kv[:, :, 0], mem_kv[:, :, 1])
    return merge_branches(u, o_att, o_rwkv, o_mem, lp), new_rows, wkv


def _moe_up_kernel(tile_expert_ref, tile_valid_ref, x_ref, wg_ref, wu_ref, bg_ref, bu_ref,
                   h_ref, wg_bf, wu_bf):
    i = pl.program_id(1)
    e = tile_expert_ref[i]
    prev = tile_expert_ref[jnp.maximum(i - 1, 0)]

    @pl.when((i == 0) | (e != prev))
    def _():
        wg_bf[...] = wg_ref[0].astype(jnp.bfloat16)
        wu_bf[...] = wu_ref[0].astype(jnp.bfloat16)

    @pl.when(tile_valid_ref[i] != 0)
    def _():
        x = x_ref[...]
        hg = jnp.dot(x, wg_bf[...], preferred_element_type=jnp.float32) + bg_ref[0]
        hu = jnp.dot(x, wu_bf[...], preferred_element_type=jnp.float32) + bu_ref[0]
        glu = jnp.minimum(hg, SWIGLU_LIMIT)
        up = jnp.clip(hu, -SWIGLU_LIMIT, SWIGLU_LIMIT)
        act = (up + 1.0) * glu * jax.nn.sigmoid(SWIGLU_ALPHA * glu)
        h_ref[...] = act.astype(h_ref.dtype)

    @pl.when(tile_valid_ref[i] == 0)
    def _():
        h_ref[...] = jnp.zeros_like(h_ref)


def _moe_down_kernel(tile_expert_ref, tile_valid_ref, h_ref, wd_ref, bd_ref, gate_ref,
                     y_ref, wd_bf):
    i = pl.program_id(1)
    e = tile_expert_ref[i]
    prev = tile_expert_ref[jnp.maximum(i - 1, 0)]

    @pl.when((i == 0) | (e != prev))
    def _():
        wd_bf[...] = wd_ref[0].astype(jnp.bfloat16)

    @pl.when(tile_valid_ref[i] != 0)
    def _():
        y = jnp.dot(h_ref[...], wd_bf[...], preferred_element_type=jnp.float32) + bd_ref[0]
        y_ref[...] = y * gate_ref[...]

    @pl.when(tile_valid_ref[i] == 0)
    def _():
        y_ref[...] = jnp.zeros_like(y_ref)


def _moe_grouped(x_sorted, slot_gate, tile_expert, tile_valid, w_gate_up, b_gate_up, w_down, b_down):
    p_pad, d = x_sorted.shape
    n_exp, _, two_f = w_gate_up.shape
    f = two_f // 2
    tm, fc, dc = MOE_TILE_M, MOE_F_CHUNK, MOE_D_CHUNK
    nt = p_pad // tm
    nfc = f // fc
    ndc = d // dc
    bgu = b_gate_up.reshape(n_exp, 1, two_f)
    bdn = b_down.reshape(n_exp, 1, d)

    h = pl.pallas_call(
        _moe_up_kernel,
        out_shape=jax.ShapeDtypeStruct((p_pad, f), jnp.bfloat16),
        grid_spec=pltpu.PrefetchScalarGridSpec(
            num_scalar_prefetch=2,
            grid=(nfc, nt),
            in_specs=[
                pl.BlockSpec((tm, d), lambda c, i, te, tv: (i, 0)),
                pl.BlockSpec((1, d, fc), lambda c, i, te, tv: (te[i], 0, c)),
                pl.BlockSpec((1, d, fc), lambda c, i, te, tv: (te[i], 0, nfc + c)),
                pl.BlockSpec((1, 1, fc), lambda c, i, te, tv: (te[i], 0, c)),
                pl.BlockSpec((1, 1, fc), lambda c, i, te, tv: (te[i], 0, nfc + c)),
            ],
            out_specs=pl.BlockSpec((tm, fc), lambda c, i, te, tv: (i, c)),
            scratch_shapes=[pltpu.VMEM((d, fc), jnp.bfloat16), pltpu.VMEM((d, fc), jnp.bfloat16)],
        ),
        compiler_params=pltpu.CompilerParams(
            dimension_semantics=("arbitrary", "arbitrary"),
            vmem_limit_bytes=VMEM_LIMIT_BYTES),
        name="moe_up",
    )(tile_expert, tile_valid, x_sorted, w_gate_up, w_gate_up, bgu, bgu)

    y = pl.pallas_call(
        _moe_down_kernel,
        out_shape=jax.ShapeDtypeStruct((p_pad, d), jnp.float32),
        grid_spec=pltpu.PrefetchScalarGridSpec(
            num_scalar_prefetch=2,
            grid=(ndc, nt),
            in_specs=[
                pl.BlockSpec((tm, f), lambda c, i, te, tv: (i, 0)),
                pl.BlockSpec((1, f, dc), lambda c, i, te, tv: (te[i], 0, c)),
                pl.BlockSpec((1, 1, dc), lambda c, i, te, tv: (te[i], 0, c)),
                pl.BlockSpec((tm, 1), lambda c, i, te, tv: (i, 0)),
            ],
            out_specs=pl.BlockSpec((tm, dc), lambda c, i, te, tv: (i, c)),
            scratch_shapes=[pltpu.VMEM((f, dc), jnp.bfloat16)],
        ),
        compiler_params=pltpu.CompilerParams(
            dimension_semantics=("arbitrary", "arbitrary"),
            vmem_limit_bytes=VMEM_LIMIT_BYTES),
        name="moe_down",
    )(tile_expert, tile_valid, h, w_down, bdn, slot_gate)
    return y


def moe_ffn(x, w_router, b_router, w_gate_up, b_gate_up, w_down, b_down):
    n, d = x.shape
    tm = MOE_TILE_M
    logits = (x @ w_router + b_router).astype(jnp.float32)
    top_logit, top_idx = lax.top_k(logits, TOP_K)
    top_w = jax.nn.softmax(top_logit, axis=-1)

    n_pairs = n * TOP_K
    nt = n_pairs // tm + N_EXPERTS
    p_pad = nt * tm
    pair_expert = top_idx.reshape(n_pairs)
    pair_token = jnp.repeat(jnp.arange(n, dtype=jnp.int32), TOP_K)
    pair_w = top_w.reshape(n_pairs)
    order = jnp.argsort(pair_expert, stable=True)
    sorted_expert = pair_expert[order]
    counts = jnp.sum(jax.nn.one_hot(pair_expert, N_EXPERTS, dtype=jnp.int32), axis=0)
    tiles_per = (counts + tm - 1) // tm
    tile_start = jnp.cumsum(tiles_per) - tiles_per
    group_start = jnp.cumsum(counts) - counts
    rank = jnp.arange(n_pairs, dtype=jnp.int32) - group_start[sorted_expert]
    dest_sorted = tile_start[sorted_expert] * tm + rank
    slot_token = jnp.zeros((p_pad,), jnp.int32).at[dest_sorted].set(pair_token[order])
    slot_gate = jnp.zeros((p_pad,), jnp.float32).at[dest_sorted].set(pair_w[order])
    pair_slot = jnp.zeros((n_pairs,), jnp.int32).at[order].set(dest_sorted).reshape(n, TOP_K)
    total_tiles = jnp.sum(tiles_per)
    tile_ids = jnp.arange(nt, dtype=jnp.int32)
    tile_end = jnp.cumsum(tiles_per)
    tile_expert = jnp.minimum(jnp.sum((tile_ids[:, None] >= tile_end[None, :]).astype(jnp.int32), axis=1),
                              N_EXPERTS - 1).astype(jnp.int32)
    tile_valid = (tile_ids < total_tiles).astype(jnp.int32)

    x_sorted = x.astype(jnp.bfloat16)[slot_token]
    y = _moe_grouped(x_sorted, slot_gate.reshape(p_pad, 1), tile_expert, tile_valid,
                     w_gate_up, b_gate_up, w_down, b_down)
    out = jnp.sum(y[pair_slot], axis=1)
    return out.astype(x.dtype)


def kernel(x_prompt, x_sample, cache_win0_kv, cache_win1_kv, cache_win2_kv, cache_mem_kv,
           state_wkv, state_shift, mem_prompt,
           w_in, w_gate, b_gate, w_branch, w_out, w_mem_kv,
           mu_rkv, mu_wag, w0, w1, w2, a0, a1, a2, g1, g2, k_k, k_a, r_k, gn_g, gn_b,
           ln1_g, ln1_b, w_router, b_router, w_gate_up, b_gate_up, w_down, b_down, ln2_g, ln2_b):
    xp, xs = x_prompt, x_sample
    bp, tp, d = xp.shape
    bs, ts, _ = xs.shape
    n_prompt = bp * tp
    l = 0
    lp = {"w_in": w_in[l], "w_gate": w_gate[l], "b_gate": b_gate[l], "w_branch": w_branch[l],
          "w_out": w_out[l], "w_mem_kv": w_mem_kv[l], "mu_rkv": mu_rkv[l], "mu_wag": mu_wag[l],
          "w0": w0[l], "w1": w1[l], "w2": w2[l], "a0": a0[l], "a1": a1[l], "a2": a2[l],
          "g1": g1[l], "g2": g2[l], "k_k": k_k[l], "k_a": k_a[l], "r_k": r_k[l],
          "gn_g": gn_g[l], "gn_b": gn_b[l]}
    mix_p, win_p, wkv_p, mem_kv_p = mix_prompt(xp, mem_prompt, lp)
    mix_s, win_s, wkv_s = mix_sample(xs, (cache_win0_kv[l], cache_win1_kv[l], cache_win2_kv[l]),
                                     cache_mem_kv[l], state_wkv[l], state_shift[l], lp)
    hp = layer_norm(DEEPNORM_ALPHA * xp + mix_p, ln1_g[l], ln1_b[l])
    hs = layer_norm(DEEPNORM_ALPHA * xs + mix_s, ln1_g[l], ln1_b[l])
    tokens = jnp.concatenate([hp.reshape(-1, d), hs.reshape(-1, d)], axis=0)
    ffn = moe_ffn(tokens, w_router[l], b_router[l], w_gate_up[l], b_gate_up[l], w_down[l], b_down[l])
    yp = layer_norm(DEEPNORM_ALPHA * hp + ffn[:n_prompt].reshape(bp, tp, d), ln2_g[l], ln2_b[l])
    ys = layer_norm(DEEPNORM_ALPHA * hs + ffn[n_prompt:].reshape(bs, ts, d), ln2_g[l], ln2_b[l])
    return (yp, ys,
            win_p[0][None], win_p[1][None], win_p[2][None], wkv_p.astype(xp.dtype)[None],
            xp[:, -1][None], mem_kv_p[None],
            win_s[0].astype(cache_win0_kv.dtype)[None], win_s[1].astype(cache_win0_kv.dtype)[None],
            win_s[2].astype(cache_win0_kv.dtype)[None], wkv_s.astype(state_wkv.dtype)[None],
            xs[:, -1].astype(state_shift.dtype)[None])
```

```python
import functools

import jax
import jax.numpy as jnp
from jax import lax
from jax.experimental import pallas as pl
from jax.experimental.pallas import tpu as pltpu

D_MODEL = 2048
DEPTH = 1
HEAD_DIM = 64
DIL_GROUPS = ((128, 1), (512, 4), (2048, 16))
HEADS_PER_GROUP = 4
N_ATT_HEADS = HEADS_PER_GROUP * len(DIL_GROUPS)
ATT_WIDTH = N_ATT_HEADS * HEAD_DIM
ATT_OUT = HEADS_PER_GROUP * HEAD_DIM
BAND_BLOCK = 128
RWKV_HEADS = 12
RWKV_HEAD_DIM = 64
RWKV_WIDTH = RWKV_HEADS * RWKV_HEAD_DIM
GN_EPS = 64e-5
MEM_TOKENS = 256
MEM_HEADS = 4
MEM_HEAD_DIM = 128
MEM_WIDTH = MEM_HEADS * MEM_HEAD_DIM
N_BRANCHES = 3
N_EXPERTS = 32
TOP_K = 4
D_EXPERT = 2048
SWIGLU_LIMIT = 7.0
SWIGLU_ALPHA = 1.702
LN_EPS = 1e-5
DEEPNORM_ALPHA = (2.0 * DEPTH) ** 0.25
NEG_INF = -1e30

MOE_TILE_M = 256
MOE_F_CHUNK = 512
MOE_D_CHUNK = 512
MOE_COMBINE_TILE = 64
DMA_ISSUE_UNROLL = 8
VMEM_LIMIT_BYTES = 56 * 1024 * 1024
SUBLANES = 8
RWKV_CHUNK = 64
RWKV_SEQ_PER_STEP = 8


def layer_norm(x, g, b):
    xf = x.astype(jnp.float32)
    mu = jnp.mean(xf, -1, keepdims=True)
    var = jnp.mean(jnp.square(xf - mu), -1, keepdims=True)
    return ((xf - mu) * lax.rsqrt(var + LN_EPS) * g + b).astype(x.dtype)


def alibi_slopes():
    h = jnp.arange(1, N_ATT_HEADS + 1, dtype=jnp.float32)
    return jnp.exp2(-8.0 * h / N_ATT_HEADS)


def project_in(u, lp):
    B, T, _ = u.shape
    p = u @ lp["w_in"]
    heads = lambda t: t.reshape(B, T, N_ATT_HEADS, HEAD_DIM)
    qa = heads(p[..., :ATT_WIDTH])
    ka = heads(p[..., ATT_WIDTH:2 * ATT_WIDTH])
    va = heads(p[..., 2 * ATT_WIDTH:3 * ATT_WIDTH])
    rkv = p[..., 3 * ATT_WIDTH:3 * ATT_WIDTH + 3 * RWKV_WIDTH]
    qm = p[..., 3 * ATT_WIDTH + 3 * RWKV_WIDTH:].reshape(B, T, MEM_HEADS, MEM_HEAD_DIM)
    return qa, ka, va, rkv, qm


def dilated_attention_prompt(q, k, v, dil, n_back, slopes):
    B, T, H, E = q.shape
    L = T // dil
    nb = -(-L // BAND_BLOCK)
    Lp = nb * BAND_BLOCK

    def by_residue(a):
        return a.astype(jnp.float32).reshape(B, L, dil, H, E).transpose(0, 2, 1, 3, 4)

    pad_q = ((0, 0), (0, 0), (0, Lp - L), (0, 0), (0, 0))
    pad_kv = ((0, 0), (0, 0), (BAND_BLOCK, Lp - L), (0, 0), (0, 0))
    qb = jnp.pad(by_residue(q), pad_q).reshape(B, dil, nb, BAND_BLOCK, H, E)

    def key_blocks(a):
        a = jnp.pad(by_residue(a), pad_kv).reshape(B, dil, nb + 1, BAND_BLOCK, H, E)
        return jnp.concatenate([a[:, :, :-1], a[:, :, 1:]], axis=3)

    kb, vb = key_blocks(k), key_blocks(v)
    qi = jnp.arange(BAND_BLOCK)[:, None]
    kj = jnp.arange(2 * BAND_BLOCK)[None, :]
    dist = qi + BAND_BLOCK - kj
    key_row = (jnp.arange(nb) * BAND_BLOCK - BAND_BLOCK)[:, None, None] + kj[None]
    valid = (dist >= 0) & (dist <= n_back) & (key_row >= 0)
    bias = -slopes[:, None, None] * (dist * dil).astype(jnp.float32)
    s = jnp.einsum("brnqhe,brnkhe->brnhqk", qb, kb) * (E ** -0.5) + bias[None, None, None]
    s = jnp.where(valid[None, None, :, None], s, NEG_INF)
    m = jnp.max(s, -1, keepdims=True)
    p = jnp.exp(s - m)
    den = jnp.sum(p, -1, keepdims=True)
    o = jnp.einsum("brnhqk,brnkhe->brnhqe", p, vb) / den
    lse = (m + jnp.log(den))[..., 0]
    o = o.transpose(0, 2, 4, 1, 3, 5).reshape(B, Lp, dil, H, E)[:, :L].reshape(B, T, H, E)
    lse = lse.transpose(0, 2, 4, 1, 3).reshape(B, Lp, dil, H)[:, :L].reshape(B, T, H)
    return o, lse


def dilated_attention_step(q, k_buf, v_buf, k_new, v_new, dil, n_back, slopes):
    Lb = k_buf.shape[1]
    T, E = q.shape[1], q.shape[3]
    k_all = jnp.concatenate([k_buf.astype(k_new.dtype), k_new], axis=1).astype(jnp.float32)
    v_all = jnp.concatenate([v_buf.astype(v_new.dtype), v_new], axis=1).astype(jnp.float32)
    j = jnp.arange(n_back + 1)
    idx = Lb + jnp.arange(T)[:, None] - j[None, :] * dil
    valid = idx >= 0
    idx = jnp.maximum(idx, 0)
    kg = k_all[:, idx]
    vg = v_all[:, idx]
    bias = -slopes[:, None] * (j * dil).astype(jnp.float32)[None, :]
    s = jnp.einsum("bthe,btjhe->bhtj", q.astype(jnp.float32), kg) * (E ** -0.5) + bias[None, :, None, :]
    s = jnp.where(valid[None, None], s, NEG_INF)
    m = jnp.max(s, -1, keepdims=True)
    p = jnp.exp(s - m)
    den = jnp.sum(p, -1, keepdims=True)
    o = jnp.einsum("bhtj,btjhe->bhte", p, vg) / den
    lse = (m + jnp.log(den))[..., 0]
    return o.transpose(0, 2, 1, 3), lse.transpose(0, 2, 1)


def combine_by_denominator(outs, lses):
    o = jnp.stack(outs, 0)
    wts = jax.nn.softmax(jnp.stack(lses, 0), axis=0)
    return jnp.sum(wts[..., None] * o, axis=0)


def _bmm(a, b):
    return lax.dot_general(a, b, (((2,), (1,)), ((0,), (0,))), preferred_element_type=jnp.float32)


def _bmm_nt(a, b):
    return lax.dot_general(a, b, (((2,), (2,)), ((0,), (0,))), preferred_element_type=jnp.float32)


def _bmm_tn(a, b):
    return lax.dot_general(a, b, (((1,), (1,)), ((0,), (0,))), preferred_element_type=jnp.float32)


def _rwkv_chunk_kernel(r_ref, lw_ref, k_ref, v_ref, a_ref, b_ref, s0_ref, o_ref, sT_ref, s_scr,
                       *, n_seq, chunk):
    bf16, f32 = jnp.bfloat16, jnp.float32
    c = pl.program_id(1)
    C = chunk
    H = r_ref.shape[1]

    @pl.when(c == 0)
    def _():
        s_scr[...] = s0_ref[...]

    row = lax.broadcasted_iota(jnp.int32, (C, C), 0)
    col = lax.broadcasted_iota(jnp.int32, (C, C), 1)
    strict = (row > col)[None]
    incl = (row >= col)[None]
    tri = jnp.broadcast_to(jnp.where(incl, 1.0, 0.0).astype(bf16), (H, C, C))

    def one_seq(i, carry):
        r, lw, k, v, a, b = (ref[i] for ref in (r_ref, lw_ref, k_ref, v_ref, a_ref, b_ref))
        s0 = s_scr[i]
        hi = lw.astype(bf16)
        r1 = lw - hi.astype(f32)
        mid = r1.astype(bf16)
        lo = (r1 - mid.astype(f32)).astype(bf16)
        lc = _bmm(tri, hi) + _bmm(tri, mid) + _bmm(tri, lo)
        e_pos = jnp.exp(lc)
        e_neg = jnp.exp(-lc)
        a_t = (a * jnp.exp(lc - lw)).astype(bf16)
        r_t = (r * e_pos).astype(bf16)
        b_t = (b * e_neg).astype(bf16)
        k_t = (k * e_neg).astype(bf16)
        v_b = v.astype(bf16)
        s0_b = s0.astype(bf16)

        a_ab = jnp.where(strict, _bmm_nt(a_t, b_t), 0.0)
        a_ak = jnp.where(strict, _bmm_nt(a_t, k_t), 0.0)
        a_rb = jnp.where(incl, _bmm_nt(r_t, b_t), 0.0)
        a_rk = jnp.where(incl, _bmm_nt(r_t, k_t), 0.0)

        x = _bmm_nt(a_t, s0_b) + _bmm(a_ak.astype(bf16), v_b)
        p = a_ab
        n = 1
        while n < C:
            p_b = p.astype(bf16)
            x = x + _bmm(p_b, x.astype(bf16))
            n *= 2
            if n < C:
                p = _bmm(p_b, p_b)
        u_b = x.astype(bf16)
        o = _bmm_nt(r_t, s0_b) + _bmm(a_rb.astype(bf16), u_b) + _bmm(a_rk.astype(bf16), v_b)
        s1 = s0 + _bmm_tn(u_b, b_t) + _bmm_tn(v_b, k_t)
        s1 = s1 * e_pos[:, C - 1:C, :]
        o_ref[i] = o
        s_scr[i] = s1
        return carry

    lax.fori_loop(0, n_seq, one_seq, 0)

    @pl.when(c == pl.num_programs(1) - 1)
    def _():
        sT_ref[...] = s_scr[...]


def _rwkv_recurrence_call(s0, r, lw, k, v, a, b, *, chunk, n_seq):
    nb, h, t, n = r.shape
    nch = t // chunk
    seq_spec = pl.BlockSpec((n_seq, h, chunk, n), lambda i, c: (i, 0, c, 0))
    st_spec = pl.BlockSpec((n_seq, h, n, n), lambda i, c: (i, 0, 0, 0))
    o, s_t = pl.pallas_call(
        functools.partial(_rwkv_chunk_kernel, n_seq=n_seq, chunk=chunk),
        out_shape=(jax.ShapeDtypeStruct((nb, h, t, n), jnp.float32),
                   jax.ShapeDtypeStruct((nb, h, n, n), jnp.float32)),
        grid=(nb // n_seq, nch),
        in_specs=[seq_spec] * 6 + [st_spec],
        out_specs=(seq_spec, st_spec),
        scratch_shapes=[pltpu.VMEM((n_seq, h, n, n), jnp.float32)],
        compiler_params=pltpu.CompilerParams(
            dimension_semantics=("arbitrary", "arbitrary"),
            vmem_limit_bytes=VMEM_LIMIT_BYTES),
        name="rwkv_chunk",
    )(r, lw, k, v, a, b, s0)
    return s_t, o


def rwkv_recurrence(s0, r, log_w, k, v, a, b):
    bsz, t = r.shape[0], r.shape[1]
    if t >= RWKV_CHUNK:
        chunk, n_seq, t_pad = RWKV_CHUNK, 1, -(-t // RWKV_CHUNK) * RWKV_CHUNK
    else:
        chunk = t_pad = -(-t // SUBLANES) * SUBLANES
        n_seq = RWKV_SEQ_PER_STEP
    to_heads = lambda x: jnp.pad(jnp.swapaxes(x, 1, 2), ((0, 0), (0, 0), (0, t_pad - t), (0, 0)))
    s_t, o = _rwkv_recurrence_call(s0.astype(jnp.float32), *(to_heads(x) for x in (r, log_w, k, v, a, b)),
                                   chunk=chunk, n_seq=n_seq)
    return s_t, jnp.swapaxes(o[:, :, :t], 1, 2)


def rwkv_branch(u, u_last, s0, p_rkv, lp):
    B, T, _ = u.shape
    f32 = jnp.float32
    w_rkv = lp["w_in"][:, 3 * ATT_WIDTH:3 * ATT_WIDTH + 3 * RWKV_WIDTH]
    u_last = u_last.astype(u.dtype)
    u_prev = jnp.concatenate([u_last[:, None], u[:, :-1]], axis=1)
    p_prev = jnp.concatenate([(u_last @ w_rkv)[:, None], p_rkv[:, :-1]], axis=1)
    rkv = p_rkv + (p_prev - p_rkv) * lp["mu_rkv"]
    r, k, v = jnp.split(rkv, 3, axis=-1)
    du = u_prev - u
    mu = lp["mu_wag"]
    xw = u + du * mu[0]
    xa = u + du * mu[1]
    xg = u + du * mu[2]
    w_log = -jax.nn.softplus(-(lp["w0"] + jnp.tanh(xw @ lp["w1"]) @ lp["w2"]).astype(f32)) - 0.5
    log_decay = -jnp.exp(w_log)
    a = jax.nn.sigmoid((lp["a0"] + (xa @ lp["a1"]) @ lp["a2"]).astype(f32))
    g = jax.nn.sigmoid(xg @ lp["g1"]) @ lp["g2"]
    heads = lambda t: t.astype(f32).reshape(B, T, RWKV_HEADS, RWKV_HEAD_DIM)
    kk = heads(k * lp["k_k"])
    kk = kk / jnp.maximum(jnp.sqrt(jnp.sum(kk * kk, -1, keepdims=True)), 1e-12)
    k = heads(k.astype(f32) * (1.0 + (a - 1.0) * lp["k_a"]))
    r, v, log_decay, a = heads(r), heads(v), heads(log_decay), heads(a)
    s_new, o = rwkv_recurrence(s0, r, log_decay, k, v, -kk, kk * a)
    mean = jnp.mean(o, -1, keepdims=True)
    var = jnp.mean(jnp.square(o - mean), -1, keepdims=True)
    on = ((o - mean) * lax.rsqrt(var + GN_EPS)).reshape(B, T, RWKV_WIDTH) * lp["gn_g"] + lp["gn_b"]
    bonus = (jnp.sum(r * k * lp["r_k"], -1, keepdims=True) * v).reshape(B, T, RWKV_WIDTH)
    return (on + bonus) * g, s_new


def memory_attention(q, mk, mv):
    s = jnp.einsum("bthe,bmhe->bhtm", q.astype(jnp.float32), mk.astype(jnp.float32)) * (MEM_HEAD_DIM ** -0.5)
    p = jax.nn.softmax(s, axis=-1)
    return jnp.einsum("bhtm,bmhe->bthe", p, mv.astype(jnp.float32))


def merge_branches(u, o_att, o_rwkv, o_mem, lp):
    B, T, D = u.shape
    dt = u.dtype
    wb = lp["w_branch"]
    y_att = o_att.reshape(B, T, ATT_OUT).astype(dt) @ wb[:ATT_OUT]
    y_rwkv = o_rwkv.astype(dt) @ wb[ATT_OUT:ATT_OUT + RWKV_WIDTH]
    y_mem = o_mem.reshape(B, T, MEM_WIDTH).astype(dt) @ wb[ATT_OUT + RWKV_WIDTH:]
    gate = jax.nn.sigmoid((u @ lp["w_gate"] + lp["b_gate"]).astype(jnp.float32)).astype(dt)
    gate = gate.reshape(B, T, N_BRANCHES, D)
    mixed = gate[:, :, 0] * y_att + gate[:, :, 1] * y_rwkv + gate[:, :, 2] * y_mem
    return mixed @ lp["w_out"]


def mix_prompt(u, mem, lp):
    B, T, _ = u.shape
    qa, ka, va, rkv, qm = project_in(u, lp)
    slopes = alibi_slopes()
    outs, lses, windows = [], [], []
    for g, (win, dil) in enumerate(DIL_GROUPS):
        hs = slice(g * HEADS_PER_GROUP, (g + 1) * HEADS_PER_GROUP)
        o, lse = dilated_attention_prompt(qa[:, :, hs], ka[:, :, hs], va[:, :, hs], dil, win // dil, slopes[hs])
        outs.append(o)
        lses.append(lse)
        keep = min(win, T)
        windows.append(jnp.stack([ka[:, T - keep:, hs], va[:, T - keep:, hs]], axis=2))
    o_att = combine_by_denominator(outs, lses)
    u0 = jnp.zeros((B, D_MODEL), u.dtype)
    s0 = jnp.zeros((B, RWKV_HEADS, RWKV_HEAD_DIM, RWKV_HEAD_DIM), jnp.float32)
    o_rwkv, wkv = rwkv_branch(u, u0, s0, rkv, lp)
    mem_kv = (mem @ lp["w_mem_kv"]).reshape(B, MEM_TOKENS, 2, MEM_HEADS, MEM_HEAD_DIM)
    o_mem = memory_attention(qm, mem_kv[:, :, 0], mem_kv[:, :, 1])
    return merge_branches(u, o_att, o_rwkv, o_mem, lp), windows, wkv, mem_kv


def mix_sample(u, win_caches, mem_kv, s0, u_last, lp):
    qa, ka, va, rkv, qm = project_in(u, lp)
    slopes = alibi_slopes()
    outs, lses, new_rows = [], [], []
    for g, (win, dil) in enumerate(DIL_GROUPS):
        hs = slice(g * HEADS_PER_GROUP, (g + 1) * HEADS_PER_GROUP)
        buf = win_caches[g]
        o, lse = dilated_attention_step(qa[:, :, hs], buf[:, :, 0], buf[:, :, 1],
                                        ka[:, :, hs], va[:, :, hs], dil, win // dil, slopes[hs])
        outs.append(o)
        lses.append(lse)
        new_rows.append(jnp.stack([ka[:, :, hs], va[:, :, hs]], axis=2))
    o_att = combine_by_denominator(outs, lses)
    o_rwkv, wkv = rwkv_branch(u, u_last, s0, rkv, lp)
    o_mem = memory_attention(qm, mem_kv[:, :, 0], mem_kv[:, :, 1])
    return merge_branches(u, o_att, o_rwkv, o_mem, lp), new_rows, wkv


def _moe_up_kernel(tile_expert_ref, tile_valid_ref, x_ref, wg_ref, wu_ref, bg_ref, bu_ref,
                   h_ref, wg_bf, wu_bf):
    i = pl.program_id(1)
    e = tile_expert_ref[i]
    prev = tile_expert_ref[jnp.maximum(i - 1, 0)]

    @pl.when((i == 0) | (e != prev))
    def _():
        wg_bf[...] = wg_ref[0].astype(jnp.bfloat16)
        wu_bf[...] = wu_ref[0].astype(jnp.bfloat16)

    @pl.when(tile_valid_ref[i] != 0)
    def _():
        x = x_ref[...]
        hg = jnp.dot(x, wg_bf[...], preferred_element_type=jnp.float32) + bg_ref[0]
        hu = jnp.dot(x, wu_bf[...], preferred_element_type=jnp.float32) + bu_ref[0]
        glu = jnp.minimum(hg, SWIGLU_LIMIT)
        up = jnp.clip(hu, -SWIGLU_LIMIT, SWIGLU_LIMIT)
        act = (up + 1.0) * glu * jax.nn.sigmoid(SWIGLU_ALPHA * glu)
        h_ref[...] = act.astype(h_ref.dtype)

    @pl.when(tile_valid_ref[i] == 0)
    def _():
        h_ref[...] = jnp.zeros_like(h_ref)


def _moe_down_kernel(tile_expert_ref, tile_valid_ref, h_ref, wd_ref, bd_ref, gate_ref,
                     y_ref, wd_bf):
    i = pl.program_id(1)
    e = tile_expert_ref[i]
    prev = tile_expert_ref[jnp.maximum(i - 1, 0)]

    @pl.when((i == 0) | (e != prev))
    def _():
        wd_bf[...] = wd_ref[0].astype(jnp.bfloat16)

    @pl.when(tile_valid_ref[i] != 0)
    def _():
        y = jnp.dot(h_ref[...], wd_bf[...], preferred_element_type=jnp.float32) + bd_ref[0]
        y_ref[...] = y * gate_ref[...]

    @pl.when(tile_valid_ref[i] == 0)
    def _():
        y_ref[...] = jnp.zeros_like(y_ref)


def _moe_grouped(x_sorted, slot_gate, tile_expert, tile_valid, w_gate_up, b_gate_up, w_down, b_down):
    p_pad, d = x_sorted.shape
    n_exp, _, two_f = w_gate_up.shape
    f = two_f // 2
    tm, fc, dc = MOE_TILE_M, MOE_F_CHUNK, MOE_D_CHUNK
    nt = p_pad // tm
    nfc = f // fc
    ndc = d // dc
    bgu = b_gate_up.reshape(n_exp, 1, two_f)
    bdn = b_down.reshape(n_exp, 1, d)

    h = pl.pallas_call(
        _moe_up_kernel,
        out_shape=jax.ShapeDtypeStruct((p_pad, f), jnp.bfloat16),
        grid_spec=pltpu.PrefetchScalarGridSpec(
            num_scalar_prefetch=2,
            grid=(nfc, nt),
            in_specs=[
                pl.BlockSpec((tm, d), lambda c, i, te, tv: (i, 0)),
                pl.BlockSpec((1, d, fc), lambda c, i, te, tv: (te[i], 0, c)),
                pl.BlockSpec((1, d, fc), lambda c, i, te, tv: (te[i], 0, nfc + c)),
                pl.BlockSpec((1, 1, fc), lambda c, i, te, tv: (te[i], 0, c)),
                pl.BlockSpec((1, 1, fc), lambda c, i, te, tv: (te[i], 0, nfc + c)),
            ],
            out_specs=pl.BlockSpec((tm, fc), lambda c, i, te, tv: (i, c)),
            scratch_shapes=[pltpu.VMEM((d, fc), jnp.bfloat16), pltpu.VMEM((d, fc), jnp.bfloat16)],
        ),
        compiler_params=pltpu.CompilerParams(
            dimension_semantics=("arbitrary", "arbitrary"),
            vmem_limit_bytes=VMEM_LIMIT_BYTES),
        name="moe_up",
    )(tile_expert, tile_valid, x_sorted, w_gate_up, w_gate_up, bgu, bgu)

    y = pl.pallas_call(
        _moe_down_kernel,
        out_shape=jax.ShapeDtypeStruct((p_pad, d), jnp.float32),
        grid_spec=pltpu.PrefetchScalarGridSpec(
            num_scalar_prefetch=2,
            grid=(ndc, nt),
            in_specs=[
                pl.BlockSpec((tm, f), lambda c, i, te, tv: (i, 0)),
                pl.BlockSpec((1, f, dc), lambda c, i, te, tv: (te[i], 0, c)),
                pl.BlockSpec((1, 1, dc), lambda c, i, te, tv: (te[i], 0, c)),
                pl.BlockSpec((tm, 1), lambda c, i, te, tv: (i, 0)),
            ],
            out_specs=pl.BlockSpec((tm, dc), lambda c, i, te, tv: (i, c)),
            scratch_shapes=[pltpu.VMEM((f, dc), jnp.bfloat16)],
        ),
        compiler_params=pltpu.CompilerParams(
            dimension_semantics=("arbitrary", "arbitrary"),
            vmem_limit_bytes=VMEM_LIMIT_BYTES),
        name="moe_down",
    )(tile_expert, tile_valid, h, w_down, bdn, slot_gate)
    return y


def _start_row_gather(idx_ref, first, n_rows, src_hbm, dst, sem):
    def issue(j, carry):
        row = idx_ref[first + j]
        pltpu.make_async_copy(src_hbm.at[pl.ds(row, 1)], dst.at[pl.ds(j, 1)], sem).start()
        return carry
    lax.fori_loop(0, n_rows, issue, 0, unroll=DMA_ISSUE_UNROLL)


def _wait_row_gather(src_hbm, dst, sem):
    pltpu.make_async_copy(src_hbm.at[pl.ds(0, dst.shape[0])], dst, sem).wait()


def _moe_gather_kernel(tok_ref, valid_ref, x_hbm, out_ref, buf, sems):
    i = pl.program_id(0)
    nt = pl.num_programs(0)
    tm = out_ref.shape[0]

    @pl.when((i == 0) & (valid_ref[0] != 0))
    def _():
        _start_row_gather(tok_ref, 0, tm, x_hbm, buf.at[0], sems.at[0])

    nxt = jnp.minimum(i + 1, nt - 1)

    @pl.when((i + 1 < nt) & (valid_ref[nxt] != 0))
    def _():
        _start_row_gather(tok_ref, nxt * tm, tm, x_hbm, buf.at[nxt % 2], sems.at[nxt % 2])

    @pl.when(valid_ref[i] != 0)
    def _():
        slot = i % 2
        _wait_row_gather(x_hbm, buf.at[slot], sems.at[slot])
        out_ref[...] = buf[slot].astype(out_ref.dtype)

    @pl.when(valid_ref[i] == 0)
    def _():
        out_ref[...] = jnp.zeros_like(out_ref)


def _moe_gather_rows(x, slot_token, tile_valid):
    n, d = x.shape
    p_pad = slot_token.shape[0]
    tm = MOE_TILE_M
    return pl.pallas_call(
        _moe_gather_kernel,
        out_shape=jax.ShapeDtypeStruct((p_pad, d), jnp.bfloat16),
        grid_spec=pltpu.PrefetchScalarGridSpec(
            num_scalar_prefetch=2,
            grid=(p_pad // tm,),
            in_specs=[pl.BlockSpec(memory_space=pl.ANY)],
            out_specs=pl.BlockSpec((tm, d), lambda i, tok, tv: (i, 0)),
            scratch_shapes=[pltpu.VMEM((2, tm, d), jnp.float32), pltpu.SemaphoreType.DMA((2,))],
        ),
        compiler_params=pltpu.CompilerParams(
            dimension_semantics=("arbitrary",), vmem_limit_bytes=VMEM_LIMIT_BYTES),
        name="moe_gather",
    )(slot_token, tile_valid, x)


def _moe_combine_kernel(slot_ref, h_ref, g_ref, b_ref, y_hbm, out_ref, buf, sems):
    i = pl.program_id(0)
    nt = pl.num_programs(0)
    tt = h_ref.shape[0]
    rows = TOP_K * tt

    @pl.when(i == 0)
    def _():
        _start_row_gather(slot_ref, 0, rows, y_hbm, buf.at[0], sems.at[0])

    nxt = jnp.minimum(i + 1, nt - 1)

    @pl.when(i + 1 < nt)
    def _():
        _start_row_gather(slot_ref, nxt * rows, rows, y_hbm, buf.at[nxt % 2], sems.at[nxt % 2])

    slot = i % 2
    _wait_row_gather(y_hbm, buf.at[slot], sems.at[slot])
    ffn = buf[slot, 0:tt]
    for k in range(1, TOP_K):
        ffn = ffn + buf[slot, k * tt:(k + 1) * tt]
    x = DEEPNORM_ALPHA * h_ref[...] + ffn
    mu = jnp.mean(x, -1, keepdims=True)
    xc = x - mu
    var = jnp.mean(xc * xc, -1, keepdims=True)
    out_ref[...] = xc * lax.rsqrt(var + LN_EPS) * g_ref[...] + b_ref[...]


def _moe_combine_ln(h, y, pair_slot, ln_g, ln_b):
    n, d = h.shape
    tt = MOE_COMBINE_TILE
    nt = n // tt
    tile_slots = jnp.swapaxes(pair_slot.reshape(nt, tt, TOP_K), 1, 2).reshape(-1)
    return pl.pallas_call(
        _moe_combine_kernel,
        out_shape=jax.ShapeDtypeStruct((n, d), jnp.float32),
        grid_spec=pltpu.PrefetchScalarGridSpec(
            num_scalar_prefetch=1,
            grid=(nt,),
            in_specs=[pl.BlockSpec((tt, d), lambda i, ps: (i, 0)),
                      pl.BlockSpec((1, d), lambda i, ps: (0, 0)),
                      pl.BlockSpec((1, d), lambda i, ps: (0, 0)),
                      pl.BlockSpec(memory_space=pl.ANY)],
            out_specs=pl.BlockSpec((tt, d), lambda i, ps: (i, 0)),
            scratch_shapes=[pltpu.VMEM((2, TOP_K * tt, d), jnp.float32), pltpu.SemaphoreType.DMA((2,))],
        ),
        compiler_params=pltpu.CompilerParams(
            dimension_semantics=("arbitrary",), vmem_limit_bytes=VMEM_LIMIT_BYTES),
        name="moe_combine",
    )(tile_slots, h, ln_g.reshape(1, d), ln_b.reshape(1, d), y)


def moe_ffn_ln(x, w_router, b_router, w_gate_up, b_gate_up, w_down, b_down, ln_g, ln_b):
    n, d = x.shape
    tm = MOE_TILE_M
    logits = (x @ w_router + b_router).astype(jnp.float32)
    top_logit, top_idx = lax.top_k(logits, TOP_K)
    top_w = jax.nn.softmax(top_logit, axis=-1)

    n_pairs = n * TOP_K
    nt = n_pairs // tm + N_EXPERTS
    p_pad = nt * tm
    pair_expert = top_idx.reshape(n_pairs)
    pair_token = jnp.repeat(jnp.arange(n, dtype=jnp.int32), TOP_K)
    pair_w = top_w.reshape(n_pairs)
    order = jnp.argsort(pair_expert, stable=True)
    sorted_expert = pair_expert[order]
    counts = jnp.sum(jax.nn.one_hot(pair_expert, N_EXPERTS, dtype=jnp.int32), axis=0)
    tiles_per = (counts + tm - 1) // tm
    tile_start = jnp.cumsum(tiles_per) - tiles_per
    group_start = jnp.cumsum(counts) - counts
    rank = jnp.arange(n_pairs, dtype=jnp.int32) - group_start[sorted_expert]
    dest_sorted = tile_start[sorted_expert] * tm + rank
    slot_token = jnp.zeros((p_pad,), jnp.int32).at[dest_sorted].set(pair_token[order])
    slot_gate = jnp.zeros((p_pad,), jnp.float32).at[dest_sorted].set(pair_w[order])
    pair_slot = jnp.zeros((n_pairs,), jnp.int32).at[order].set(dest_sorted).reshape(n, TOP_K)
    total_tiles = jnp.sum(tiles_per)
    tile_ids = jnp.arange(nt, dtype=jnp.int32)
    tile_end = jnp.cumsum(tiles_per)
    tile_expert = jnp.minimum(jnp.sum((tile_ids[:, None] >= tile_end[None, :]).astype(jnp.int32), axis=1),
                              N_EXPERTS - 1).astype(jnp.int32)
    tile_valid = (tile_ids < total_tiles).astype(jnp.int32)

    x_sorted = _moe_gather_rows(x, slot_token, tile_valid)
    y = _moe_grouped(x_sorted, slot_gate.reshape(p_pad, 1), tile_expert, tile_valid,
                     w_gate_up, b_gate_up, w_down, b_down)
    return _moe_combine_ln(x, y, pair_slot, ln_g, ln_b)


def kernel(x_prompt, x_sample, cache_win0_kv, cache_win1_kv, cache_win2_kv, cache_mem_kv,
           state_wkv, state_shift, mem_prompt,
           w_in, w_gate, b_gate, w_branch, w_out, w_mem_kv,
           mu_rkv, mu_wag, w0, w1, w2, a0, a1, a2, g1, g2, k_k, k_a, r_k, gn_g, gn_b,
           ln1_g, ln1_b, w_router, b_router, w_gate_up, b_gate_up, w_down, b_down, ln2_g, ln2_b):
    xp, xs = x_prompt, x_sample
    bp, tp, d = xp.shape
    bs, ts, _ = xs.shape
    n_prompt = bp * tp
    l = 0
    lp = {"w_in": w_in[l], "w_gate": w_gate[l], "b_gate": b_gate[l], "w_branch": w_branch[l],
          "w_out": w_out[l], "w_mem_kv": w_mem_kv[l], "mu_rkv": mu_rkv[l], "mu_wag": mu_wag[l],
          "w0": w0[l], "w1": w1[l], "w2": w2[l], "a0": a0[l], "a1": a1[l], "a2": a2[l],
          "g1": g1[l], "g2": g2[l], "k_k": k_k[l], "k_a": k_a[l], "r_k": r_k[l],
          "gn_g": gn_g[l], "gn_b": gn_b[l]}
    mix_p, win_p, wkv_p, mem_kv_p = mix_prompt(xp, mem_prompt, lp)
    mix_s, win_s, wkv_s = mix_sample(xs, (cache_win0_kv[l], cache_win1_kv[l], cache_win2_kv[l]),
                                     cache_mem_kv[l], state_wkv[l], state_shift[l], lp)
    hp = layer_norm(DEEPNORM_ALPHA * xp + mix_p, ln1_g[l], ln1_b[l])
    hs = layer_norm(DEEPNORM_ALPHA * xs + mix_s, ln1_g[l], ln1_b[l])
    tokens = jnp.concatenate([hp.reshape(-1, d), hs.reshape(-1, d)], axis=0)
    y_all = moe_ffn_ln(tokens, w_router[l], b_router[l], w_gate_up[l], b_gate_up[l], w_down[l], b_down[l],
                       ln2_g[l], ln2_b[l])
    yp = y_all[:n_prompt].reshape(bp, tp, d)
    ys = y_all[n_prompt:].reshape(bs, ts, d)
    return (yp, ys,
            win_p[0][None], win_p[1][None], win_p[2][None], wkv_p.astype(xp.dtype)[None],
            xp[:, -1][None], mem_kv_p[None],
            win_s[0].astype(cache_win0_kv.dtype)[None], win_s[1].astype(cache_win0_kv.dtype)[None],
            win_s[2].astype(cache_win0_kv.dtype)[None], wkv_s.astype(state_wkv.dtype)[None],
            xs[:, -1].astype(state_shift.dtype)[None])
```

```python
import functools

import jax
import jax.numpy as jnp
from jax import lax
from jax.experimental import pallas as pl
from jax.experimental.pallas import tpu as pltpu

D_MODEL = 2048
DEPTH = 1
HEAD_DIM = 64
DIL_GROUPS = ((128, 1), (512, 4), (2048, 16))
HEADS_PER_GROUP = 4
N_ATT_HEADS = HEADS_PER_GROUP * len(DIL_GROUPS)
ATT_WIDTH = N_ATT_HEADS * HEAD_DIM
ATT_OUT = HEADS_PER_GROUP * HEAD_DIM
BAND_BLOCK = 128
N_BACK = 128
GROUP_WIDTH = HEADS_PER_GROUP * HEAD_DIM
RWKV_HEADS = 12
RWKV_HEAD_DIM = 64
RWKV_WIDTH = RWKV_HEADS * RWKV_HEAD_DIM
GN_EPS = 64e-5
MEM_TOKENS = 256
MEM_HEADS = 4
MEM_HEAD_DIM = 128
MEM_WIDTH = MEM_HEADS * MEM_HEAD_DIM
PROJ_WIDTH = 3 * ATT_WIDTH + 3 * RWKV_WIDTH + MEM_WIDTH
N_BRANCHES = 3
N_EXPERTS = 32
TOP_K = 4
D_EXPERT = 2048
SWIGLU_LIMIT = 7.0
SWIGLU_ALPHA = 1.702
LN_EPS = 1e-5
DEEPNORM_ALPHA = (2.0 * DEPTH) ** 0.25
NEG_INF = -1e30

VMEM_LIMIT_BYTES = 56 * 1024 * 1024
SUBLANES = 8
LANES = 128
MOE_TILE_M = 256
MOE_F_CHUNK = 512
MOE_D_CHUNK = 512
MOE_COMBINE_TILE = 64
DMA_ISSUE_UNROLL = 8
RWKV_CHUNK = 64
RWKV_SEQ_PER_STEP = 8
PROJ_TILE_M = 1088
MATMUL_TILE_N = 512
MERGE_TILE_M = 256
ATT_COMBINE_TILE = 512
MEM_ATTN_TILE_Q = 512


def _matmul_kernel(x_ref, w_ref, b_ref, o_ref, *, sigmoid):
    y = jnp.dot(x_ref[...], w_ref[...], preferred_element_type=jnp.float32) + b_ref[...]
    o_ref[...] = jax.nn.sigmoid(y) if sigmoid else y


def matmul_bias(x, w, bias=None, *, tile_m, sigmoid=False, name="matmul"):
    m, k = x.shape
    n = w.shape[1]
    tn = MATMUL_TILE_N if n % MATMUL_TILE_N == 0 else n
    if bias is None:
        bias = jnp.zeros((n,), jnp.float32)
    return pl.pallas_call(
        functools.partial(_matmul_kernel, sigmoid=sigmoid),
        out_shape=jax.ShapeDtypeStruct((m, n), jnp.float32),
        grid=(n // tn, m // tile_m),
        in_specs=[pl.BlockSpec((tile_m, k), lambda j, i: (i, 0)),
                  pl.BlockSpec((k, tn), lambda j, i: (0, j)),
                  pl.BlockSpec((1, tn), lambda j, i: (0, j))],
        out_specs=pl.BlockSpec((tile_m, tn), lambda j, i: (i, j)),
        compiler_params=pltpu.CompilerParams(
            dimension_semantics=("arbitrary", "arbitrary"), vmem_limit_bytes=VMEM_LIMIT_BYTES),
        name=name,
    )(x, w, bias.reshape(1, n))


def _merge_kernel(att_ref, rwkv_ref, mem_ref, gate_ref, x_ref, wb_ref, wo_ref, g_ref, b_ref,
                  wr_hi_ref, wr_lo_ref, br_ref, h_ref, topw_ref, topi_ref):
    bf16, f32 = jnp.bfloat16, jnp.float32
    d = x_ref.shape[1]
    dot = functools.partial(jnp.dot, preferred_element_type=f32)
    y = dot(att_ref[...].astype(bf16), wb_ref[0:ATT_OUT]) * gate_ref[:, 0:d]
    y = y + dot(rwkv_ref[...].astype(bf16), wb_ref[ATT_OUT:ATT_OUT + RWKV_WIDTH]) * gate_ref[:, d:2 * d]
    y = y + dot(mem_ref[...].astype(bf16), wb_ref[ATT_OUT + RWKV_WIDTH:]) * gate_ref[:, 2 * d:]
    z = DEEPNORM_ALPHA * x_ref[...] + dot(y.astype(bf16), wo_ref[...])
    mu = jnp.mean(z, -1, keepdims=True)
    zc = z - mu
    var = jnp.mean(zc * zc, -1, keepdims=True)
    h = zc * lax.rsqrt(var + LN_EPS) * g_ref[...] + b_ref[...]
    h_ref[...] = h

    h_hi = h.astype(bf16)
    h_lo = (h - h_hi.astype(f32)).astype(bf16)
    logits = (dot(h_hi, wr_hi_ref[...]) + dot(h_hi, wr_lo_ref[...]) + dot(h_lo, wr_hi_ref[...])) + br_ref[...]
    lane = lax.broadcasted_iota(jnp.int32, logits.shape, 1)
    work = jnp.where(lane < N_EXPERTS, logits, -jnp.inf)
    tops, ids = [], []
    for _ in range(TOP_K):
        m = jnp.max(work, -1, keepdims=True)
        idx = jnp.min(jnp.where(work == m, lane, LANES), -1, keepdims=True)
        tops.append(m)
        ids.append(idx)
        work = jnp.where(lane == idx, -jnp.inf, work)
    e = [jnp.exp(t - tops[0]) for t in tops]
    den = e[0] + e[1] + e[2] + e[3]
    tw = jnp.zeros(logits.shape, f32)
    ti = jnp.zeros(logits.shape, jnp.int32)
    for k in range(TOP_K):
        tw = jnp.where(lane == k, e[k] / den, tw)
        ti = jnp.where(lane == k, ids[k], ti)
    topw_ref[...] = tw
    topi_ref[...] = ti


def merge_ln_router(o_att, o_rwkv, o_mem, gate, x, w_branch, w_out, ln_g, ln_b, w_router, b_router):
    n, d = x.shape
    tm = MERGE_TILE_M
    wr = jnp.zeros((d, LANES), jnp.float32).at[:, :N_EXPERTS].set(w_router)
    wr_hi = wr.astype(jnp.bfloat16)
    wr_lo = (wr - wr_hi.astype(jnp.float32)).astype(jnp.bfloat16)
    br = jnp.zeros((1, LANES), jnp.float32).at[0, :N_EXPERTS].set(b_router)
    rows = lambda w: pl.BlockSpec((tm, w), lambda i: (i, 0))
    full = lambda a: pl.BlockSpec(a.shape, lambda i: (0, 0))
    h, topw, topi = pl.pallas_call(
        _merge_kernel,
        out_shape=(jax.ShapeDtypeStruct((n, d), jnp.float32),
                   jax.ShapeDtypeStruct((n, LANES), jnp.float32),
                   jax.ShapeDtypeStruct((n, LANES), jnp.int32)),
        grid=(n // tm,),
        in_specs=[rows(ATT_OUT), rows(RWKV_WIDTH), rows(MEM_WIDTH), rows(3 * d), rows(d),
                  full(w_branch), full(w_out), pl.BlockSpec((1, d), lambda i: (0, 0)),
                  pl.BlockSpec((1, d), lambda i: (0, 0)), full(wr_hi), full(wr_lo), full(br)],
        out_specs=(rows(d), rows(LANES), rows(LANES)),
        compiler_params=pltpu.CompilerParams(
            dimension_semantics=("arbitrary",), vmem_limit_bytes=VMEM_LIMIT_BYTES),
        name="merge_ln_router",
    )(o_att, o_rwkv, o_mem, gate, x, w_branch, w_out, ln_g.reshape(1, d), ln_b.reshape(1, d), wr_hi, wr_lo, br)
    return h, topw[:, :TOP_K], topi[:, :TOP_K]


def _alibi_slope(head):
    return 2.0 ** (-8.0 * (head + 1) / N_ATT_HEADS)


def _dot_nt(a, b):
    return lax.dot_general(a, b, (((1,), (1,)), ((), ())), preferred_element_type=jnp.float32)


def _dil_attn_prompt_kernel(q_ref, kc_ref, kp_ref, vc_ref, vp_ref, o_ref, lse_ref, *, group, dil):
    bf16 = jnp.bfloat16
    blk = BAND_BLOCK
    i = pl.program_id(1)
    qi = lax.broadcasted_iota(jnp.int32, (blk, blk), 0)
    kj = lax.broadcasted_iota(jnp.int32, (blk, blk), 1)
    dist_c = qi - kj
    dist_p = dist_c + blk
    valid_c = dist_c >= 0
    valid_p = (dist_p <= blk) & (i > 0)
    scale = HEAD_DIM ** -0.5
    for h in range(HEADS_PER_GROUP):
        slope = _alibi_slope(group * HEADS_PER_GROUP + h) * dil
        q = q_ref[0, h].astype(bf16)
        s_c = _dot_nt(q, kc_ref[0, h].astype(bf16)) * scale - slope * dist_c.astype(jnp.float32)
        s_p = _dot_nt(q, kp_ref[0, h].astype(bf16)) * scale - slope * dist_p.astype(jnp.float32)
        s_c = jnp.where(valid_c, s_c, NEG_INF)
        s_p = jnp.where(valid_p, s_p, NEG_INF)
        m = jnp.maximum(jnp.max(s_c, -1, keepdims=True), jnp.max(s_p, -1, keepdims=True))
        p_c = jnp.exp(s_c - m)
        p_p = jnp.exp(s_p - m)
        den = jnp.sum(p_c, -1, keepdims=True) + jnp.sum(p_p, -1, keepdims=True)
        pv = (jnp.dot(p_c.astype(bf16), vc_ref[0, h].astype(bf16), preferred_element_type=jnp.float32)
              + jnp.dot(p_p.astype(bf16), vp_ref[0, h].astype(bf16), preferred_element_type=jnp.float32))
        o_ref[0, h] = pv / den
        lse_ref[0, h] = jnp.broadcast_to(m + jnp.log(den), (blk, HEAD_DIM))


def dilated_attention_prompt(q, k, v, group, dil):
    b, t, h, e = q.shape
    l = t // dil
    assert l % BAND_BLOCK == 0

    def by_residue(a):
        return a.reshape(b, l, dil, h, e).transpose(0, 2, 3, 1, 4).reshape(b * dil, h, l, e)

    def from_residue(a):
        return a.reshape(b, dil, h, l, e).transpose(0, 3, 1, 2, 4).reshape(b, t, h, e)

    cur = pl.BlockSpec((1, h, BAND_BLOCK, e), lambda n, i: (n, 0, i, 0))
    prev = pl.BlockSpec((1, h, BAND_BLOCK, e), lambda n, i: (n, 0, jnp.maximum(i - 1, 0), 0))
    qr, kr, vr = by_residue(q), by_residue(k), by_residue(v)
    o, lse = pl.pallas_call(
        functools.partial(_dil_attn_prompt_kernel, group=group, dil=dil),
        out_shape=(jax.ShapeDtypeStruct(qr.shape, jnp.float32), jax.ShapeDtypeStruct(qr.shape, jnp.float32)),
        grid=(b * dil, l // BAND_BLOCK),
        in_specs=[cur, cur, prev, cur, prev],
        out_specs=(cur, cur),
        compiler_params=pltpu.CompilerParams(
            dimension_semantics=("arbitrary", "arbitrary"), vmem_limit_bytes=VMEM_LIMIT_BYTES),
        name="dil_attn_prompt_g%d" % group,
    )(qr, kr, kr, vr, vr)
    return from_residue(o), from_residue(lse)


def _att_combine_kernel(o0, o1, o2, l0, l1, l2, out_ref):
    a, b, c = l0[...], l1[...], l2[...]
    m = jnp.maximum(jnp.maximum(a, b), c)
    ea, eb, ec = jnp.exp(a - m), jnp.exp(b - m), jnp.exp(c - m)
    out_ref[...] = (ea * o0[...] + eb * o1[...] + ec * o2[...]) / (ea + eb + ec)


def att_combine(outs, lses):
    n, w = outs[0].shape
    spec = pl.BlockSpec((ATT_COMBINE_TILE, w), lambda i: (i, 0))
    return pl.pallas_call(
        _att_combine_kernel,
        out_shape=jax.ShapeDtypeStruct((n, w), jnp.float32),
        grid=(n // ATT_COMBINE_TILE,),
        in_specs=[spec] * 6,
        out_specs=spec,
        compiler_params=pltpu.CompilerParams(dimension_semantics=("arbitrary",)),
        name="att_combine",
    )(*outs, *lses)


def _mem_attn_kernel(q_ref, kv_ref, o_ref):
    bf16 = jnp.bfloat16
    e = MEM_HEAD_DIM
    scale = e ** -0.5
    for h in range(MEM_HEADS):
        q = q_ref[0, :, h * e:(h + 1) * e].astype(bf16)
        k = kv_ref[0, :, h * e:(h + 1) * e].astype(bf16)
        v = kv_ref[0, :, (MEM_HEADS + h) * e:(MEM_HEADS + h + 1) * e].astype(bf16)
        s = _dot_nt(q, k) * scale
        m = jnp.max(s, -1, keepdims=True)
        p = jnp.exp(s - m)
        p = p / jnp.sum(p, -1, keepdims=True)
        o_ref[0, :, h * e:(h + 1) * e] = jnp.dot(p.astype(bf16), v, preferred_element_type=jnp.float32)


def memory_attention(q, mem_kv):
    b, t, w = q.shape
    tq = min(t, MEM_ATTN_TILE_Q)
    return pl.pallas_call(
        _mem_attn_kernel,
        out_shape=jax.ShapeDtypeStruct((b, t, w), jnp.float32),
        grid=(b, t // tq),
        in_specs=[pl.BlockSpec((1, tq, w), lambda n, i: (n, i, 0)),
                  pl.BlockSpec((1, MEM_TOKENS, 2 * w), lambda n, i: (n, 0, 0))],
        out_specs=pl.BlockSpec((1, tq, w), lambda n, i: (n, i, 0)),
        compiler_params=pltpu.CompilerParams(
            dimension_semantics=("arbitrary", "arbitrary"), vmem_limit_bytes=VMEM_LIMIT_BYTES),
        name="mem_attn",
    )(q, mem_kv)


def _dil_attn_step_kernel(qkv_ref, c0_ref, c1_ref, c2_ref, o_ref, *, n_new):
    bf16, f32 = jnp.bfloat16, jnp.float32
    gw = GROUP_WIDTH
    sub = SUBLANES
    scale = HEAD_DIM ** -0.5
    r = lax.broadcasted_iota(jnp.int32, (gw, gw), 0) // HEAD_DIM
    c = lax.broadcasted_iota(jnp.int32, (gw, gw), 1) // HEAD_DIM
    seg_ones = jnp.where(r == c, 1.0, 0.0).astype(bf16)

    def seg_sum(x):
        flat = x.reshape(-1, gw).astype(bf16)
        return jnp.dot(flat, seg_ones, preferred_element_type=f32).reshape(x.shape)

    lane_head = lax.broadcasted_iota(jnp.int32, (1, gw), 1) // HEAD_DIM
    row8 = lax.broadcasted_iota(jnp.int32, (sub, gw), 0)
    pad_rows = lambda x: jnp.concatenate([x, jnp.zeros((sub - n_new, gw), f32)], axis=0)

    def slope_row(g, dil):
        out = jnp.zeros((1, gw), f32)
        for h in range(HEADS_PER_GROUP):
            out = jnp.where(lane_head == h, _alibi_slope(g * HEADS_PER_GROUP + h) * dil, out)
        return out

    outs, lses = [], []
    for g, (_, dil) in enumerate(DIL_GROUPS):
        slope = slope_row(g, dil)
        q4 = qkv_ref[0, :, g * gw:(g + 1) * gw]
        k_new = pad_rows(qkv_ref[0, :, ATT_WIDTH + g * gw:ATT_WIDTH + (g + 1) * gw])
        v_new = pad_rows(qkv_ref[0, :, 2 * ATT_WIDTH + g * gw:2 * ATT_WIDTH + (g + 1) * gw])
        x = (c0_ref, c1_ref, c2_ref)[g][0]
        k_b, v_b = x[:, :, :gw], x[:, :, gw:]
        a_idx = lax.broadcasted_iota(jnp.int32, k_b.shape, 0)
        s_idx = lax.broadcasted_iota(jnp.int32, k_b.shape, 1)
        if dil == 1:
            jj = a_idx * sub + s_idx
            o_g = jnp.zeros((sub, gw), f32)
            l_g = jnp.zeros((sub, gw), f32)
            all_rows = lambda v, op: jnp.broadcast_to(
                (jnp.max if op is jnp.maximum else jnp.sum)(v, axis=0, keepdims=True), v.shape)
            for t in range(n_new):
                q_t = q4[t:t + 1]
                s_b = seg_sum(k_b * q_t[None]) * scale - slope[None] * (N_BACK + t - jj).astype(f32)
                s_b = jnp.where(jj >= t, s_b, NEG_INF)
                s_n = seg_sum(k_new * q_t) * scale - slope * (t - row8).astype(f32)
                s_n = jnp.where(row8 <= t, s_n, NEG_INF)
                s_all = jnp.concatenate([s_b, s_n[None]], axis=0)
                v_all = jnp.concatenate([v_b, v_new[None]], axis=0)
                m = all_rows(jnp.max(s_all, axis=0), jnp.maximum)
                p = jnp.exp(s_all - m[None])
                den = all_rows(jnp.sum(p, axis=0), jnp.add)
                pv = all_rows(jnp.sum(p * v_all, axis=0), jnp.add)
                o_g = jnp.where(row8 == t, pv / den, o_g)
                l_g = jnp.where(row8 == t, m + jnp.log(den), l_g)
        else:
            per_q = dil if dil < sub else sub
            if per_q < sub:
                q8 = jnp.concatenate([q4] * (sub // per_q), axis=0)
                jj = a_idx * (sub // per_q) + s_idx // per_q
                valid = s_idx >= 0
                fold = lambda v, op: op(v, pltpu.roll(v, per_q, 0))
            else:
                q8 = pad_rows(q4)
                jj = a_idx
                valid = s_idx < n_new
                fold = lambda v, op: v
            s_b = seg_sum(k_b * q8[None]) * scale - slope[None] * (N_BACK - jj).astype(f32)
            s_b = jnp.where(valid, s_b, NEG_INF)
            s_n = jnp.where(row8 < n_new, seg_sum(k_new * pad_rows(q4)) * scale, NEG_INF)
            m = fold(jnp.maximum(jnp.max(s_b, axis=0), s_n), jnp.maximum)
            p_b = jnp.exp(s_b - m[None])
            p_n = jnp.exp(s_n - m)
            den = fold(jnp.sum(p_b, axis=0), jnp.add) + p_n
            pv = fold(jnp.sum(p_b * v_b, axis=0), jnp.add) + p_n * v_new
            o_g, l_g = pv / den, m + jnp.log(den)
        outs.append(o_g)
        lses.append(l_g)
    mm = jnp.maximum(jnp.maximum(lses[0], lses[1]), lses[2])
    w = [jnp.exp(l - mm) for l in lses]
    merged = (w[0] * outs[0] + w[1] * outs[1] + w[2] * outs[2]) / (w[0] + w[1] + w[2])
    o_ref[0] = merged[:n_new]


def dilated_attention_step(p_s, cache0, cache1, cache2):
    b, t, _ = p_s.shape
    row = 2 * GROUP_WIDTH
    sub = SUBLANES
    assert t <= DIL_GROUPS[1][1] and t <= sub
    c0 = cache0.reshape(b, N_BACK // sub, sub, row)
    c1 = cache1.reshape(b, N_BACK * DIL_GROUPS[1][1] // sub, sub, row)
    c2 = cache2.reshape(b, N_BACK, DIL_GROUPS[2][1], row)
    blk = lambda rows: pl.BlockSpec((1, rows, sub, row), lambda n: (n, 0, 0, 0))
    return pl.pallas_call(
        functools.partial(_dil_attn_step_kernel, n_new=t),
        out_shape=jax.ShapeDtypeStruct((b, t, GROUP_WIDTH), jnp.float32),
        grid=(b,),
        in_specs=[pl.BlockSpec((1, t, 3 * ATT_WIDTH), lambda n: (n, 0, 0)),
                  blk(c0.shape[1]), blk(c1.shape[1]), blk(c2.shape[1])],
        out_specs=pl.BlockSpec((1, t, GROUP_WIDTH), lambda n: (n, 0, 0)),
        compiler_params=pltpu.CompilerParams(
            dimension_semantics=("arbitrary",), vmem_limit_bytes=VMEM_LIMIT_BYTES),
        name="dil_attn_step",
    )(p_s, c0, c1, c2)


def _bmm(a, b):
    return lax.dot_general(a, b, (((2,), (1,)), ((0,), (0,))), preferred_element_type=jnp.float32)


def _bmm_nt(a, b):
    return lax.dot_general(a, b, (((2,), (2,)), ((0,), (0,))), preferred_element_type=jnp.float32)


def _bmm_tn(a, b):
    return lax.dot_general(a, b, (((1,), (1,)), ((0,), (0,))), preferred_element_type=jnp.float32)


def _rwkv_chunk_kernel(r_ref, lw_ref, k_ref, v_ref, a_ref, b_ref, s0_ref, o_ref, sT_ref, s_scr,
                       *, n_seq, chunk):
    bf16, f32 = jnp.bfloat16, jnp.float32
    c = pl.program_id(1)
    C = chunk
    H = r_ref.shape[1]

    @pl.when(c == 0)
    def _():
        s_scr[...] = s0_ref[...]

    row = lax.broadcasted_iota(jnp.int32, (C, C), 0)
    col = lax.broadcasted_iota(jnp.int32, (C, C), 1)
    strict = (row > col)[None]
    incl = (row >= col)[None]
    tri = jnp.broadcast_to(jnp.where(incl, 1.0, 0.0).astype(bf16), (H, C, C))

    def one_seq(i, carry):
        r, lw, k, v, a, b = (ref[i] for ref in (r_ref, lw_ref, k_ref, v_ref, a_ref, b_ref))
        s0 = s_scr[i]
        hi = lw.astype(bf16)
        r1 = lw - hi.astype(f32)
        mid = r1.astype(bf16)
        lo = (r1 - mid.astype(f32)).astype(bf16)
        lc = _bmm(tri, hi) + _bmm(tri, mid) + _bmm(tri, lo)
        e_pos = jnp.exp(lc)
        e_neg = jnp.exp(-lc)
        a_t = (a * jnp.exp(lc - lw)).astype(bf16)
        r_t = (r * e_pos).astype(bf16)
        b_t = (b * e_neg).astype(bf16)
        k_t = (k * e_neg).astype(bf16)
        v_b = v.astype(bf16)
        s0_b = s0.astype(bf16)

        a_ab = jnp.where(strict, _bmm_nt(a_t, b_t), 0.0)
        a_ak = jnp.where(strict, _bmm_nt(a_t, k_t), 0.0)
        a_rb = jnp.where(incl, _bmm_nt(r_t, b_t), 0.0)
        a_rk = jnp.where(incl, _bmm_nt(r_t, k_t), 0.0)

        x = _bmm_nt(a_t, s0_b) + _bmm(a_ak.astype(bf16), v_b)
        p = a_ab
        n = 1
        while n < C:
            p_b = p.astype(bf16)
            x = x + _bmm(p_b, x.astype(bf16))
            n *= 2
            if n < C:
                p = _bmm(p_b, p_b)
        u_b = x.astype(bf16)
        o = _bmm_nt(r_t, s0_b) + _bmm(a_rb.astype(bf16), u_b) + _bmm(a_rk.astype(bf16), v_b)
        s1 = s0 + _bmm_tn(u_b, b_t) + _bmm_tn(v_b, k_t)
        s1 = s1 * e_pos[:, C - 1:C, :]
        o_ref[i] = o
        s_scr[i] = s1
        return carry

    lax.fori_loop(0, n_seq, one_seq, 0)

    @pl.when(c == pl.num_programs(1) - 1)
    def _():
        sT_ref[...] = s_scr[...]


def _rwkv_recurrence_call(s0, r, lw, k, v, a, b, *, chunk, n_seq):
    nb, h, t, n = r.shape
    nch = t // chunk
    seq_spec = pl.BlockSpec((n_seq, h, chunk, n), lambda i, c: (i, 0, c, 0))
    st_spec = pl.BlockSpec((n_seq, h, n, n), lambda i, c: (i, 0, 0, 0))
    o, s_t = pl.pallas_call(
        functools.partial(_rwkv_chunk_kernel, n_seq=n_seq, chunk=chunk),
        out_shape=(jax.ShapeDtypeStruct((nb, h, t, n), jnp.float32),
                   jax.ShapeDtypeStruct((nb, h, n, n), jnp.float32)),
        grid=(nb // n_seq, nch),
        in_specs=[seq_spec] * 6 + [st_spec],
        out_specs=(seq_spec, st_spec),
        scratch_shapes=[pltpu.VMEM((n_seq, h, n, n), jnp.float32)],
        compiler_params=pltpu.CompilerParams(
            dimension_semantics=("arbitrary", "arbitrary"),
            vmem_limit_bytes=VMEM_LIMIT_BYTES),
        name="rwkv_chunk",
    )(r, lw, k, v, a, b, s0)
    return s_t, o


def rwkv_recurrence(s0, r, log_w, k, v, a, b):
    bsz, t = r.shape[0], r.shape[1]
    if t >= RWKV_CHUNK:
        chunk, n_seq, t_pad = RWKV_CHUNK, 1, -(-t // RWKV_CHUNK) * RWKV_CHUNK
    else:
        chunk = t_pad = -(-t // SUBLANES) * SUBLANES
        n_seq = RWKV_SEQ_PER_STEP
    to_heads = lambda x: jnp.pad(jnp.swapaxes(x, 1, 2), ((0, 0), (0, 0), (0, t_pad - t), (0, 0)))
    s_t, o = _rwkv_recurrence_call(s0.astype(jnp.float32), *(to_heads(x) for x in (r, log_w, k, v, a, b)),
                                   chunk=chunk, n_seq=n_seq)
    return s_t, jnp.swapaxes(o[:, :, :t], 1, 2)


def rwkv_branch(u, u_last, s0, p_rkv, lp):
    B, T, _ = u.shape
    f32 = jnp.float32
    w_rkv = lp["w_in"][:, 3 * ATT_WIDTH:3 * ATT_WIDTH + 3 * RWKV_WIDTH]
    u_last = u_last.astype(u.dtype)
    u_prev = jnp.concatenate([u_last[:, None], u[:, :-1]], axis=1)
    p_prev = jnp.concatenate([(u_last @ w_rkv)[:, None], p_rkv[:, :-1]], axis=1)
    rkv = p_rkv + (p_prev - p_rkv) * lp["mu_rkv"]
    r, k, v = jnp.split(rkv, 3, axis=-1)
    du = u_prev - u
    mu = lp["mu_wag"]
    xw = u + du * mu[0]
    xa = u + du * mu[1]
    xg = u + du * mu[2]
    w_log = -jax.nn.softplus(-(lp["w0"] + jnp.tanh(xw @ lp["w1"]) @ lp["w2"]).astype(f32)) - 0.5
    log_decay = -jnp.exp(w_log)
    a = jax.nn.sigmoid((lp["a0"] + (xa @ lp["a1"]) @ lp["a2"]).astype(f32))
    g = jax.nn.sigmoid(xg @ lp["g1"]) @ lp["g2"]
    heads = lambda t: t.astype(f32).reshape(B, T, RWKV_HEADS, RWKV_HEAD_DIM)
    kk = heads(k * lp["k_k"])
    kk = kk / jnp.maximum(jnp.sqrt(jnp.sum(kk * kk, -1, keepdims=True)), 1e-12)
    k = heads(k.astype(f32) * (1.0 + (a - 1.0) * lp["k_a"]))
    r, v, log_decay, a = heads(r), heads(v), heads(log_decay), heads(a)
    s_new, o = rwkv_recurrence(s0, r, log_decay, k, v, -kk, kk * a)
    mean = jnp.mean(o, -1, keepdims=True)
    var = jnp.mean(jnp.square(o - mean), -1, keepdims=True)
    on = ((o - mean) * lax.rsqrt(var + GN_EPS)).reshape(B, T, RWKV_WIDTH) * lp["gn_g"] + lp["gn_b"]
    bonus = (jnp.sum(r * k * lp["r_k"], -1, keepdims=True) * v).reshape(B, T, RWKV_WIDTH)
    return (on + bonus) * g, s_new


def _moe_up_kernel(tile_expert_ref, tile_valid_ref, x_ref, wg_ref, wu_ref, bg_ref, bu_ref,
                   h_ref, wg_bf, wu_bf):
    i = pl.program_id(1)
    e = tile_expert_ref[i]
    prev = tile_expert_ref[jnp.maximum(i - 1, 0)]

    @pl.when((i == 0) | (e != prev))
    def _():
        wg_bf[...] = wg_ref[0].astype(jnp.bfloat16)
        wu_bf[...] = wu_ref[0].astype(jnp.bfloat16)

    @pl.when(tile_valid_ref[i] != 0)
    def _():
        x = x_ref[...]
        hg = jnp.dot(x, wg_bf[...], preferred_element_type=jnp.float32) + bg_ref[0]
        hu = jnp.dot(x, wu_bf[...], preferred_element_type=jnp.float32) + bu_ref[0]
        glu = jnp.minimum(hg, SWIGLU_LIMIT)
        up = jnp.clip(hu, -SWIGLU_LIMIT, SWIGLU_LIMIT)
        act = (up + 1.0) * glu * jax.nn.sigmoid(SWIGLU_ALPHA * glu)
        h_ref[...] = act.astype(h_ref.dtype)

    @pl.when(tile_valid_ref[i] == 0)
    def _():
        h_ref[...] = jnp.zeros_like(h_ref)


def _moe_down_kernel(tile_expert_ref, tile_valid_ref, h_ref, wd_ref, bd_ref, gate_ref,
                     y_ref, wd_bf):
    i = pl.program_id(1)
    e = tile_expert_ref[i]
    prev = tile_expert_ref[jnp.maximum(i - 1, 0)]

    @pl.when((i == 0) | (e != prev))
    def _():
        wd_bf[...] = wd_ref[0].astype(jnp.bfloat16)

    @pl.when(tile_valid_ref[i] != 0)
    def _():
        y = jnp.dot(h_ref[...], wd_bf[...], preferred_element_type=jnp.float32) + bd_ref[0]
        y_ref[...] = y * gate_ref[...]

    @pl.when(tile_valid_ref[i] == 0)
    def _():
        y_ref[...] = jnp.zeros_like(y_ref)


def _moe_grouped(x_sorted, slot_gate, tile_expert, tile_valid, w_gate_up, b_gate_up, w_down, b_down):
    p_pad, d = x_sorted.shape
    n_exp, _, two_f = w_gate_up.shape
    f = two_f // 2
    tm, fc, dc = MOE_TILE_M, MOE_F_CHUNK, MOE_D_CHUNK
    nt = p_pad // tm
    nfc = f // fc
    ndc = d // dc
    bgu = b_gate_up.reshape(n_exp, 1, two_f)
    bdn = b_down.reshape(n_exp, 1, d)

    h = pl.pallas_call(
        _moe_up_kernel,
        out_shape=jax.ShapeDtypeStruct((p_pad, f), jnp.bfloat16),
        grid_spec=pltpu.PrefetchScalarGridSpec(
            num_scalar_prefetch=2,
            grid=(nfc, nt),
            in_specs=[
                pl.BlockSpec((tm, d), lambda c, i, te, tv: (i, 0)),
                pl.BlockSpec((1, d, fc), lambda c, i, te, tv: (te[i], 0, c)),
                pl.BlockSpec((1, d, fc), lambda c, i, te, tv: (te[i], 0, nfc + c)),
                pl.BlockSpec((1, 1, fc), lambda c, i, te, tv: (te[i], 0, c)),
                pl.BlockSpec((1, 1, fc), lambda c, i, te, tv: (te[i], 0, nfc + c)),
            ],
            out_specs=pl.BlockSpec((tm, fc), lambda c, i, te, tv: (i, c)),
            scratch_shapes=[pltpu.VMEM((d, fc), jnp.bfloat16), pltpu.VMEM((d, fc), jnp.bfloat16)],
        ),
        compiler_params=pltpu.CompilerParams(
            dimension_semantics=("arbitrary", "arbitrary"),
            vmem_limit_bytes=VMEM_LIMIT_BYTES),
        name="moe_up",
    )(tile_expert, tile_valid, x_sorted, w_gate_up, w_gate_up, bgu, bgu)

    y = pl.pallas_call(
        _moe_down_kernel,
        out_shape=jax.ShapeDtypeStruct((p_pad, d), jnp.float32),
        grid_spec=pltpu.PrefetchScalarGridSpec(
            num_scalar_prefetch=2,
            grid=(ndc, nt),
            in_specs=[
                pl.BlockSpec((tm, f), lambda c, i, te, tv: (i, 0)),
                pl.BlockSpec((1, f, dc), lambda c, i, te, tv: (te[i], 0, c)),
                pl.BlockSpec((1, 1, dc), lambda c, i, te, tv: (te[i], 0, c)),
                pl.BlockSpec((tm, 1), lambda c, i, te, tv: (i, 0)),
            ],
            out_specs=pl.BlockSpec((tm, dc), lambda c, i, te, tv: (i, c)),
            scratch_shapes=[pltpu.VMEM((f, dc), jnp.bfloat16)],
        ),
        compiler_params=pltpu.CompilerParams(
            dimension_semantics=("arbitrary", "arbitrary"),
            vmem_limit_bytes=VMEM_LIMIT_BYTES),
        name="moe_down",
    )(tile_expert, tile_valid, h, w_down, bdn, slot_gate)
    return y


def _start_row_gather(idx_ref, first, n_rows, src_hbm, dst, sem):
    def issue(j, carry):
        row = idx_ref[first + j]
        pltpu.make_async_copy(src_hbm.at[pl.ds(row, 1)], dst.at[pl.ds(j, 1)], sem).start()
        return carry
    lax.fori_loop(0, n_rows, issue, 0, unroll=DMA_ISSUE_UNROLL)


def _wait_row_gather(src_hbm, dst, sem):
    pltpu.make_async_copy(src_hbm.at[pl.ds(0, dst.shape[0])], dst, sem).wait()


def _moe_gather_kernel(tok_ref, valid_ref, x_hbm, out_ref, buf, sems):
    i = pl.program_id(0)
    nt = pl.num_programs(0)
    tm = out_ref.shape[0]

    @pl.when((i == 0) & (valid_ref[0] != 0))
    def _():
        _start_row_gather(tok_ref, 0, tm, x_hbm, buf.at[0], sems.at[0])

    nxt = jnp.minimum(i + 1, nt - 1)

    @pl.when((i + 1 < nt) & (valid_ref[nxt] != 0))
    def _():
        _start_row_gather(tok_ref, nxt * tm, tm, x_hbm, buf.at[nxt % 2], sems.at[nxt % 2])

    @pl.when(valid_ref[i] != 0)
    def _():
        slot = i % 2
        _wait_row_gather(x_hbm, buf.at[slot], sems.at[slot])
        out_ref[...] = buf[slot].astype(out_ref.dtype)

    @pl.when(valid_ref[i] == 0)
    def _():
        out_ref[...] = jnp.zeros_like(out_ref)


def _moe_gather_rows(x, slot_token, tile_valid):
    n, d = x.shape
    p_pad = slot_token.shape[0]
    tm = MOE_TILE_M
    return pl.pallas_call(
        _moe_gather_kernel,
        out_shape=jax.ShapeDtypeStruct((p_pad, d), jnp.bfloat16),
        grid_spec=pltpu.PrefetchScalarGridSpec(
            num_scalar_prefetch=2,
            grid=(p_pad // tm,),
            in_specs=[pl.BlockSpec(memory_space=pl.ANY)],
            out_specs=pl.BlockSpec((tm, d), lambda i, tok, tv: (i, 0)),
            scratch_shapes=[pltpu.VMEM((2, tm, d), jnp.float32), pltpu.SemaphoreType.DMA((2,))],
        ),
        compiler_params=pltpu.CompilerParams(
            dimension_semantics=("arbitrary",), vmem_limit_bytes=VMEM_LIMIT_BYTES),
        name="moe_gather",
    )(slot_token, tile_valid, x)


def _moe_combine_kernel(slot_ref, h_ref, g_ref, b_ref, y_hbm, out_ref, buf, sems):
    i = pl.program_id(0)
    nt = pl.num_programs(0)
    tt = h_ref.shape[0]
    rows = TOP_K * tt

    @pl.when(i == 0)
    def _():
        _start_row_gather(slot_ref, 0, rows, y_hbm, buf.at[0], sems.at[0])

    nxt = jnp.minimum(i + 1, nt - 1)

    @pl.when(i + 1 < nt)
    def _():
        _start_row_gather(slot_ref, nxt * rows, rows, y_hbm, buf.at[nxt % 2], sems.at[nxt % 2])

    slot = i % 2
    _wait_row_gather(y_hbm, buf.at[slot], sems.at[slot])
    ffn = buf[slot, 0:tt]
    for k in range(1, TOP_K):
        ffn = ffn + buf[slot, k * tt:(k + 1) * tt]
    x = DEEPNORM_ALPHA * h_ref[...] + ffn
    mu = jnp.mean(x, -1, keepdims=True)
    xc = x - mu
    var = jnp.mean(xc * xc, -1, keepdims=True)
    out_ref[...] = xc * lax.rsqrt(var + LN_EPS) * g_ref[...] + b_ref[...]


def _moe_combine_ln(h, y, pair_slot, ln_g, ln_b):
    n, d = h.shape
    tt = MOE_COMBINE_TILE
    nt = n // tt
    tile_slots = jnp.swapaxes(pair_slot.reshape(nt, tt, TOP_K), 1, 2).reshape(-1)
    return pl.pallas_call(
        _moe_combine_kernel,
        out_shape=jax.ShapeDtypeStruct((n, d), jnp.float32),
        grid_spec=pltpu.PrefetchScalarGridSpec(
            num_scalar_prefetch=1,
            grid=(nt,),
            in_specs=[pl.BlockSpec((tt, d), lambda i, ps: (i, 0)),
                      pl.BlockSpec((1, d), lambda i, ps: (0, 0)),
                      pl.BlockSpec((1, d), lambda i, ps: (0, 0)),
                      pl.BlockSpec(memory_space=pl.ANY)],
            out_specs=pl.BlockSpec((tt, d), lambda i, ps: (i, 0)),
            scratch_shapes=[pltpu.VMEM((2, TOP_K * tt, d), jnp.float32), pltpu.SemaphoreType.DMA((2,))],
        ),
        compiler_params=pltpu.CompilerParams(
            dimension_semantics=("arbitrary",), vmem_limit_bytes=VMEM_LIMIT_BYTES),
        name="moe_combine",
    )(tile_slots, h, ln_g.reshape(1, d), ln_b.reshape(1, d), y)


def moe_ffn_ln(x, top_w, top_idx, w_gate_up, b_gate_up, w_down, b_down, ln_g, ln_b):
    n, d = x.shape
    tm = MOE_TILE_M
    n_pairs = n * TOP_K
    nt = n_pairs // tm + N_EXPERTS
    p_pad = nt * tm
    pair_expert = top_idx.reshape(n_pairs)
    pair_token = jnp.repeat(jnp.arange(n, dtype=jnp.int32), TOP_K)
    pair_w = top_w.reshape(n_pairs)
    order = jnp.argsort(pair_expert, stable=True)
    sorted_expert = pair_expert[order]
    counts = jnp.sum(jax.nn.one_hot(pair_expert, N_EXPERTS, dtype=jnp.int32), axis=0)
    tiles_per = (counts + tm - 1) // tm
    tile_start = jnp.cumsum(tiles_per) - tiles_per
    group_start = jnp.cumsum(counts) - counts
    rank = jnp.arange(n_pairs, dtype=jnp.int32) - group_start[sorted_expert]
    dest_sorted = tile_start[sorted_expert] * tm + rank
    slot_token = jnp.zeros((p_pad,), jnp.int32).at[dest_sorted].set(pair_token[order])
    slot_gate = jnp.zeros((p_pad,), jnp.float32).at[dest_sorted].set(pair_w[order])
    pair_slot = jnp.zeros((n_pairs,), jnp.int32).at[order].set(dest_sorted).reshape(n, TOP_K)
    total_tiles = jnp.sum(tiles_per)
    tile_ids = jnp.arange(nt, dtype=jnp.int32)
    tile_end = jnp.cumsum(tiles_per)
    tile_expert = jnp.minimum(jnp.sum((tile_ids[:, None] >= tile_end[None, :]).astype(jnp.int32), axis=1),
                              N_EXPERTS - 1).astype(jnp.int32)
    tile_valid = (tile_ids < total_tiles).astype(jnp.int32)

    x_sorted = _moe_gather_rows(x, slot_token, tile_valid)
    y = _moe_grouped(x_sorted, slot_gate.reshape(p_pad, 1), tile_expert, tile_valid,
                     w_gate_up, b_gate_up, w_down, b_down)
    return _moe_combine_ln(x, y, pair_slot, ln_g, ln_b)


def kernel(x_prompt, x_sample, cache_win0_kv, cache_win1_kv, cache_win2_kv, cache_mem_kv,
           state_wkv, state_shift, mem_prompt,
           w_in, w_gate, b_gate, w_branch, w_out, w_mem_kv,
           mu_rkv, mu_wag, w0, w1, w2, a0, a1, a2, g1, g2, k_k, k_a, r_k, gn_g, gn_b,
           ln1_g, ln1_b, w_router, b_router, w_gate_up, b_gate_up, w_down, b_down, ln2_g, ln2_b):
    bf16 = jnp.bfloat16
    xp, xs = x_prompt, x_sample
    bp, tp, d = xp.shape
    bs, ts, _ = xs.shape
    n_prompt = bp * tp
    l = 0
    lp = {"w_in": w_in[l], "mu_rkv": mu_rkv[l], "mu_wag": mu_wag[l],
          "w0": w0[l], "w1": w1[l], "w2": w2[l], "a0": a0[l], "a1": a1[l], "a2": a2[l],
          "g1": g1[l], "g2": g2[l], "k_k": k_k[l], "k_a": k_a[l], "r_k": r_k[l],
          "gn_g": gn_g[l], "gn_b": gn_b[l]}

    u_all = jnp.concatenate([xp.reshape(-1, d), xs.reshape(-1, d)], axis=0)
    u_bf = u_all.astype(bf16)
    proj = matmul_bias(u_bf, w_in[l].astype(bf16), tile_m=PROJ_TILE_M, name="proj_in")
    gate = matmul_bias(u_bf, w_gate[l].astype(bf16), b_gate[l], tile_m=PROJ_TILE_M, sigmoid=True, name="proj_gate")
    mem_rows = mem_prompt.reshape(-1, d)
    mem_kv_p = matmul_bias(mem_rows.astype(bf16), w_mem_kv[l].astype(bf16), tile_m=mem_rows.shape[0], name="proj_mem")
    proj_p = proj[:n_prompt].reshape(bp, tp, PROJ_WIDTH)
    proj_s = proj[n_prompt:].reshape(bs, ts, PROJ_WIDTH)
    rkv0, qm0 = 3 * ATT_WIDTH, 3 * ATT_WIDTH + 3 * RWKV_WIDTH

    heads = lambda t: t.reshape(t.shape[0], t.shape[1], N_ATT_HEADS, HEAD_DIM)
    qa_p, ka_p, va_p = (heads(proj_p[..., i * ATT_WIDTH:(i + 1) * ATT_WIDTH]) for i in range(3))
    ka_s, va_s = (heads(proj_s[..., i * ATT_WIDTH:(i + 1) * ATT_WIDTH]) for i in (1, 2))
    outs, lses, win_p, win_s = [], [], [], []
    for g, (win, dil) in enumerate(DIL_GROUPS):
        hs = slice(g * HEADS_PER_GROUP, (g + 1) * HEADS_PER_GROUP)
        o, lse = dilated_attention_prompt(qa_p[:, :, hs], ka_p[:, :, hs], va_p[:, :, hs], g, dil)
        outs.append(o.reshape(n_prompt, ATT_OUT))
        lses.append(lse.reshape(n_prompt, ATT_OUT))
        keep = min(win, tp)
        win_p.append(jnp.stack([ka_p[:, tp - keep:, hs], va_p[:, tp - keep:, hs]], axis=2))
        win_s.append(jnp.stack([ka_s[:, :, hs], va_s[:, :, hs]], axis=2))
    o_att_p = att_combine(outs, lses)
    o_att_s = dilated_attention_step(proj_s, cache_win0_kv[l], cache_win1_kv[l], cache_win2_kv[l])

    o_mem_p = memory_attention(proj_p[..., qm0:], mem_kv_p.reshape(bp, MEM_TOKENS, 2 * MEM_WIDTH))
    o_mem_s = memory_attention(proj_s[..., qm0:], cache_mem_kv[l].reshape(bs, MEM_TOKENS, 2 * MEM_WIDTH))

    u0 = jnp.zeros((bp, d), xp.dtype)
    s0 = jnp.zeros((bp, RWKV_HEADS, RWKV_HEAD_DIM, RWKV_HEAD_DIM), jnp.float32)
    o_rwkv_p, wkv_p = rwkv_branch(xp, u0, s0, proj_p[..., rkv0:qm0], lp)
    o_rwkv_s, wkv_s = rwkv_branch(xs, state_shift[l], state_wkv[l], proj_s[..., rkv0:qm0], lp)

    cat = lambda a, b, w: jnp.concatenate([a.reshape(-1, w), b.reshape(-1, w)], axis=0)
    h, top_w, top_idx = merge_ln_router(
        cat(o_att_p, o_att_s, ATT_OUT), cat(o_rwkv_p, o_rwkv_s, RWKV_WIDTH), cat(o_mem_p, o_mem_s, MEM_WIDTH),
        gate, u_all, w_branch[l].astype(bf16), w_out[l].astype(bf16), ln1_g[l], ln1_b[l], w_router[l], b_router[l])
    y_all = moe_ffn_ln(h, top_w, top_idx, w_gate_up[l], b_gate_up[l], w_down[l], b_down[l], ln2_g[l], ln2_b[l])
    yp = y_all[:n_prompt].reshape(bp, tp, d)
    ys = y_all[n_prompt:].reshape(bs, ts, d)
    mem_kv_out = mem_kv_p.reshape(bp, MEM_TOKENS, 2, MEM_HEADS, MEM_HEAD_DIM)
    return (yp, ys,
            win_p[0][None], win_p[1][None], win_p[2][None], wkv_p.astype(xp.dtype)[None],
            xp[:, -1][None], mem_kv_out[None],
            win_s[0].astype(cache_win0_kv.dtype)[None], win_s[1].astype(cache_win0_kv.dtype)[None],
            win_s[2].astype(cache_win0_kv.dtype)[None], wkv_s.astype(state_wkv.dtype)[None],
            xs[:, -1].astype(state_shift.dtype)[None])
```

```python
import functools

import jax
import jax.numpy as jnp
from jax import lax
from jax.experimental import pallas as pl
from jax.experimental.pallas import tpu as pltpu

D_MODEL = 2048
DEPTH = 1
HEAD_DIM = 64
DIL_GROUPS = ((128, 1), (512, 4), (2048, 16))
HEADS_PER_GROUP = 4
N_ATT_HEADS = HEADS_PER_GROUP * len(DIL_GROUPS)
ATT_WIDTH = N_ATT_HEADS * HEAD_DIM
ATT_OUT = HEADS_PER_GROUP * HEAD_DIM
BAND_BLOCK = 128
N_BACK = 128
GROUP_WIDTH = HEADS_PER_GROUP * HEAD_DIM
RWKV_HEADS = 12
RWKV_HEAD_DIM = 64
RWKV_WIDTH = RWKV_HEADS * RWKV_HEAD_DIM
GN_EPS = 64e-5
MEM_TOKENS = 256
MEM_HEADS = 4
MEM_HEAD_DIM = 128
MEM_WIDTH = MEM_HEADS * MEM_HEAD_DIM
PROJ_WIDTH = 3 * ATT_WIDTH + 3 * RWKV_WIDTH + MEM_WIDTH
N_BRANCHES = 3
N_EXPERTS = 32
TOP_K = 4
D_EXPERT = 2048
SWIGLU_LIMIT = 7.0
SWIGLU_ALPHA = 1.702
LN_EPS = 1e-5
DEEPNORM_ALPHA = (2.0 * DEPTH) ** 0.25
NEG_INF = -1e30

VMEM_LIMIT_BYTES = 56 * 1024 * 1024
SUBLANES = 8
LANES = 128
MOE_TILE_M = 256
MOE_F_CHUNK = 1024
MOE_D_CHUNK = 1024
MOE_ROUTE_TILE = 64
RWKV_CHUNK = 64
RWKV_SEQ_PER_STEP = 8
PROJ_TILE_M = 1088
MATMUL_TILE_N = 512
MERGE_TILE_M = 256
ATT_COMBINE_TILE = 512
MEM_ATTN_TILE_Q = 512
ATTN_STEP_SEQS = 2


def _matmul_kernel(x_ref, w_ref, b_ref, o_ref, *, sigmoid):
    y = jnp.dot(x_ref[...], w_ref[...], preferred_element_type=jnp.float32) + b_ref[...]
    o_ref[...] = jax.nn.sigmoid(y) if sigmoid else y


def matmul_bias(x, w, bias=None, *, tile_m, sigmoid=False, name="matmul"):
    m, k = x.shape
    n = w.shape[1]
    tn = MATMUL_TILE_N if n % MATMUL_TILE_N == 0 else n
    if bias is None:
        bias = jnp.zeros((n,), jnp.float32)
    return pl.pallas_call(
        functools.partial(_matmul_kernel, sigmoid=sigmoid),
        out_shape=jax.ShapeDtypeStruct((m, n), jnp.float32),
        grid=(n // tn, m // tile_m),
        in_specs=[pl.BlockSpec((tile_m, k), lambda j, i: (i, 0)),
                  pl.BlockSpec((k, tn), lambda j, i: (0, j)),
                  pl.BlockSpec((1, tn), lambda j, i: (0, j))],
        out_specs=pl.BlockSpec((tile_m, tn), lambda j, i: (i, j)),
        compiler_params=pltpu.CompilerParams(
            dimension_semantics=("arbitrary", "arbitrary"), vmem_limit_bytes=VMEM_LIMIT_BYTES),
        name=name,
    )(x, w, bias.reshape(1, n))


def _merge_kernel(att_ref, rwkv_ref, mem_ref, gate_ref, x_ref, wb_ref, wo_ref, g_ref, b_ref,
                  wr_hi_ref, wr_lo_ref, br_ref, h_ref, topw_ref, topi_ref):
    bf16, f32 = jnp.bfloat16, jnp.float32
    d = x_ref.shape[1]
    dot = functools.partial(jnp.dot, preferred_element_type=f32)
    y = dot(att_ref[...].astype(bf16), wb_ref[0:ATT_OUT]) * gate_ref[:, 0:d]
    y = y + dot(rwkv_ref[...].astype(bf16), wb_ref[ATT_OUT:ATT_OUT + RWKV_WIDTH]) * gate_ref[:, d:2 * d]
    y = y + dot(mem_ref[...].astype(bf16), wb_ref[ATT_OUT + RWKV_WIDTH:]) * gate_ref[:, 2 * d:]
    z = DEEPNORM_ALPHA * x_ref[...] + dot(y.astype(bf16), wo_ref[...])
    mu = jnp.mean(z, -1, keepdims=True)
    zc = z - mu
    var = jnp.mean(zc * zc, -1, keepdims=True)
    h = zc * lax.rsqrt(var + LN_EPS) * g_ref[...] + b_ref[...]
    h_ref[...] = h

    h_hi = h.astype(bf16)
    h_lo = (h - h_hi.astype(f32)).astype(bf16)
    logits = (dot(h_hi, wr_hi_ref[...]) + dot(h_hi, wr_lo_ref[...]) + dot(h_lo, wr_hi_ref[...])) + br_ref[...]
    lane = lax.broadcasted_iota(jnp.int32, logits.shape, 1)
    work = jnp.where(lane < N_EXPERTS, logits, -jnp.inf)
    tops, ids = [], []
    for _ in range(TOP_K):
        m = jnp.max(work, -1, keepdims=True)
        idx = jnp.min(jnp.where(work == m, lane, LANES), -1, keepdims=True)
        tops.append(m)
        ids.append(idx)
        work = jnp.where(lane == idx, -jnp.inf, work)
    e = [jnp.exp(t - tops[0]) for t in tops]
    den = e[0] + e[1] + e[2] + e[3]
    tw = jnp.zeros(logits.shape, f32)
    ti = jnp.zeros(logits.shape, jnp.int32)
    for k in range(TOP_K):
        tw = jnp.where(lane == k, e[k] / den, tw)
        ti = jnp.where(lane == k, ids[k], ti)
    topw_ref[...] = tw
    topi_ref[...] = ti


def merge_ln_router(o_att, o_rwkv, o_mem, gate, x, w_branch, w_out, ln_g, ln_b, w_router, b_router):
    n, d = x.shape
    tm = MERGE_TILE_M
    wr = jnp.zeros((d, LANES), jnp.float32).at[:, :N_EXPERTS].set(w_router)
    wr_hi = wr.astype(jnp.bfloat16)
    wr_lo = (wr - wr_hi.astype(jnp.float32)).astype(jnp.bfloat16)
    br = jnp.zeros((1, LANES), jnp.float32).at[0, :N_EXPERTS].set(b_router)
    rows = lambda w: pl.BlockSpec((tm, w), lambda i: (i, 0))
    full = lambda a: pl.BlockSpec(a.shape, lambda i: (0, 0))
    h, topw, topi = pl.pallas_call(
        _merge_kernel,
        out_shape=(jax.ShapeDtypeStruct((n, d), jnp.float32),
                   jax.ShapeDtypeStruct((n, LANES), jnp.float32),
                   jax.ShapeDtypeStruct((n, LANES), jnp.int32)),
        grid=(n // tm,),
        in_specs=[rows(ATT_OUT), rows(RWKV_WIDTH), rows(MEM_WIDTH), rows(3 * d), rows(d),
                  full(w_branch), full(w_out), pl.BlockSpec((1, d), lambda i: (0, 0)),
                  pl.BlockSpec((1, d), lambda i: (0, 0)), full(wr_hi), full(wr_lo), full(br)],
        out_specs=(rows(d), rows(LANES), rows(LANES)),
        compiler_params=pltpu.CompilerParams(
            dimension_semantics=("arbitrary",), vmem_limit_bytes=VMEM_LIMIT_BYTES),
        name="merge_ln_router",
    )(o_att, o_rwkv, o_mem, gate, x, w_branch, w_out, ln_g.reshape(1, d), ln_b.reshape(1, d), wr_hi, wr_lo, br)
    return h, topw[:, :TOP_K], topi[:, :TOP_K]


def _alibi_slope(head):
    return 2.0 ** (-8.0 * (head + 1) / N_ATT_HEADS)


def _dot_nt(a, b):
    return lax.dot_general(a, b, (((1,), (1,)), ((), ())), preferred_element_type=jnp.float32)


def _dil_attn_prompt_kernel(q_ref, kc_ref, kp_ref, vc_ref, vp_ref, o_ref, lse_ref, *, group, dil):
    bf16 = jnp.bfloat16
    blk = BAND_BLOCK
    i = pl.program_id(1)
    qi = lax.broadcasted_iota(jnp.int32, (blk, blk), 0)
    kj = lax.broadcasted_iota(jnp.int32, (blk, blk), 1)
    dist_c = qi - kj
    dist_p = dist_c + blk
    valid_c = dist_c >= 0
    valid_p = (dist_p <= blk) & (i > 0)
    scale = HEAD_DIM ** -0.5
    for h in range(HEADS_PER_GROUP):
        slope = _alibi_slope(group * HEADS_PER_GROUP + h) * dil
        q = q_ref[0, h].astype(bf16)
        s_c = _dot_nt(q, kc_ref[0, h].astype(bf16)) * scale - slope * dist_c.astype(jnp.float32)
        s_p = _dot_nt(q, kp_ref[0, h].astype(bf16)) * scale - slope * dist_p.astype(jnp.float32)
        s_c = jnp.where(valid_c, s_c, NEG_INF)
        s_p = jnp.where(valid_p, s_p, NEG_INF)
        m = jnp.maximum(jnp.max(s_c, -1, keepdims=True), jnp.max(s_p, -1, keepdims=True))
        p_c = jnp.exp(s_c - m)
        p_p = jnp.exp(s_p - m)
        den = jnp.sum(p_c, -1, keepdims=True) + jnp.sum(p_p, -1, keepdims=True)
        pv = (jnp.dot(p_c.astype(bf16), vc_ref[0, h].astype(bf16), preferred_element_type=jnp.float32)
              + jnp.dot(p_p.astype(bf16), vp_ref[0, h].astype(bf16), preferred_element_type=jnp.float32))
        o_ref[0, h] = pv / den
        lse_ref[0, h] = jnp.broadcast_to(m + jnp.log(den), (blk, HEAD_DIM))


def dilated_attention_prompt(q, k, v, group, dil):
    b, t, h, e = q.shape
    l = t // dil
    assert l % BAND_BLOCK == 0

    def by_residue(a):
        return a.reshape(b, l, dil, h, e).transpose(0, 2, 3, 1, 4).reshape(b * dil, h, l, e)

    def from_residue(a):
        return a.reshape(b, dil, h, l, e).transpose(0, 3, 1, 2, 4).reshape(b, t, h, e)

    cur = pl.BlockSpec((1, h, BAND_BLOCK, e), lambda n, i: (n, 0, i, 0))
    prev = pl.BlockSpec((1, h, BAND_BLOCK, e), lambda n, i: (n, 0, jnp.maximum(i - 1, 0), 0))
    qr, kr, vr = by_residue(q), by_residue(k), by_residue(v)
    o, lse = pl.pallas_call(
        functools.partial(_dil_attn_prompt_kernel, group=group, dil=dil),
        out_shape=(jax.ShapeDtypeStruct(qr.shape, jnp.float32), jax.ShapeDtypeStruct(qr.shape, jnp.float32)),
        grid=(b * dil, l // BAND_BLOCK),
        in_specs=[cur, cur, prev, cur, prev],
        out_specs=(cur, cur),
        compiler_params=pltpu.CompilerParams(
            dimension_semantics=("arbitrary", "arbitrary"), vmem_limit_bytes=VMEM_LIMIT_BYTES),
        name="dil_attn_prompt_g%d" % group,
    )(qr, kr, kr, vr, vr)
    return from_residue(o), from_residue(lse)


def _att_combine_kernel(o0, o1, o2, l0, l1, l2, out_ref):
    a, b, c = l0[...], l1[...], l2[...]
    m = jnp.maximum(jnp.maximum(a, b), c)
    ea, eb, ec = jnp.exp(a - m), jnp.exp(b - m), jnp.exp(c - m)
    out_ref[...] = (ea * o0[...] + eb * o1[...] + ec * o2[...]) / (ea + eb + ec)


def att_combine(outs, lses):
    n, w = outs[0].shape
    spec = pl.BlockSpec((ATT_COMBINE_TILE, w), lambda i: (i, 0))
    return pl.pallas_call(
        _att_combine_kernel,
        out_shape=jax.ShapeDtypeStruct((n, w), jnp.float32),
        grid=(n // ATT_COMBINE_TILE,),
        in_specs=[spec] * 6,
        out_specs=spec,
        compiler_params=pltpu.CompilerParams(dimension_semantics=("arbitrary",)),
        name="att_combine",
    )(*outs, *lses)


def _mem_attn_kernel(q_ref, kv_ref, o_ref):
    bf16 = jnp.bfloat16
    e = MEM_HEAD_DIM
    scale = e ** -0.5
    for h in range(MEM_HEADS):
        q = q_ref[0, :, h * e:(h + 1) * e].astype(bf16)
        k = kv_ref[0, :, h * e:(h + 1) * e].astype(bf16)
        v = kv_ref[0, :, (MEM_HEADS + h) * e:(MEM_HEADS + h + 1) * e].astype(bf16)
        s = _dot_nt(q, k) * scale
        m = jnp.max(s, -1, keepdims=True)
        p = jnp.exp(s - m)
        p = p / jnp.sum(p, -1, keepdims=True)
        o_ref[0, :, h * e:(h + 1) * e] = jnp.dot(p.astype(bf16), v, preferred_element_type=jnp.float32)


def memory_attention(q, mem_kv):
    b, t, w = q.shape
    tq = min(t, MEM_ATTN_TILE_Q)
    return pl.pallas_call(
        _mem_attn_kernel,
        out_shape=jax.ShapeDtypeStruct((b, t, w), jnp.float32),
        grid=(b, t // tq),
        in_specs=[pl.BlockSpec((1, tq, w), lambda n, i: (n, i, 0)),
                  pl.BlockSpec((1, MEM_TOKENS, 2 * w), lambda n, i: (n, 0, 0))],
        out_specs=pl.BlockSpec((1, tq, w), lambda n, i: (n, i, 0)),
        compiler_params=pltpu.CompilerParams(
            dimension_semantics=("arbitrary", "arbitrary"), vmem_limit_bytes=VMEM_LIMIT_BYTES),
        name="mem_attn",
    )(q, mem_kv)


def _dil_attn_step_kernel(qkv_ref, c0_ref, c1_ref, c2_ref, o_ref, *, n_new):
    bf16, f32 = jnp.bfloat16, jnp.float32
    scale = HEAD_DIM ** -0.5
    caches = (c0_ref, c1_ref, c2_ref)
    tq = lax.broadcasted_iota(jnp.int32, (n_new, n_new), 0)
    tk = lax.broadcasted_iota(jnp.int32, (n_new, n_new), 1)
    for b, h in [(b, h) for b in range(qkv_ref.shape[0]) for h in range(HEADS_PER_GROUP)]:
        outs, lses = [], []
        for g, (win, dil) in enumerate(DIL_GROUPS):
            head = g * HEADS_PER_GROUP + h
            slope = _alibi_slope(head)
            lanes = slice(head * HEAD_DIM, (head + 1) * HEAD_DIM)
            q = qkv_ref[b, :, lanes].astype(bf16)
            k_new = qkv_ref[b, :, ATT_WIDTH + head * HEAD_DIM:ATT_WIDTH + (head + 1) * HEAD_DIM].astype(bf16)
            v_new = qkv_ref[b, :, 2 * ATT_WIDTH + head * HEAD_DIM:2 * ATT_WIDTH + (head + 1) * HEAD_DIM].astype(bf16)
            s = jnp.dot(q, caches[g][b, 0, h].astype(bf16), preferred_element_type=f32) * scale
            dist = (win + lax.broadcasted_iota(jnp.int32, (n_new, win), 0)
                    - lax.broadcasted_iota(jnp.int32, (n_new, win), 1))
            valid = ((dist & (dil - 1)) == 0) & (dist <= N_BACK * dil)
            s = jnp.where(valid, s - slope * dist.astype(f32), NEG_INF)
            sn = _dot_nt(q, k_new) * scale
            dn = tq - tk
            sn = jnp.where((dn >= 0) & ((dn & (dil - 1)) == 0), sn - slope * dn.astype(f32), NEG_INF)
            m = jnp.maximum(jnp.max(s, axis=1, keepdims=True), jnp.max(sn, axis=1, keepdims=True))
            p = jnp.exp(s - m)
            pn = jnp.exp(sn - m)
            den = jnp.sum(p, axis=1, keepdims=True) + jnp.sum(pn, axis=1, keepdims=True)
            pv = (_dot_nt(p.astype(bf16), caches[g][b, 1, h].astype(bf16))
                  + jnp.dot(pn.astype(bf16), v_new, preferred_element_type=f32))
            outs.append(pv / den)
            lses.append(m + jnp.log(den))
        mm = jnp.maximum(jnp.maximum(lses[0], lses[1]), lses[2])
        w = [jnp.exp(l - mm) for l in lses]
        o_ref[b, :, h * HEAD_DIM:(h + 1) * HEAD_DIM] = (
            (w[0] * outs[0] + w[1] * outs[1] + w[2] * outs[2]) / (w[0] + w[1] + w[2]))


def dilated_attention_step(p_s, cache0, cache1, cache2):
    b, t, _ = p_s.shape
    feat_major = lambda c: jnp.transpose(c, (0, 2, 3, 4, 1))
    ns = ATTN_STEP_SEQS
    win_spec = lambda w: pl.BlockSpec((ns, 2, HEADS_PER_GROUP, HEAD_DIM, w), lambda n: (n, 0, 0, 0, 0))
    return pl.pallas_call(
        functools.partial(_dil_attn_step_kernel, n_new=t),
        out_shape=jax.ShapeDtypeStruct((b, t, GROUP_WIDTH), jnp.float32),
        grid=(b // ns,),
        in_specs=[pl.BlockSpec((ns, t, 3 * ATT_WIDTH), lambda n: (n, 0, 0))] + [win_spec(w) for w, _ in DIL_GROUPS],
        out_specs=pl.BlockSpec((ns, t, GROUP_WIDTH), lambda n: (n, 0, 0)),
        compiler_params=pltpu.CompilerParams(
            dimension_semantics=("arbitrary",), vmem_limit_bytes=VMEM_LIMIT_BYTES),
        name="dil_attn_step",
    )(p_s, feat_major(cache0), feat_major(cache1), feat_major(cache2))


def _bmm(a, b):
    return lax.dot_general(a, b, (((2,), (1,)), ((0,), (0,))), preferred_element_type=jnp.float32)


def _bmm_nt(a, b):
    return lax.dot_general(a, b, (((2,), (2,)), ((0,), (0,))), preferred_element_type=jnp.float32)


def _bmm_tn(a, b):
    return lax.dot_general(a, b, (((1,), (1,)), ((0,), (0,))), preferred_element_type=jnp.float32)


def _rwkv_chunk_kernel(r_ref, lw_ref, k_ref, v_ref, a_ref, b_ref, s0_ref, o_ref, sT_ref, s_scr,
                       *, n_seq, chunk):
    bf16, f32 = jnp.bfloat16, jnp.float32
    c = pl.program_id(1)
    C = chunk
    H = r_ref.shape[1]

    @pl.when(c == 0)
    def _():
        s_scr[...] = s0_ref[...]

    row = lax.broadcasted_iota(jnp.int32, (C, C), 0)
    col = lax.broadcasted_iota(jnp.int32, (C, C), 1)
    strict = (row > col)[None]
    incl = (row >= col)[None]
    tri = jnp.broadcast_to(jnp.where(incl, 1.0, 0.0).astype(bf16), (H, C, C))

    def one_seq(i, carry):
        r, lw, k, v, a, b = (ref[i] for ref in (r_ref, lw_ref, k_ref, v_ref, a_ref, b_ref))
        s0 = s_scr[i]
        hi = lw.astype(bf16)
        r1 = lw - hi.astype(f32)
        mid = r1.astype(bf16)
        lo = (r1 - mid.astype(f32)).astype(bf16)
        lc = _bmm(tri, hi) + _bmm(tri, mid) + _bmm(tri, lo)
        e_pos = jnp.exp(lc)
        e_neg = jnp.exp(-lc)
        a_t = (a * jnp.exp(lc - lw)).astype(bf16)
        r_t = (r * e_pos).astype(bf16)
        b_t = (b * e_neg).astype(bf16)
        k_t = (k * e_neg).astype(bf16)
        v_b = v.astype(bf16)
        s0_b = s0.astype(bf16)

        a_ab = jnp.where(strict, _bmm_nt(a_t, b_t), 0.0)
        a_ak = jnp.where(strict, _bmm_nt(a_t, k_t), 0.0)
        a_rb = jnp.where(incl, _bmm_nt(r_t, b_t), 0.0)
        a_rk = jnp.where(incl, _bmm_nt(r_t, k_t), 0.0)

        x = _bmm_nt(a_t, s0_b) + _bmm(a_ak.astype(bf16), v_b)
        p = a_ab
        n = 1
        while n < C:
            p_b = p.astype(bf16)
            x = x + _bmm(p_b, x.astype(bf16))
            n *= 2
            if n < C:
                p = _bmm(p_b, p_b)
        u_b = x.astype(bf16)
        o = _bmm_nt(r_t, s0_b) + _bmm(a_rb.astype(bf16), u_b) + _bmm(a_rk.astype(bf16), v_b)
        s1 = s0 + _bmm_tn(u_b, b_t) + _bmm_tn(v_b, k_t)
        s1 = s1 * e_pos[:, C - 1:C, :]
        o_ref[i] = o
        s_scr[i] = s1
        return carry

    lax.fori_loop(0, n_seq, one_seq, 0)

    @pl.when(c == pl.num_programs(1) - 1)
    def _():
        sT_ref[...] = s_scr[...]


def _rwkv_recurrence_call(s0, r, lw, k, v, a, b, *, chunk, n_seq):
    nb, h, t, n = r.shape
    nch = t // chunk
    seq_spec = pl.BlockSpec((n_seq, h, chunk, n), lambda i, c: (i, 0, c, 0))
    st_spec = pl.BlockSpec((n_seq, h, n, n), lambda i, c: (i, 0, 0, 0))
    o, s_t = pl.pallas_call(
        functools.partial(_rwkv_chunk_kernel, n_seq=n_seq, chunk=chunk),
        out_shape=(jax.ShapeDtypeStruct((nb, h, t, n), jnp.float32),
                   jax.ShapeDtypeStruct((nb, h, n, n), jnp.float32)),
        grid=(nb // n_seq, nch),
        in_specs=[seq_spec] * 6 + [st_spec],
        out_specs=(seq_spec, st_spec),
        scratch_shapes=[pltpu.VMEM((n_seq, h, n, n), jnp.float32)],
        compiler_params=pltpu.CompilerParams(
            dimension_semantics=("arbitrary", "arbitrary"),
            vmem_limit_bytes=VMEM_LIMIT_BYTES),
        name="rwkv_chunk",
    )(r, lw, k, v, a, b, s0)
    return s_t, o


def rwkv_recurrence(s0, r, log_w, k, v, a, b):
    bsz, t = r.shape[0], r.shape[1]
    if t >= RWKV_CHUNK:
        chunk, n_seq, t_pad = RWKV_CHUNK, 1, -(-t // RWKV_CHUNK) * RWKV_CHUNK
    else:
        chunk = t_pad = -(-t // SUBLANES) * SUBLANES
        n_seq = RWKV_SEQ_PER_STEP
    to_heads = lambda x: jnp.pad(jnp.swapaxes(x, 1, 2), ((0, 0), (0, 0), (0, t_pad - t), (0, 0)))
    s_t, o = _rwkv_recurrence_call(s0.astype(jnp.float32), *(to_heads(x) for x in (r, log_w, k, v, a, b)),
                                   chunk=chunk, n_seq=n_seq)
    return s_t, jnp.swapaxes(o[:, :, :t], 1, 2)


def rwkv_branch(u, u_last, s0, p_rkv, lp):
    B, T, _ = u.shape
    f32 = jnp.float32
    w_rkv = lp["w_in"][:, 3 * ATT_WIDTH:3 * ATT_WIDTH + 3 * RWKV_WIDTH]
    u_last = u_last.astype(u.dtype)
    u_prev = jnp.concatenate([u_last[:, None], u[:, :-1]], axis=1)
    p_prev = jnp.concatenate([(u_last @ w_rkv)[:, None], p_rkv[:, :-1]], axis=1)
    rkv = p_rkv + (p_prev - p_rkv) * lp["mu_rkv"]
    r, k, v = jnp.split(rkv, 3, axis=-1)
    du = u_prev - u
    mu = lp["mu_wag"]
    xw = u + du * mu[0]
    xa = u + du * mu[1]
    xg = u + du * mu[2]
    w_log = -jax.nn.softplus(-(lp["w0"] + jnp.tanh(xw @ lp["w1"]) @ lp["w2"]).astype(f32)) - 0.5
    log_decay = -jnp.exp(w_log)
    a = jax.nn.sigmoid((lp["a0"] + (xa @ lp["a1"]) @ lp["a2"]).astype(f32))
    g = jax.nn.sigmoid(xg @ lp["g1"]) @ lp["g2"]
    heads = lambda t: t.astype(f32).reshape(B, T, RWKV_HEADS, RWKV_HEAD_DIM)
    kk = heads(k * lp["k_k"])
    kk = kk / jnp.maximum(jnp.sqrt(jnp.sum(kk * kk, -1, keepdims=True)), 1e-12)
    k = heads(k.astype(f32) * (1.0 + (a - 1.0) * lp["k_a"]))
    r, v, log_decay, a = heads(r), heads(v), heads(log_decay), heads(a)
    s_new, o = rwkv_recurrence(s0, r, log_decay, k, v, -kk, kk * a)
    mean = jnp.mean(o, -1, keepdims=True)
    var = jnp.mean(jnp.square(o - mean), -1, keepdims=True)
    on = ((o - mean) * lax.rsqrt(var + GN_EPS)).reshape(B, T, RWKV_WIDTH) * lp["gn_g"] + lp["gn_b"]
    bonus = (jnp.sum(r * k * lp["r_k"], -1, keepdims=True) * v).reshape(B, T, RWKV_WIDTH)
    return (on + bonus) * g, s_new


def _moe_scatter_kernel(slot_ref, h_ref, init_hbm, out_hbm, stage, sems):
    del init_hbm
    i = pl.program_id(0)
    nt = pl.num_programs(0)
    tt = h_ref.shape[0]
    slot = i % 2

    def wait_buffer(s):
        for _ in range(TOP_K):
            pltpu.make_async_copy(stage.at[s], out_hbm.at[pl.ds(0, tt)], sems.at[s]).wait()

    @pl.when(i >= 2)
    def _():
        wait_buffer(slot)

    stage[slot] = h_ref[...]
    for j in range(tt):
        for k in range(TOP_K):
            dst = slot_ref[(i * tt + j) * TOP_K + k]
            pltpu.make_async_copy(stage.at[slot, pl.ds(j, 1)], out_hbm.at[pl.ds(dst, 1)], sems.at[slot]).start()

    @pl.when(i == nt - 1)
    def _():
        wait_buffer(slot)

        @pl.when(nt >= 2)
        def _():
            wait_buffer(1 - slot)


def _moe_scatter_rows(h, pair_slot, p_pad):
    n, d = h.shape
    tt = MOE_ROUTE_TILE
    return pl.pallas_call(
        _moe_scatter_kernel,
        out_shape=jax.ShapeDtypeStruct((p_pad, d), jnp.float32),
        grid_spec=pltpu.PrefetchScalarGridSpec(
            num_scalar_prefetch=1,
            grid=(n // tt,),
            in_specs=[pl.BlockSpec((tt, d), lambda i, ps: (i, 0)),
                      pl.BlockSpec(memory_space=pl.ANY)],
            out_specs=pl.BlockSpec(memory_space=pl.ANY),
            scratch_shapes=[pltpu.VMEM((2, tt, d), jnp.float32), pltpu.SemaphoreType.DMA((2,))],
        ),
        input_output_aliases={2: 0},
        compiler_params=pltpu.CompilerParams(
            dimension_semantics=("arbitrary",), vmem_limit_bytes=VMEM_LIMIT_BYTES),
        name="moe_scatter",
    )(pair_slot.reshape(-1), h, jnp.zeros((p_pad, d), jnp.float32))


def _moe_up_kernel(tile_expert_ref, tile_valid_ref, x_ref, wg_ref, wu_ref, bg_ref, bu_ref,
                   h_ref, wg_bf, wu_bf):
    i = pl.program_id(1)
    e = tile_expert_ref[i]
    prev = tile_expert_ref[jnp.maximum(i - 1, 0)]

    @pl.when((i == 0) | (e != prev))
    def _():
        wg_bf[...] = wg_ref[0].astype(jnp.bfloat16)
        wu_bf[...] = wu_ref[0].astype(jnp.bfloat16)

    @pl.when(tile_valid_ref[i] != 0)
    def _():
        x = x_ref[...].astype(jnp.bfloat16)
        hg = jnp.dot(x, wg_bf[...], preferred_element_type=jnp.float32) + bg_ref[0]
        hu = jnp.dot(x, wu_bf[...], preferred_element_type=jnp.float32) + bu_ref[0]
        glu = jnp.minimum(hg, SWIGLU_LIMIT)
        up = jnp.clip(hu, -SWIGLU_LIMIT, SWIGLU_LIMIT)
        act = (up + 1.0) * glu * jax.nn.sigmoid(SWIGLU_ALPHA * glu)
        h_ref[...] = act.astype(h_ref.dtype)

    @pl.when(tile_valid_ref[i] == 0)
    def _():
        h_ref[...] = jnp.zeros_like(h_ref)


def _moe_down_kernel(tile_expert_ref, tile_valid_ref, h_ref, wd_ref, bd_ref, y_ref, wd_bf):
    i = pl.program_id(1)
    e = tile_expert_ref[i]
    prev = tile_expert_ref[jnp.maximum(i - 1, 0)]

    @pl.when((i == 0) | (e != prev))
    def _():
        wd_bf[...] = wd_ref[0].astype(jnp.bfloat16)

    @pl.when(tile_valid_ref[i] != 0)
    def _():
        y_ref[...] = jnp.dot(h_ref[...], wd_bf[...], preferred_element_type=jnp.float32) + bd_ref[0]

    @pl.when(tile_valid_ref[i] == 0)
    def _():
        y_ref[...] = jnp.zeros_like(y_ref)


def _moe_grouped(x_sorted, tile_expert, tile_valid, w_gate_up, b_gate_up, w_down, b_down):
    p_pad, d = x_sorted.shape
    n_exp, _, two_f = w_gate_up.shape
    f = two_f // 2
    tm, fc, dc = MOE_TILE_M, MOE_F_CHUNK, MOE_D_CHUNK
    nt = p_pad // tm
    nfc = f // fc
    ndc = d // dc
    bgu = b_gate_up.reshape(n_exp, 1, two_f)
    bdn = b_down.reshape(n_exp, 1, d)

    h = pl.pallas_call(
        _moe_up_kernel,
        out_shape=jax.ShapeDtypeStruct((p_pad, f), jnp.bfloat16),
        grid_spec=pltpu.PrefetchScalarGridSpec(
            num_scalar_prefetch=2,
            grid=(nfc, nt),
            in_specs=[
                pl.BlockSpec((tm, d), lambda c, i, te, tv: (i, 0)),
                pl.BlockSpec((1, d, fc), lambda c, i, te, tv: (te[i], 0, c)),
                pl.BlockSpec((1, d, fc), lambda c, i, te, tv: (te[i], 0, nfc + c)),
                pl.BlockSpec((1, 1, fc), lambda c, i, te, tv: (te[i], 0, c)),
                pl.BlockSpec((1, 1, fc), lambda c, i, te, tv: (te[i], 0, nfc + c)),
            ],
            out_specs=pl.BlockSpec((tm, fc), lambda c, i, te, tv: (i, c)),
            scratch_shapes=[pltpu.VMEM((d, fc), jnp.bfloat16), pltpu.VMEM((d, fc), jnp.bfloat16)],
        ),
        compiler_params=pltpu.CompilerParams(
            dimension_semantics=("arbitrary", "arbitrary"),
            vmem_limit_bytes=VMEM_LIMIT_BYTES),
        name="moe_up",
    )(tile_expert, tile_valid, x_sorted, w_gate_up, w_gate_up, bgu, bgu)

    y = pl.pallas_call(
        _moe_down_kernel,
        out_shape=jax.ShapeDtypeStruct((p_pad, d), jnp.float32),
        grid_spec=pltpu.PrefetchScalarGridSpec(
            num_scalar_prefetch=2,
            grid=(ndc, nt),
            in_specs=[
                pl.BlockSpec((tm, f), lambda c, i, te, tv: (i, 0)),
                pl.BlockSpec((1, f, dc), lambda c, i, te, tv: (te[i], 0, c)),
                pl.BlockSpec((1, 1, dc), lambda c, i, te, tv: (te[i], 0, c)),
            ],
            out_specs=pl.BlockSpec((tm, dc), lambda c, i, te, tv: (i, c)),
            scratch_shapes=[pltpu.VMEM((f, dc), jnp.bfloat16)],
        ),
        compiler_params=pltpu.CompilerParams(
            dimension_semantics=("arbitrary", "arbitrary"),
            vmem_limit_bytes=VMEM_LIMIT_BYTES),
        name="moe_down",
    )(tile_expert, tile_valid, h, w_down, bdn)
    return y


def _moe_combine_kernel(slot_ref, h_ref, w_ref, g_ref, b_ref, y_hbm, out_ref, buf, sems):
    i = pl.program_id(0)
    nt = pl.num_programs(0)
    tt = h_ref.shape[0]

    def start_tile(t, s):
        for j in range(tt):
            for k in range(TOP_K):
                src = slot_ref[(t * tt + j) * TOP_K + k]
                pltpu.make_async_copy(y_hbm.at[pl.ds(src, 1)], buf.at[s, pl.ds(k * tt + j, 1)], sems.at[s]).start()

    @pl.when(i == 0)
    def _():
        start_tile(0, 0)

    @pl.when(i + 1 < nt)
    def _():
        start_tile(i + 1, (i + 1) % 2)

    slot = i % 2
    pltpu.make_async_copy(y_hbm.at[pl.ds(0, TOP_K * tt)], buf.at[slot], sems.at[slot]).wait()
    w = w_ref[...]
    ffn = buf[slot, 0:tt] * w[:, 0:1]
    for k in range(1, TOP_K):
        ffn = ffn + buf[slot, k * tt:(k + 1) * tt] * w[:, k:k + 1]
    x = DEEPNORM_ALPHA * h_ref[...] + ffn
    mu = jnp.mean(x, -1, keepdims=True)
    xc = x - mu
    var = jnp.mean(xc * xc, -1, keepdims=True)
    out_ref[...] = xc * lax.rsqrt(var + LN_EPS) * g_ref[...] + b_ref[...]


def _moe_combine_ln(h, y, pair_slot, top_w, ln_g, ln_b):
    n, d = h.shape
    tt = MOE_ROUTE_TILE
    return pl.pallas_call(
        _moe_combine_kernel,
        out_shape=jax.ShapeDtypeStruct((n, d), jnp.float32),
        grid_spec=pltpu.PrefetchScalarGridSpec(
            num_scalar_prefetch=1,
            grid=(n // tt,),
            in_specs=[pl.BlockSpec((tt, d), lambda i, ps: (i, 0)),
                      pl.BlockSpec((tt, TOP_K), lambda i, ps: (i, 0)),
                      pl.BlockSpec((1, d), lambda i, ps: (0, 0)),
                      pl.BlockSpec((1, d), lambda i, ps: (0, 0)),
                      pl.BlockSpec(memory_space=pl.ANY)],
            out_specs=pl.BlockSpec((tt, d), lambda i, ps: (i, 0)),
            scratch_shapes=[pltpu.VMEM((2, TOP_K * tt, d), jnp.float32), pltpu.SemaphoreType.DMA((2,))],
        ),
        compiler_params=pltpu.CompilerParams(
            dimension_semantics=("arbitrary",), vmem_limit_bytes=VMEM_LIMIT_BYTES),
        name="moe_combine",
    )(pair_slot.reshape(-1), h, top_w, ln_g.reshape(1, d), ln_b.reshape(1, d), y)


def moe_ffn_ln(x, top_w, top_idx, w_gate_up, b_gate_up, w_down, b_down, ln_g, ln_b):
    n, d = x.shape
    tm = MOE_TILE_M
    nt = n * TOP_K // tm + N_EXPERTS
    onehot = (top_idx[:, :, None] == jnp.arange(N_EXPERTS, dtype=top_idx.dtype)).astype(jnp.int32)
    routed = jnp.sum(onehot, axis=1)
    before = jnp.cumsum(routed, axis=0) - routed
    counts = before[-1] + routed[-1]
    tiles_per = (counts + tm - 1) // tm
    tile_end = jnp.cumsum(tiles_per)
    first_row = (tile_end - tiles_per) * tm
    pair_slot = jnp.sum(onehot * (before + first_row)[:, None, :], axis=2)
    tile_ids = jnp.arange(nt, dtype=jnp.int32)
    tile_expert = jnp.minimum(jnp.sum((tile_ids[:, None] >= tile_end[None, :]).astype(jnp.int32), axis=1),
                              N_EXPERTS - 1).astype(jnp.int32)
    tile_valid = (tile_ids < tile_end[-1]).astype(jnp.int32)

    x_sorted = _moe_scatter_rows(x, pair_slot, nt * tm)
    y = _moe_grouped(x_sorted, tile_expert, tile_valid, w_gate_up, b_gate_up, w_down, b_down)
    return _moe_combine_ln(x, y, pair_slot, top_w, ln_g, ln_b)


def kernel(x_prompt, x_sample, cache_win0_kv, cache_win1_kv, cache_win2_kv, cache_mem_kv,
           state_wkv, state_shift, mem_prompt,
           w_in, w_gate, b_gate, w_branch, w_out, w_mem_kv,
           mu_rkv, mu_wag, w0, w1, w2, a0, a1, a2, g1, g2, k_k, k_a, r_k, gn_g, gn_b,
           ln1_g, ln1_b, w_router, b_router, w_gate_up, b_gate_up, w_down, b_down, ln2_g, ln2_b):
    bf16 = jnp.bfloat16
    xp, xs = x_prompt, x_sample
    bp, tp, d = xp.shape
    bs, ts, _ = xs.shape
    n_prompt = bp * tp
    l = 0
    lp = {"w_in": w_in[l], "mu_rkv": mu_rkv[l], "mu_wag": mu_wag[l],
          "w0": w0[l], "w1": w1[l], "w2": w2[l], "a0": a0[l], "a1": a1[l], "a2": a2[l],
          "g1": g1[l], "g2": g2[l], "k_k": k_k[l], "k_a": k_a[l], "r_k": r_k[l],
          "gn_g": gn_g[l], "gn_b": gn_b[l]}

    u_all = jnp.concatenate([xp.reshape(-1, d), xs.reshape(-1, d)], axis=0)
    u_bf = u_all.astype(bf16)
    proj = matmul_bias(u_bf, w_in[l].astype(bf16), tile_m=PROJ_TILE_M, name="proj_in")
    gate = matmul_bias(u_bf, w_gate[l].astype(bf16), b_gate[l], tile_m=PROJ_TILE_M, sigmoid=True, name="proj_gate")
    mem_rows = mem_prompt.reshape(-1, d)
    mem_kv_p = matmul_bias(mem_rows.astype(bf16), w_mem_kv[l].astype(bf16), tile_m=mem_rows.shape[0], name="proj_mem")
    proj_p = proj[:n_prompt].reshape(bp, tp, PROJ_WIDTH)
    proj_s = proj[n_prompt:].reshape(bs, ts, PROJ_WIDTH)
    rkv0, qm0 = 3 * ATT_WIDTH, 3 * ATT_WIDTH + 3 * RWKV_WIDTH

    heads = lambda t: t.reshape(t.shape[0], t.shape[1], N_ATT_HEADS, HEAD_DIM)
    qa_p, ka_p, va_p = (heads(proj_p[..., i * ATT_WIDTH:(i + 1) * ATT_WIDTH]) for i in range(3))
    ka_s, va_s = (heads(proj_s[..., i * ATT_WIDTH:(i + 1) * ATT_WIDTH]) for i in (1, 2))
    outs, lses, win_p, win_s = [], [], [], []
    for g, (win, dil) in enumerate(DIL_GROUPS):
        hs = slice(g * HEADS_PER_GROUP, (g + 1) * HEADS_PER_GROUP)
        o, lse = dilated_attention_prompt(qa_p[:, :, hs], ka_p[:, :, hs], va_p[:, :, hs], g, dil)
        outs.append(o.reshape(n_prompt, ATT_OUT))
        lses.append(lse.reshape(n_prompt, ATT_OUT))
        keep = min(win, tp)
        win_p.append(jnp.stack([ka_p[:, tp - keep:, hs], va_p[:, tp - keep:, hs]], axis=2))
        win_s.append(jnp.stack([ka_s[:, :, hs], va_s[:, :, hs]], axis=2))
    o_att_p = att_combine(outs, lses)
    o_att_s = dilated_attention_step(proj_s, cache_win0_kv[l], cache_win1_kv[l], cache_win2_kv[l])

    o_mem_p = memory_attention(proj_p[..., qm0:], mem_kv_p.reshape(bp, MEM_TOKENS, 2 * MEM_WIDTH))
    o_mem_s = memory_attention(proj_s[..., qm0:], cache_mem_kv[l].reshape(bs, MEM_TOKENS, 2 * MEM_WIDTH))

    u0 = jnp.zeros((bp, d), xp.dtype)
    s0 = jnp.zeros((bp, RWKV_HEADS, RWKV_HEAD_DIM, RWKV_HEAD_DIM), jnp.float32)
    o_rwkv_p, wkv_p = rwkv_branch(xp, u0, s0, proj_p[..., rkv0:qm0], lp)
    o_rwkv_s, wkv_s = rwkv_branch(xs, state_shift[l], state_wkv[l], proj_s[..., rkv0:qm0], lp)

    cat = lambda a, b, w: jnp.concatenate([a.reshape(-1, w), b.reshape(-1, w)], axis=0)
    h, top_w, top_idx = merge_ln_router(
        cat(o_att_p, o_att_s, ATT_OUT), cat(o_rwkv_p, o_rwkv_s, RWKV_WIDTH), cat(o_mem_p, o_mem_s, MEM_WIDTH),
        gate, u_all, w_branch[l].astype(bf16), w_out[l].astype(bf16), ln1_g[l], ln1_b[l], w_router[l], b_router[l])
    y_all = moe_ffn_ln(h, top_w, top_idx, w_gate_up[l], b_gate_up[l], w_down[l], b_down[l], ln2_g[l], ln2_b[l])
    yp = y_all[:n_prompt].reshape(bp, tp, d)
    ys = y_all[n_prompt:].reshape(bs, ts, d)
    mem_kv_out = mem_kv_p.reshape(bp, MEM_TOKENS, 2, MEM_HEADS, MEM_HEAD_DIM)
    return (yp, ys,
            win_p[0][None], win_p[1][None], win_p[2][None], wkv_p.astype(xp.dtype)[None],
            xp[:, -1][None], mem_kv_out[None],
            win_s[0].astype(cache_win0_kv.dtype)[None], win_s[1].astype(cache_win0_kv.dtype)[None],
            win_s[2].astype(cache_win0_kv.dtype)[None], wkv_s.astype(state_wkv.dtype)[None],
            xs[:, -1].astype(state_shift.dtype)[None])
```

```python
import functools

import jax
import jax.numpy as jnp
from jax import lax
from jax.experimental import pallas as pl
from jax.experimental.pallas import tpu as pltpu

D_MODEL = 2048
DEPTH = 1
HEAD_DIM = 64
DIL_GROUPS = ((128, 1), (512, 4), (2048, 16))
HEADS_PER_GROUP = 4
N_ATT_HEADS = HEADS_PER_GROUP * len(DIL_GROUPS)
ATT_WIDTH = N_ATT_HEADS * HEAD_DIM
ATT_OUT = HEADS_PER_GROUP * HEAD_DIM
BAND_BLOCK = 128
N_BACK = 128
GROUP_WIDTH = HEADS_PER_GROUP * HEAD_DIM
RWKV_HEADS = 12
RWKV_HEAD_DIM = 64
RWKV_WIDTH = RWKV_HEADS * RWKV_HEAD_DIM
GN_EPS = 64e-5
MEM_TOKENS = 256
MEM_HEADS = 4
MEM_HEAD_DIM = 128
MEM_WIDTH = MEM_HEADS * MEM_HEAD_DIM
PROJ_WIDTH = 3 * ATT_WIDTH + 3 * RWKV_WIDTH + MEM_WIDTH
N_BRANCHES = 3
N_EXPERTS = 32
TOP_K = 4
D_EXPERT = 2048
SWIGLU_LIMIT = 7.0
SWIGLU_ALPHA = 1.702
LN_EPS = 1e-5
DEEPNORM_ALPHA = (2.0 * DEPTH) ** 0.25
NEG_INF = -1e30

VMEM_LIMIT_BYTES = 56 * 1024 * 1024
SUBLANES = 8
LANES = 128
MOE_TILE_M = 256
MOE_F_CHUNK = 1024
MOE_D_CHUNK = 1024
MOE_ROUTE_TILE = 64
RWKV_CHUNK = 64
RWKV_SEQ_PER_STEP = 8
RWKV_PRE_TILE_M = 256
PROJ_TILE_M = 1088
MATMUL_TILE_N = 512
MERGE_TILE_M = 256
ATT_COMBINE_TILE = 512
MEM_ATTN_TILE_Q = 512
ATTN_STEP_SEQS = 2


def _matmul_kernel(x_ref, w_ref, b_ref, o_ref, *, sigmoid):
    y = jnp.dot(x_ref[...], w_ref[...], preferred_element_type=jnp.float32) + b_ref[...]
    o_ref[...] = jax.nn.sigmoid(y) if sigmoid else y


def matmul_bias(x, w, bias=None, *, tile_m, sigmoid=False, name="matmul"):
    m, k = x.shape
    n = w.shape[1]
    tn = MATMUL_TILE_N if n % MATMUL_TILE_N == 0 else n
    if bias is None:
        bias = jnp.zeros((n,), jnp.float32)
    return pl.pallas_call(
        functools.partial(_matmul_kernel, sigmoid=sigmoid),
        out_shape=jax.ShapeDtypeStruct((m, n), jnp.float32),
        grid=(n // tn, m // tile_m),
        in_specs=[pl.BlockSpec((tile_m, k), lambda j, i: (i, 0)),
                  pl.BlockSpec((k, tn), lambda j, i: (0, j)),
                  pl.BlockSpec((1, tn), lambda j, i: (0, j))],
        out_specs=pl.BlockSpec((tile_m, tn), lambda j, i: (i, j)),
        compiler_params=pltpu.CompilerParams(
            dimension_semantics=("arbitrary", "arbitrary"), vmem_limit_bytes=VMEM_LIMIT_BYTES),
        name=name,
    )(x, w, bias.reshape(1, n))


def _merge_kernel(att_ref, rwkv_ref, mem_ref, gate_ref, x_ref, wb_ref, wo_ref, g_ref, b_ref,
                  wr_hi_ref, wr_lo_ref, br_ref, h_ref, topw_ref, topi_ref):
    bf16, f32 = jnp.bfloat16, jnp.float32
    d = x_ref.shape[1]
    dot = functools.partial(jnp.dot, preferred_element_type=f32)
    y = dot(att_ref[...].astype(bf16), wb_ref[0:ATT_OUT]) * gate_ref[:, 0:d]
    y = y + dot(rwkv_ref[...].astype(bf16), wb_ref[ATT_OUT:ATT_OUT + RWKV_WIDTH]) * gate_ref[:, d:2 * d]
    y = y + dot(mem_ref[...].astype(bf16), wb_ref[ATT_OUT + RWKV_WIDTH:]) * gate_ref[:, 2 * d:]
    z = DEEPNORM_ALPHA * x_ref[...] + dot(y.astype(bf16), wo_ref[...])
    mu = jnp.mean(z, -1, keepdims=True)
    zc = z - mu
    var = jnp.mean(zc * zc, -1, keepdims=True)
    h = zc * lax.rsqrt(var + LN_EPS) * g_ref[...] + b_ref[...]
    h_ref[...] = h

    h_hi = h.astype(bf16)
    h_lo = (h - h_hi.astype(f32)).astype(bf16)
    logits = (dot(h_hi, wr_hi_ref[...]) + dot(h_hi, wr_lo_ref[...]) + dot(h_lo, wr_hi_ref[...])) + br_ref[...]
    lane = lax.broadcasted_iota(jnp.int32, logits.shape, 1)
    work = jnp.where(lane < N_EXPERTS, logits, -jnp.inf)
    tops, ids = [], []
    for _ in range(TOP_K):
        m = jnp.max(work, -1, keepdims=True)
        idx = jnp.min(jnp.where(work == m, lane, LANES), -1, keepdims=True)
        tops.append(m)
        ids.append(idx)
        work = jnp.where(lane == idx, -jnp.inf, work)
    e = [jnp.exp(t - tops[0]) for t in tops]
    den = e[0] + e[1] + e[2] + e[3]
    tw = jnp.zeros(logits.shape, f32)
    ti = jnp.zeros(logits.shape, jnp.int32)
    for k in range(TOP_K):
        tw = jnp.where(lane == k, e[k] / den, tw)
        ti = jnp.where(lane == k, ids[k], ti)
    topw_ref[...] = tw
    topi_ref[...] = ti


def merge_ln_router(o_att, o_rwkv, o_mem, gate, x, w_branch, w_out, ln_g, ln_b, w_router, b_router):
    n, d = x.shape
    tm = MERGE_TILE_M
    wr = jnp.zeros((d, LANES), jnp.float32).at[:, :N_EXPERTS].set(w_router)
    wr_hi = wr.astype(jnp.bfloat16)
    wr_lo = (wr - wr_hi.astype(jnp.float32)).astype(jnp.bfloat16)
    br = jnp.zeros((1, LANES), jnp.float32).at[0, :N_EXPERTS].set(b_router)
    rows = lambda w: pl.BlockSpec((tm, w), lambda i: (i, 0))
    full = lambda a: pl.BlockSpec(a.shape, lambda i: (0, 0))
    h, topw, topi = pl.pallas_call(
        _merge_kernel,
        out_shape=(jax.ShapeDtypeStruct((n, d), jnp.float32),
                   jax.ShapeDtypeStruct((n, LANES), jnp.float32),
                   jax.ShapeDtypeStruct((n, LANES), jnp.int32)),
        grid=(n // tm,),
        in_specs=[rows(ATT_OUT), rows(RWKV_WIDTH), rows(MEM_WIDTH), rows(3 * d), rows(d),
                  full(w_branch), full(w_out), pl.BlockSpec((1, d), lambda i: (0, 0)),
                  pl.BlockSpec((1, d), lambda i: (0, 0)), full(wr_hi), full(wr_lo), full(br)],
        out_specs=(rows(d), rows(LANES), rows(LANES)),
        compiler_params=pltpu.CompilerParams(
            dimension_semantics=("arbitrary",), vmem_limit_bytes=VMEM_LIMIT_BYTES),
        name="merge_ln_router",
    )(o_att, o_rwkv, o_mem, gate, x, w_branch, w_out, ln_g.reshape(1, d), ln_b.reshape(1, d), wr_hi, wr_lo, br)
    return h, topw[:, :TOP_K], topi[:, :TOP_K]


def _alibi_slope(head):
    return 2.0 ** (-8.0 * (head + 1) / N_ATT_HEADS)


def _dot_nt(a, b):
    return lax.dot_general(a, b, (((1,), (1,)), ((), ())), preferred_element_type=jnp.float32)


def _dil_attn_prompt_kernel(q_ref, kc_ref, kp_ref, vc_ref, vp_ref, o_ref, lse_ref, *, group, dil):
    bf16 = jnp.bfloat16
    blk = BAND_BLOCK
    i = pl.program_id(1)
    qi = lax.broadcasted_iota(jnp.int32, (blk, blk), 0)
    kj = lax.broadcasted_iota(jnp.int32, (blk, blk), 1)
    dist_c = qi - kj
    dist_p = dist_c + blk
    valid_c = dist_c >= 0
    valid_p = (dist_p <= blk) & (i > 0)
    scale = HEAD_DIM ** -0.5
    for h in range(HEADS_PER_GROUP):
        slope = _alibi_slope(group * HEADS_PER_GROUP + h) * dil
        q = q_ref[0, h].astype(bf16)
        s_c = _dot_nt(q, kc_ref[0, h].astype(bf16)) * scale - slope * dist_c.astype(jnp.float32)
        s_p = _dot_nt(q, kp_ref[0, h].astype(bf16)) * scale - slope * dist_p.astype(jnp.float32)
        s_c = jnp.where(valid_c, s_c, NEG_INF)
        s_p = jnp.where(valid_p, s_p, NEG_INF)
        m = jnp.maximum(jnp.max(s_c, -1, keepdims=True), jnp.max(s_p, -1, keepdims=True))
        p_c = jnp.exp(s_c - m)
        p_p = jnp.exp(s_p - m)
        den = jnp.sum(p_c, -1, keepdims=True) + jnp.sum(p_p, -1, keepdims=True)
        pv = (jnp.dot(p_c.astype(bf16), vc_ref[0, h].astype(bf16), preferred_element_type=jnp.float32)
              + jnp.dot(p_p.astype(bf16), vp_ref[0, h].astype(bf16), preferred_element_type=jnp.float32))
        o_ref[0, h] = pv / den
        lse_ref[0, h] = jnp.broadcast_to(m + jnp.log(den), (blk, HEAD_DIM))


def dilated_attention_prompt(q, k, v, group, dil):
    b, t, h, e = q.shape
    l = t // dil
    assert l % BAND_BLOCK == 0

    def by_residue(a):
        return a.reshape(b, l, dil, h, e).transpose(0, 2, 3, 1, 4).reshape(b * dil, h, l, e)

    def from_residue(a):
        return a.reshape(b, dil, h, l, e).transpose(0, 3, 1, 2, 4).reshape(b, t, h, e)

    cur = pl.BlockSpec((1, h, BAND_BLOCK, e), lambda n, i: (n, 0, i, 0))
    prev = pl.BlockSpec((1, h, BAND_BLOCK, e), lambda n, i: (n, 0, jnp.maximum(i - 1, 0), 0))
    qr, kr, vr = by_residue(q), by_residue(k), by_residue(v)
    o, lse = pl.pallas_call(
        functools.partial(_dil_attn_prompt_kernel, group=group, dil=dil),
        out_shape=(jax.ShapeDtypeStruct(qr.shape, jnp.float32), jax.ShapeDtypeStruct(qr.shape, jnp.float32)),
        grid=(b * dil, l // BAND_BLOCK),
        in_specs=[cur, cur, prev, cur, prev],
        out_specs=(cur, cur),
        compiler_params=pltpu.CompilerParams(
            dimension_semantics=("arbitrary", "arbitrary"), vmem_limit_bytes=VMEM_LIMIT_BYTES),
        name="dil_attn_prompt_g%d" % group,
    )(qr, kr, kr, vr, vr)
    return from_residue(o), from_residue(lse)


def _att_combine_kernel(o0, o1, o2, l0, l1, l2, out_ref):
    a, b, c = l0[...], l1[...], l2[...]
    m = jnp.maximum(jnp.maximum(a, b), c)
    ea, eb, ec = jnp.exp(a - m), jnp.exp(b - m), jnp.exp(c - m)
    out_ref[...] = (ea * o0[...] + eb * o1[...] + ec * o2[...]) / (ea + eb + ec)


def att_combine(outs, lses):
    n, w = outs[0].shape
    spec = pl.BlockSpec((ATT_COMBINE_TILE, w), lambda i: (i, 0))
    return pl.pallas_call(
        _att_combine_kernel,
        out_shape=jax.ShapeDtypeStruct((n, w), jnp.float32),
        grid=(n // ATT_COMBINE_TILE,),
        in_specs=[spec] * 6,
        out_specs=spec,
        compiler_params=pltpu.CompilerParams(dimension_semantics=("arbitrary",)),
        name="att_combine",
    )(*outs, *lses)


def _mem_attn_kernel(q_ref, kv_ref, o_ref):
    bf16 = jnp.bfloat16
    e = MEM_HEAD_DIM
    scale = e ** -0.5
    for h in range(MEM_HEADS):
        q = q_ref[0, :, h * e:(h + 1) * e].astype(bf16)
        if len(kv_ref.shape) == 3:
            k = kv_ref[0, :, h * e:(h + 1) * e].astype(bf16)
            v = kv_ref[0, :, (MEM_HEADS + h) * e:(MEM_HEADS + h + 1) * e].astype(bf16)
        else:
            k = kv_ref[0, :, 0, h, :].astype(bf16)
            v = kv_ref[0, :, 1, h, :].astype(bf16)
        s = _dot_nt(q, k) * scale
        m = jnp.max(s, -1, keepdims=True)
        p = jnp.exp(s - m)
        p = p / jnp.sum(p, -1, keepdims=True)
        o_ref[0, :, h * e:(h + 1) * e] = jnp.dot(p.astype(bf16), v, preferred_element_type=jnp.float32)


def memory_attention(q, q_col_block, mem_kv, blocks_per_seq):
    b = mem_kv.shape[0]
    tq = q.shape[1]
    w = MEM_WIDTH
    kv_zeros = (0,) * (mem_kv.ndim - 1)
    return pl.pallas_call(
        _mem_attn_kernel,
        out_shape=jax.ShapeDtypeStruct((b * blocks_per_seq, tq, w), jnp.float32),
        grid=(b, blocks_per_seq),
        in_specs=[pl.BlockSpec((1, tq, w), lambda n, i: (n * blocks_per_seq + i, 0, q_col_block)),
                  pl.BlockSpec((1,) + mem_kv.shape[1:], lambda n, i: (n,) + kv_zeros)],
        out_specs=pl.BlockSpec((1, tq, w), lambda n, i: (n * blocks_per_seq + i, 0, 0)),
        compiler_params=pltpu.CompilerParams(
            dimension_semantics=("arbitrary", "arbitrary"), vmem_limit_bytes=VMEM_LIMIT_BYTES),
        name="mem_attn",
    )(q, mem_kv)


def _dil_attn_step_kernel(qkv_ref, c0_ref, c1_ref, c2_ref, o_ref, *, n_new):
    bf16, f32 = jnp.bfloat16, jnp.float32
    scale = HEAD_DIM ** -0.5
    caches = (c0_ref, c1_ref, c2_ref)
    tq = lax.broadcasted_iota(jnp.int32, (n_new, n_new), 0)
    tk = lax.broadcasted_iota(jnp.int32, (n_new, n_new), 1)
    for b, h in [(b, h) for b in range(qkv_ref.shape[0]) for h in range(HEADS_PER_GROUP)]:
        outs, lses = [], []
        for g, (win, dil) in enumerate(DIL_GROUPS):
            head = g * HEADS_PER_GROUP + h
            slope = _alibi_slope(head)
            lanes = slice(head * HEAD_DIM, (head + 1) * HEAD_DIM)
            q = qkv_ref[b, :, lanes].astype(bf16)
            k_new = qkv_ref[b, :, ATT_WIDTH + head * HEAD_DIM:ATT_WIDTH + (head + 1) * HEAD_DIM].astype(bf16)
            v_new = qkv_ref[b, :, 2 * ATT_WIDTH + head * HEAD_DIM:2 * ATT_WIDTH + (head + 1) * HEAD_DIM].astype(bf16)
            s = jnp.dot(q, caches[g][b, 0, h].astype(bf16), preferred_element_type=f32) * scale
            dist = (win + lax.broadcasted_iota(jnp.int32, (n_new, win), 0)
                    - lax.broadcasted_iota(jnp.int32, (n_new, win), 1))
            valid = ((dist & (dil - 1)) == 0) & (dist <= N_BACK * dil)
            s = jnp.where(valid, s - slope * dist.astype(f32), NEG_INF)
            sn = _dot_nt(q, k_new) * scale
            dn = tq - tk
            sn = jnp.where((dn >= 0) & ((dn & (dil - 1)) == 0), sn - slope * dn.astype(f32), NEG_INF)
            m = jnp.maximum(jnp.max(s, axis=1, keepdims=True), jnp.max(sn, axis=1, keepdims=True))
            p = jnp.exp(s - m)
            pn = jnp.exp(sn - m)
            den = jnp.sum(p, axis=1, keepdims=True) + jnp.sum(pn, axis=1, keepdims=True)
            pv = (_dot_nt(p.astype(bf16), caches[g][b, 1, h].astype(bf16))
                  + jnp.dot(pn.astype(bf16), v_new, preferred_element_type=f32))
            outs.append(pv / den)
            lses.append(m + jnp.log(den))
        mm = jnp.maximum(jnp.maximum(lses[0], lses[1]), lses[2])
        w = [jnp.exp(l - mm) for l in lses]
        o_ref[b, :, h * HEAD_DIM:(h + 1) * HEAD_DIM] = (
            (w[0] * outs[0] + w[1] * outs[1] + w[2] * outs[2]) / (w[0] + w[1] + w[2]))


def dilated_attention_step(p_s, cache0, cache1, cache2):
    b, t, _ = p_s.shape
    feat_major = lambda c: jnp.transpose(c, (0, 2, 3, 4, 1))
    ns = ATTN_STEP_SEQS
    win_spec = lambda w: pl.BlockSpec((ns, 2, HEADS_PER_GROUP, HEAD_DIM, w), lambda n: (n, 0, 0, 0, 0))
    return pl.pallas_call(
        functools.partial(_dil_attn_step_kernel, n_new=t),
        out_shape=jax.ShapeDtypeStruct((b, t, GROUP_WIDTH), jnp.float32),
        grid=(b // ns,),
        in_specs=[pl.BlockSpec((ns, t, 3 * ATT_WIDTH), lambda n: (n, 0, 0))] + [win_spec(w) for w, _ in DIL_GROUPS],
        out_specs=pl.BlockSpec((ns, t, GROUP_WIDTH), lambda n: (n, 0, 0)),
        compiler_params=pltpu.CompilerParams(
            dimension_semantics=("arbitrary",), vmem_limit_bytes=VMEM_LIMIT_BYTES),
        name="dil_attn_step",
    )(p_s, feat_major(cache0), feat_major(cache1), feat_major(cache2))


def _rwkv_pre_kernel(u_ref, up_ref, p_ref, pp_ref, mu_rkv_ref, mu_wag_ref, w0_ref, w1_ref, w2_ref,
                     a0_ref, a1_ref, a2_ref, g1_ref, g2_ref, kk_ref, ka_ref, rk_ref,
                     r_ref, lw_ref, k_ref, v_ref, an_ref, b_ref, g_ref, bonus_ref):
    bf16, f32 = jnp.bfloat16, jnp.float32
    dot = functools.partial(jnp.dot, preferred_element_type=f32)
    w = RWKV_WIDTH
    u = u_ref[...]
    du = up_ref[...] - u
    p = p_ref[...]
    rkv = p + (pp_ref[...] - p) * mu_rkv_ref[...]
    r, k, v = rkv[:, :w], rkv[:, w:2 * w], rkv[:, 2 * w:]
    xw = (u + du * mu_wag_ref[0:1]).astype(bf16)
    xa = (u + du * mu_wag_ref[1:2]).astype(bf16)
    xg = (u + du * mu_wag_ref[2:3]).astype(bf16)
    y = -(w0_ref[...] + dot(jnp.tanh(dot(xw, w1_ref[...])).astype(bf16), w2_ref[...]))
    softplus = jnp.maximum(y, 0.0) + jnp.log1p(jnp.exp(-jnp.abs(y)))
    lw_ref[...] = -jnp.exp(-softplus - 0.5)
    a = jax.nn.sigmoid(a0_ref[...] + dot(dot(xa, a1_ref[...]).astype(bf16), a2_ref[...]))
    g_ref[...] = dot(jax.nn.sigmoid(dot(xg, g1_ref[...])).astype(bf16), g2_ref[...])

    rr = lax.broadcasted_iota(jnp.int32, (LANES, LANES), 0) // RWKV_HEAD_DIM
    cc = lax.broadcasted_iota(jnp.int32, (LANES, LANES), 1) // RWKV_HEAD_DIM
    seg_ones = jnp.where(rr == cc, 1.0, 0.0).astype(bf16)

    def head_sum(x):
        hi = x.astype(bf16)
        lo = (x - hi.astype(f32)).astype(bf16)
        cols = [dot(hi[:, j:j + LANES], seg_ones) + dot(lo[:, j:j + LANES], seg_ones)
                for j in range(0, w, LANES)]
        return jnp.concatenate(cols, axis=1)

    kk = k * kk_ref[...]
    kk = kk / jnp.maximum(jnp.sqrt(head_sum(kk * kk)), 1e-12)
    k2 = k * (1.0 + (a - 1.0) * ka_ref[...])
    r_ref[...] = r
    k_ref[...] = k2
    v_ref[...] = v
    an_ref[...] = -kk
    b_ref[...] = kk * a
    bonus_ref[...] = head_sum(r * k2 * rk_ref[...]) * v


def rwkv_pre(u, u_prev, proj, p_prev, lp):
    n, d = u.shape
    tm = RWKV_PRE_TILE_M
    w3 = 3 * RWKV_WIDTH
    bf16 = jnp.bfloat16
    row = lambda x: x.reshape(1, -1)
    consts = [row(lp["mu_rkv"]), lp["mu_wag"], row(lp["w0"]), lp["w1"].astype(bf16), lp["w2"].astype(bf16),
              row(lp["a0"]), lp["a1"].astype(bf16), lp["a2"].astype(bf16), lp["g1"].astype(bf16),
              lp["g2"].astype(bf16), row(lp["k_k"]), row(lp["k_a"]), row(lp["r_k"])]
    full = lambda a: pl.BlockSpec(a.shape, lambda i: (0, 0))
    out_spec = pl.BlockSpec((tm, RWKV_WIDTH), lambda i: (i, 0))
    return pl.pallas_call(
        _rwkv_pre_kernel,
        out_shape=tuple(jax.ShapeDtypeStruct((n, RWKV_WIDTH), jnp.float32) for _ in range(8)),
        grid=(n // tm,),
        in_specs=[pl.BlockSpec((tm, d), lambda i: (i, 0)), pl.BlockSpec((tm, d), lambda i: (i, 0)),
                  pl.BlockSpec((tm, w3), lambda i: (i, 1)), pl.BlockSpec((tm, w3), lambda i: (i, 0))]
                 + [full(c) for c in consts],
        out_specs=tuple(out_spec for _ in range(8)),
        compiler_params=pltpu.CompilerParams(
            dimension_semantics=("arbitrary",), vmem_limit_bytes=VMEM_LIMIT_BYTES),
        name="rwkv_pre",
    )(u, u_prev, proj, p_prev, *consts)


def _bmm(a, b):
    return lax.dot_general(a, b, (((2,), (1,)), ((0,), (0,))), preferred_element_type=jnp.float32)


def _bmm_nt(a, b):
    return lax.dot_general(a, b, (((2,), (2,)), ((0,), (0,))), preferred_element_type=jnp.float32)


def _bmm_tn(a, b):
    return lax.dot_general(a, b, (((1,), (1,)), ((0,), (0,))), preferred_element_type=jnp.float32)


def _rwkv_chunk_kernel(r_ref, lw_ref, k_ref, v_ref, a_ref, b_ref, g_ref, bonus_ref, gng_ref, gnb_ref, s0_ref,
                       o_ref, sT_ref, s_scr, *, n_seq, seq_rows, chunk):
    bf16, f32 = jnp.bfloat16, jnp.float32
    c = pl.program_id(1)
    C = chunk
    H, N = RWKV_HEADS, RWKV_HEAD_DIM

    @pl.when(c == 0)
    def _():
        s_scr[...] = s0_ref[...]

    row = lax.broadcasted_iota(jnp.int32, (C, C), 0)
    col = lax.broadcasted_iota(jnp.int32, (C, C), 1)
    strict = (row > col)[None]
    incl = (row >= col)[None]
    tri = jnp.broadcast_to(jnp.where(incl, 1.0, 0.0).astype(bf16), (H, C, C))

    def by_head(ref, i):
        x = ref[i]
        if seq_rows < C:
            x = jnp.concatenate([x, jnp.zeros((C - seq_rows, x.shape[1]), f32)], axis=0)
        return jnp.stack([x[:, h * N:(h + 1) * N] for h in range(H)], axis=0)

    def one_seq(i, carry):
        r, lw, k, v, a, b = (by_head(ref, i) for ref in (r_ref, lw_ref, k_ref, v_ref, a_ref, b_ref))
        s0 = s_scr[i]
        hi = lw.astype(bf16)
        r1 = lw - hi.astype(f32)
        mid = r1.astype(bf16)
        lo = (r1 - mid.astype(f32)).astype(bf16)
        lc = _bmm(tri, hi) + _bmm(tri, mid) + _bmm(tri, lo)
        e_pos = jnp.exp(lc)
        e_neg = jnp.exp(-lc)
        a_t = (a * jnp.exp(lc - lw)).astype(bf16)
        r_t = (r * e_pos).astype(bf16)
        b_t = (b * e_neg).astype(bf16)
        k_t = (k * e_neg).astype(bf16)
        v_b = v.astype(bf16)
        s0_b = s0.astype(bf16)

        a_ab = jnp.where(strict, _bmm_nt(a_t, b_t), 0.0)
        a_ak = jnp.where(strict, _bmm_nt(a_t, k_t), 0.0)
        a_rb = jnp.where(incl, _bmm_nt(r_t, b_t), 0.0)
        a_rk = jnp.where(incl, _bmm_nt(r_t, k_t), 0.0)

        x = _bmm_nt(a_t, s0_b) + _bmm(a_ak.astype(bf16), v_b)
        p = a_ab
        n = 1
        while n < C:
            p_b = p.astype(bf16)
            x = x + _bmm(p_b, x.astype(bf16))
            n *= 2
            if n < C:
                p = _bmm(p_b, p_b)
        u_b = x.astype(bf16)
        o = _bmm_nt(r_t, s0_b) + _bmm(a_rb.astype(bf16), u_b) + _bmm(a_rk.astype(bf16), v_b)
        s1 = s0 + _bmm_tn(u_b, b_t) + _bmm_tn(v_b, k_t)
        s_scr[i] = s1 * e_pos[:, C - 1:C, :]

        mean = jnp.mean(o, -1, keepdims=True)
        oc = o - mean
        var = jnp.mean(oc * oc, -1, keepdims=True)
        on = oc * lax.rsqrt(var + GN_EPS)
        on = jnp.concatenate([on[h] for h in range(H)], axis=1)[:seq_rows]
        o_ref[i] = (on * gng_ref[...] + gnb_ref[...] + bonus_ref[i]) * g_ref[i]
        return carry

    lax.fori_loop(0, n_seq, one_seq, 0)

    @pl.when(c == pl.num_programs(1) - 1)
    def _():
        sT_ref[...] = s_scr[...]


def rwkv_recurrence(s0, seqs, gn_g, gn_b, *, n_batch, n_chunks, seq_rows, n_seq, chunk):
    h, n = RWKV_HEADS, RWKV_HEAD_DIM
    w = RWKV_WIDTH
    seq_spec = pl.BlockSpec((n_seq, seq_rows, w), lambda i, c: (i * n_chunks + c, 0, 0))
    st_spec = pl.BlockSpec((n_seq, h, n, n), lambda i, c: (i, 0, 0, 0))
    vec_spec = pl.BlockSpec((1, w), lambda i, c: (0, 0))
    o, s_t = pl.pallas_call(
        functools.partial(_rwkv_chunk_kernel, n_seq=n_seq, seq_rows=seq_rows, chunk=chunk),
        out_shape=(jax.ShapeDtypeStruct((n_batch * n_chunks, seq_rows, w), jnp.float32),
                   jax.ShapeDtypeStruct((n_batch, h, n, n), jnp.float32)),
        grid=(n_batch // n_seq, n_chunks),
        in_specs=[seq_spec] * 8 + [vec_spec, vec_spec, st_spec],
        out_specs=(seq_spec, st_spec),
        scratch_shapes=[pltpu.VMEM((n_seq, h, n, n), jnp.float32)],
        compiler_params=pltpu.CompilerParams(
            dimension_semantics=("arbitrary", "arbitrary"), vmem_limit_bytes=VMEM_LIMIT_BYTES),
        name="rwkv_chunk",
    )(*seqs, gn_g.reshape(1, w), gn_b.reshape(1, w), s0)
    return o, s_t


def _moe_scatter_kernel(slot_ref, h_ref, init_hbm, out_hbm, stage, sems):
    del init_hbm
    i = pl.program_id(0)
    nt = pl.num_programs(0)
    tt = h_ref.shape[0]
    slot = i % 2

    def wait_buffer(s):
        for _ in range(TOP_K):
            pltpu.make_async_copy(stage.at[s], out_hbm.at[pl.ds(0, tt)], sems.at[s]).wait()

    @pl.when(i >= 2)
    def _():
        wait_buffer(slot)

    stage[slot] = h_ref[...]
    for j in range(tt):
        for k in range(TOP_K):
            dst = slot_ref[(i * tt + j) * TOP_K + k]
            pltpu.make_async_copy(stage.at[slot, pl.ds(j, 1)], out_hbm.at[pl.ds(dst, 1)], sems.at[slot]).start()

    @pl.when(i == nt - 1)
    def _():
        wait_buffer(slot)

        @pl.when(nt >= 2)
        def _():
            wait_buffer(1 - slot)


def _moe_scatter_rows(h, pair_slot, p_pad):
    n, d = h.shape
    tt = MOE_ROUTE_TILE
    return pl.pallas_call(
        _moe_scatter_kernel,
        out_shape=jax.ShapeDtypeStruct((p_pad, d), jnp.float32),
        grid_spec=pltpu.PrefetchScalarGridSpec(
            num_scalar_prefetch=1,
            grid=(n // tt,),
            in_specs=[pl.BlockSpec((tt, d), lambda i, ps: (i, 0)),
                      pl.BlockSpec(memory_space=pl.ANY)],
            out_specs=pl.BlockSpec(memory_space=pl.ANY),
            scratch_shapes=[pltpu.VMEM((2, tt, d), jnp.float32), pltpu.SemaphoreType.DMA((2,))],
        ),
        input_output_aliases={2: 0},
        compiler_params=pltpu.CompilerParams(
            dimension_semantics=("arbitrary",), vmem_limit_bytes=VMEM_LIMIT_BYTES),
        name="moe_scatter",
    )(pair_slot.reshape(-1), h, jnp.zeros((p_pad, d), jnp.float32))


def _moe_up_kernel(tile_expert_ref, tile_valid_ref, x_ref, wg_ref, wu_ref, bg_ref, bu_ref,
                   h_ref, wg_bf, wu_bf):
    i = pl.program_id(1)
    e = tile_expert_ref[i]
    prev = tile_expert_ref[jnp.maximum(i - 1, 0)]

    @pl.when((i == 0) | (e != prev))
    def _():
        wg_bf[...] = wg_ref[0].astype(jnp.bfloat16)
        wu_bf[...] = wu_ref[0].astype(jnp.bfloat16)

    @pl.when(tile_valid_ref[i] != 0)
    def _():
        x = x_ref[...].astype(jnp.bfloat16)
        hg = jnp.dot(x, wg_bf[...], preferred_element_type=jnp.float32) + bg_ref[0]
        hu = jnp.dot(x, wu_bf[...], preferred_element_type=jnp.float32) + bu_ref[0]
        glu = jnp.minimum(hg, SWIGLU_LIMIT)
        up = jnp.clip(hu, -SWIGLU_LIMIT, SWIGLU_LIMIT)
        act = (up + 1.0) * glu * jax.nn.sigmoid(SWIGLU_ALPHA * glu)
        h_ref[...] = act.astype(h_ref.dtype)

    @pl.when(tile_valid_ref[i] == 0)
    def _():
        h_ref[...] = jnp.zeros_like(h_ref)


def _moe_down_kernel(tile_expert_ref, tile_valid_ref, h_ref, wd_ref, bd_ref, y_ref, wd_bf):
    i = pl.program_id(1)
    e = tile_expert_ref[i]
    prev = tile_expert_ref[jnp.maximum(i - 1, 0)]

    @pl.when((i == 0) | (e != prev))
    def _():
        wd_bf[...] = wd_ref[0].astype(jnp.bfloat16)

    @pl.when(tile_valid_ref[i] != 0)
    def _():
        y_ref[...] = jnp.dot(h_ref[...], wd_bf[...], preferred_element_type=jnp.float32) + bd_ref[0]

    @pl.when(tile_valid_ref[i] == 0)
    def _():
        y_ref[...] = jnp.zeros_like(y_ref)


def _moe_grouped(x_sorted, tile_expert, tile_valid, w_gate_up, b_gate_up, w_down, b_down):
    p_pad, d = x_sorted.shape
    n_exp, _, two_f = w_gate_up.shape
    f = two_f // 2
    tm, fc, dc = MOE_TILE_M, MOE_F_CHUNK, MOE_D_CHUNK
    nt = p_pad // tm
    nfc = f // fc
    ndc = d // dc
    bgu = b_gate_up.reshape(n_exp, 1, two_f)
    bdn = b_down.reshape(n_exp, 1, d)

    h = pl.pallas_call(
        _moe_up_kernel,
        out_shape=jax.ShapeDtypeStruct((p_pad, f), jnp.bfloat16),
        grid_spec=pltpu.PrefetchScalarGridSpec(
            num_scalar_prefetch=2,
            grid=(nfc, nt),
            in_specs=[
                pl.BlockSpec((tm, d), lambda c, i, te, tv: (i, 0)),
                pl.BlockSpec((1, d, fc), lambda c, i, te, tv: (te[i], 0, c)),
                pl.BlockSpec((1, d, fc), lambda c, i, te, tv: (te[i], 0, nfc + c)),
                pl.BlockSpec((1, 1, fc), lambda c, i, te, tv: (te[i], 0, c)),
                pl.BlockSpec((1, 1, fc), lambda c, i, te, tv: (te[i], 0, nfc + c)),
            ],
            out_specs=pl.BlockSpec((tm, fc), lambda c, i, te, tv: (i, c)),
            scratch_shapes=[pltpu.VMEM((d, fc), jnp.bfloat16), pltpu.VMEM((d, fc), jnp.bfloat16)],
        ),
        compiler_params=pltpu.CompilerParams(
            dimension_semantics=("arbitrary", "arbitrary"),
            vmem_limit_bytes=VMEM_LIMIT_BYTES),
        name="moe_up",
    )(tile_expert, tile_valid, x_sorted, w_gate_up, w_gate_up, bgu, bgu)

    y = pl.pallas_call(
        _moe_down_kernel,
        out_shape=jax.ShapeDtypeStruct((p_pad, d), jnp.float32),
        grid_spec=pltpu.PrefetchScalarGridSpec(
            num_scalar_prefetch=2,
            grid=(ndc, nt),
            in_specs=[
                pl.BlockSpec((tm, f), lambda c, i, te, tv: (i, 0)),
                pl.BlockSpec((1, f, dc), lambda c, i, te, tv: (te[i], 0, c)),
                pl.BlockSpec((1, 1, dc), lambda c, i, te, tv: (te[i], 0, c)),
            ],
            out_specs=pl.BlockSpec((tm, dc), lambda c, i, te, tv: (i, c)),
            scratch_shapes=[pltpu.VMEM((f, dc), jnp.bfloat16)],
        ),
        compiler_params=pltpu.CompilerParams(
            dimension_semantics=("arbitrary", "arbitrary"),
            vmem_limit_bytes=VMEM_LIMIT_BYTES),
        name="moe_down",
    )(tile_expert, tile_valid, h, w_down, bdn)
    return y


def _moe_combine_kernel(slot_ref, h_ref, w_ref, g_ref, b_ref, y_hbm, out_ref, buf, sems):
    i = pl.program_id(0)
    nt = pl.num_programs(0)
    tt = h_ref.shape[0]

    def start_tile(t, s):
        for j in range(tt):
            for k in range(TOP_K):
                src = slot_ref[(t * tt + j) * TOP_K + k]
                pltpu.make_async_copy(y_hbm.at[pl.ds(src, 1)], buf.at[s, pl.ds(k * tt + j, 1)], sems.at[s]).start()

    @pl.when(i == 0)
    def _():
        start_tile(0, 0)

    @pl.when(i + 1 < nt)
    def _():
        start_tile(i + 1, (i + 1) % 2)

    slot = i % 2
    pltpu.make_async_copy(y_hbm.at[pl.ds(0, TOP_K * tt)], buf.at[slot], sems.at[slot]).wait()
    w = w_ref[...]
    ffn = buf[slot, 0:tt] * w[:, 0:1]
    for k in range(1, TOP_K):
        ffn = ffn + buf[slot, k * tt:(k + 1) * tt] * w[:, k:k + 1]
    x = DEEPNORM_ALPHA * h_ref[...] + ffn
    mu = jnp.mean(x, -1, keepdims=True)
    xc = x - mu
    var = jnp.mean(xc * xc, -1, keepdims=True)
    out_ref[...] = xc * lax.rsqrt(var + LN_EPS) * g_ref[...] + b_ref[...]


def _moe_combine_ln(h, y, pair_slot, top_w, ln_g, ln_b):
    n, d = h.shape
    tt = MOE_ROUTE_TILE
    return pl.pallas_call(
        _moe_combine_kernel,
        out_shape=jax.ShapeDtypeStruct((n, d), jnp.float32),
        grid_spec=pltpu.PrefetchScalarGridSpec(
            num_scalar_prefetch=1,
            grid=(n // tt,),
            in_specs=[pl.BlockSpec((tt, d), lambda i, ps: (i, 0)),
                      pl.BlockSpec((tt, TOP_K), lambda i, ps: (i, 0)),
                      pl.BlockSpec((1, d), lambda i, ps: (0, 0)),
                      pl.BlockSpec((1, d), lambda i, ps: (0, 0)),
                      pl.BlockSpec(memory_space=pl.ANY)],
            out_specs=pl.BlockSpec((tt, d), lambda i, ps: (i, 0)),
            scratch_shapes=[pltpu.VMEM((2, TOP_K * tt, d), jnp.float32), pltpu.SemaphoreType.DMA((2,))],
        ),
        compiler_params=pltpu.CompilerParams(
            dimension_semantics=("arbitrary",), vmem_limit_bytes=VMEM_LIMIT_BYTES),
        name="moe_combine",
    )(pair_slot.reshape(-1), h, top_w, ln_g.reshape(1, d), ln_b.reshape(1, d), y)


def moe_ffn_ln(x, top_w, top_idx, w_gate_up, b_gate_up, w_down, b_down, ln_g, ln_b):
    n, d = x.shape
    tm = MOE_TILE_M
    nt = n * TOP_K // tm + N_EXPERTS
    onehot = (top_idx[:, :, None] == jnp.arange(N_EXPERTS, dtype=top_idx.dtype)).astype(jnp.int32)
    routed = jnp.sum(onehot, axis=1)
    before = jnp.cumsum(routed, axis=0) - routed
    counts = before[-1] + routed[-1]
    tiles_per = (counts + tm - 1) // tm
    tile_end = jnp.cumsum(tiles_per)
    first_row = (tile_end - tiles_per) * tm
    pair_slot = jnp.sum(onehot * (before + first_row)[:, None, :], axis=2)
    tile_ids = jnp.arange(nt, dtype=jnp.int32)
    tile_expert = jnp.minimum(jnp.sum((tile_ids[:, None] >= tile_end[None, :]).astype(jnp.int32), axis=1),
                              N_EXPERTS - 1).astype(jnp.int32)
    tile_valid = (tile_ids < tile_end[-1]).astype(jnp.int32)

    x_sorted = _moe_scatter_rows(x, pair_slot, nt * tm)
    y = _moe_grouped(x_sorted, tile_expert, tile_valid, w_gate_up, b_gate_up, w_down, b_down)
    return _moe_combine_ln(x, y, pair_slot, top_w, ln_g, ln_b)


def kernel(x_prompt, x_sample, cache_win0_kv, cache_win1_kv, cache_win2_kv, cache_mem_kv,
           state_wkv, state_shift, mem_prompt,
           w_in, w_gate, b_gate, w_branch, w_out, w_mem_kv,
           mu_rkv, mu_wag, w0, w1, w2, a0, a1, a2, g1, g2, k_k, k_a, r_k, gn_g, gn_b,
           ln1_g, ln1_b, w_router, b_router, w_gate_up, b_gate_up, w_down, b_down, ln2_g, ln2_b):
    bf16 = jnp.bfloat16
    xp, xs = x_prompt, x_sample
    bp, tp, d = xp.shape
    bs, ts, _ = xs.shape
    n_prompt = bp * tp
    l = 0
    lp = {"mu_rkv": mu_rkv[l], "mu_wag": mu_wag[l],
          "w0": w0[l], "w1": w1[l], "w2": w2[l], "a0": a0[l], "a1": a1[l], "a2": a2[l],
          "g1": g1[l], "g2": g2[l], "k_k": k_k[l], "k_a": k_a[l], "r_k": r_k[l]}

    u_all = jnp.concatenate([xp.reshape(-1, d), xs.reshape(-1, d)], axis=0)
    u_bf = u_all.astype(bf16)
    w_in_bf = w_in[l].astype(bf16)
    proj = matmul_bias(u_bf, w_in_bf, tile_m=PROJ_TILE_M, name="proj_in")
    gate = matmul_bias(u_bf, w_gate[l].astype(bf16), b_gate[l], tile_m=PROJ_TILE_M, sigmoid=True, name="proj_gate")
    mem_rows = mem_prompt.reshape(-1, d)
    mem_kv_p = matmul_bias(mem_rows.astype(bf16), w_mem_kv[l].astype(bf16), tile_m=mem_rows.shape[0], name="proj_mem")
    proj_p = proj[:n_prompt].reshape(bp, tp, PROJ_WIDTH)
    proj_s = proj[n_prompt:].reshape(bs, ts, PROJ_WIDTH)
    rkv0, qm0 = 3 * ATT_WIDTH, 3 * ATT_WIDTH + 3 * RWKV_WIDTH

    heads = lambda t: t.reshape(t.shape[0], t.shape[1], N_ATT_HEADS, HEAD_DIM)
    qa_p, ka_p, va_p = (heads(proj_p[..., i * ATT_WIDTH:(i + 1) * ATT_WIDTH]) for i in range(3))
    ka_s, va_s = (heads(proj_s[..., i * ATT_WIDTH:(i + 1) * ATT_WIDTH]) for i in (1, 2))
    outs, lses, win_p, win_s = [], [], [], []
    for g, (win, dil) in enumerate(DIL_GROUPS):
        hs = slice(g * HEADS_PER_GROUP, (g + 1) * HEADS_PER_GROUP)
        o, lse = dilated_attention_prompt(qa_p[:, :, hs], ka_p[:, :, hs], va_p[:, :, hs], g, dil)
        outs.append(o.reshape(n_prompt, ATT_OUT))
        lses.append(lse.reshape(n_prompt, ATT_OUT))
        keep = min(win, tp)
        win_p.append(jnp.stack([ka_p[:, tp - keep:, hs], va_p[:, tp - keep:, hs]], axis=2))
        win_s.append(jnp.stack([ka_s[:, :, hs], va_s[:, :, hs]], axis=2))
    o_att_p = att_combine(outs, lses)
    o_att_s = dilated_attention_step(proj_s, cache_win0_kv[l], cache_win1_kv[l], cache_win2_kv[l])

    tq = MEM_ATTN_TILE_Q
    o_mem_p = memory_attention(proj.reshape(-1, tq, PROJ_WIDTH), qm0 // MEM_WIDTH,
                               mem_kv_p.reshape(bp, MEM_TOKENS, 2 * MEM_WIDTH), tp // tq)
    o_mem_s = memory_attention(proj_s, qm0 // MEM_WIDTH, cache_mem_kv[l], 1)

    u_last_s = state_shift[l].astype(xs.dtype)
    p_last_s = matmul_bias(u_last_s.astype(bf16), w_in_bf[:, rkv0:qm0], tile_m=bs, name="proj_shift")
    shift = lambda first, rest: jnp.concatenate([first[:, None], rest[:, :-1]], axis=1).reshape(-1, rest.shape[-1])
    u_prev = jnp.concatenate([shift(jnp.zeros((bp, d), xp.dtype), xp), shift(u_last_s, xs)], axis=0)
    p_prev = jnp.concatenate([shift(jnp.zeros((bp, qm0 - rkv0), jnp.float32), proj_p[..., rkv0:qm0]),
                              shift(p_last_s, proj_s[..., rkv0:qm0])], axis=0)
    rwkv_in = rwkv_pre(u_all, u_prev, proj, p_prev, lp)
    s0 = jnp.zeros((bp, RWKV_HEADS, RWKV_HEAD_DIM, RWKV_HEAD_DIM), jnp.float32)
    o_rwkv_p, wkv_p = rwkv_recurrence(
        s0, [x.reshape(-1, RWKV_CHUNK, RWKV_WIDTH) for x in rwkv_in], gn_g[l], gn_b[l],
        n_batch=bp, n_chunks=tp // RWKV_CHUNK, seq_rows=RWKV_CHUNK, n_seq=1, chunk=RWKV_CHUNK)
    o_rwkv_s, wkv_s = rwkv_recurrence(
        state_wkv[l].astype(jnp.float32), [x[n_prompt:].reshape(bs, ts, RWKV_WIDTH) for x in rwkv_in], gn_g[l], gn_b[l],
        n_batch=bs, n_chunks=1, seq_rows=ts, n_seq=RWKV_SEQ_PER_STEP, chunk=-(-ts // SUBLANES) * SUBLANES)

    cat = lambda a, b, w: jnp.concatenate([a.reshape(-1, w), b.reshape(-1, w)], axis=0)
    h, top_w, top_idx = merge_ln_router(
        cat(o_att_p, o_att_s, ATT_OUT), cat(o_rwkv_p, o_rwkv_s, RWKV_WIDTH), cat(o_mem_p, o_mem_s, MEM_WIDTH),
        gate, u_all, w_branch[l].astype(bf16), w_out[l].astype(bf16), ln1_g[l], ln1_b[l], w_router[l], b_router[l])
    y_all = moe_ffn_ln(h, top_w, top_idx, w_gate_up[l], b_gate_up[l], w_down[l], b_down[l], ln2_g[l], ln2_b[l])
    yp = y_all[:n_prompt].reshape(bp, tp, d)
    ys = y_all[n_prompt:].reshape(bs, ts, d)
    mem_kv_out = mem_kv_p.reshape(bp, MEM_TOKENS, 2, MEM_HEADS, MEM_HEAD_DIM)
    return (yp, ys,
            win_p[0][None], win_p[1][None], win_p[2][None], wkv_p.astype(xp.dtype)[None],
            xp[:, -1][None], mem_kv_out[None],
            win_s[0].astype(cache_win0_kv.dtype)[None], win_s[1].astype(cache_win0_kv.dtype)[None],
            win_s[2].astype(cache_win0_kv.dtype)[None], wkv_s.astype(state_wkv.dtype)[None],
            xs[:, -1].astype(state_shift.dtype)[None])
```

```python
import functools

import jax
import jax.numpy as jnp
from jax import lax
from jax.experimental import pallas as pl
from jax.experimental.pallas import tpu as pltpu

D_MODEL = 2048
DEPTH = 1
HEAD_DIM = 64
DIL_GROUPS = ((128, 1), (512, 4), (2048, 16))
HEADS_PER_GROUP = 4
N_ATT_HEADS = HEADS_PER_GROUP * len(DIL_GROUPS)
ATT_WIDTH = N_ATT_HEADS * HEAD_DIM
ATT_OUT = HEADS_PER_GROUP * HEAD_DIM
BAND_BLOCK = 128
N_BACK = 128
GROUP_WIDTH = HEADS_PER_GROUP * HEAD_DIM
RWKV_HEADS = 12
RWKV_HEAD_DIM = 64
RWKV_WIDTH = RWKV_HEADS * RWKV_HEAD_DIM
GN_EPS = 64e-5
MEM_TOKENS = 256
MEM_HEADS = 4
MEM_HEAD_DIM = 128
MEM_WIDTH = MEM_HEADS * MEM_HEAD_DIM
PROJ_WIDTH = 3 * ATT_WIDTH + 3 * RWKV_WIDTH + MEM_WIDTH
N_BRANCHES = 3
N_EXPERTS = 32
TOP_K = 4
D_EXPERT = 2048
SWIGLU_LIMIT = 7.0
SWIGLU_ALPHA = 1.702
LN_EPS = 1e-5
DEEPNORM_ALPHA = (2.0 * DEPTH) ** 0.25
NEG_INF = -1e30

VMEM_LIMIT_BYTES = 56 * 1024 * 1024
SUBLANES = 8
LANES = 128
MOE_TILE_M = 256
MOE_F_CHUNK = 1024
MOE_D_CHUNK = 1024
MOE_ROUTE_TILE = 64
RWKV_CHUNK = 64
RWKV_SEQ_PER_STEP = 8
RWKV_PRE_TILE_M = 256
PROJ_TILE_M = 1088
PROJ_PROMPT_TILE_M = 1024
MEM_ATTN_STEP_SEQS = 8
MATMUL_TILE_N = 512
MERGE_TILE_M = 256
ATT_COMBINE_TILE = 512
MEM_ATTN_TILE_Q = 512
ATTN_STEP_SEQS = 2


def _matmul_kernel(x_ref, w_ref, b_ref, o_ref, *, sigmoid):
    y = jnp.dot(x_ref[...], w_ref[...], preferred_element_type=jnp.float32) + b_ref[...]
    o_ref[...] = jax.nn.sigmoid(y) if sigmoid else y


def matmul_bias(x, w, bias=None, *, tile_m, sigmoid=False, name="matmul"):
    m, k = x.shape
    n = w.shape[1]
    tn = MATMUL_TILE_N if n % MATMUL_TILE_N == 0 else n
    if bias is None:
        bias = jnp.zeros((n,), jnp.float32)
    return pl.pallas_call(
        functools.partial(_matmul_kernel, sigmoid=sigmoid),
        out_shape=jax.ShapeDtypeStruct((m, n), jnp.float32),
        grid=(n // tn, m // tile_m),
        in_specs=[pl.BlockSpec((tile_m, k), lambda j, i: (i, 0)),
                  pl.BlockSpec((k, tn), lambda j, i: (0, j)),
                  pl.BlockSpec((1, tn), lambda j, i: (0, j))],
        out_specs=pl.BlockSpec((tile_m, tn), lambda j, i: (i, j)),
        compiler_params=pltpu.CompilerParams(
            dimension_semantics=("arbitrary", "arbitrary"), vmem_limit_bytes=VMEM_LIMIT_BYTES),
        name=name,
    )(x, w, bias.reshape(1, n))


def _merge_kernel(att_ref, rwkv_ref, mem_ref, gate_ref, x_ref, wb_ref, wo_ref, g_ref, b_ref,
                  wr_hi_ref, wr_lo_ref, br_ref, h_ref, topw_ref, topi_ref):
    bf16, f32 = jnp.bfloat16, jnp.float32
    d = x_ref.shape[1]
    dot = functools.partial(jnp.dot, preferred_element_type=f32)
    y = dot(att_ref[...].astype(bf16), wb_ref[0:ATT_OUT]) * gate_ref[:, 0:d]
    y = y + dot(rwkv_ref[...].astype(bf16), wb_ref[ATT_OUT:ATT_OUT + RWKV_WIDTH]) * gate_ref[:, d:2 * d]
    y = y + dot(mem_ref[...].astype(bf16), wb_ref[ATT_OUT + RWKV_WIDTH:]) * gate_ref[:, 2 * d:]
    z = DEEPNORM_ALPHA * x_ref[...] + dot(y.astype(bf16), wo_ref[...])
    mu = jnp.mean(z, -1, keepdims=True)
    zc = z - mu
    var = jnp.mean(zc * zc, -1, keepdims=True)
    h = zc * lax.rsqrt(var + LN_EPS) * g_ref[...] + b_ref[...]
    h_ref[...] = h

    h_hi = h.astype(bf16)
    h_lo = (h - h_hi.astype(f32)).astype(bf16)
    logits = (dot(h_hi, wr_hi_ref[...]) + dot(h_hi, wr_lo_ref[...]) + dot(h_lo, wr_hi_ref[...])) + br_ref[...]
    lane = lax.broadcasted_iota(jnp.int32, logits.shape, 1)
    work = jnp.where(lane < N_EXPERTS, logits, -jnp.inf)
    tops, ids = [], []
    for _ in range(TOP_K):
        m = jnp.max(work, -1, keepdims=True)
        idx = jnp.min(jnp.where(work == m, lane, LANES), -1, keepdims=True)
        tops.append(m)
        ids.append(idx)
        work = jnp.where(lane == idx, -jnp.inf, work)
    e = [jnp.exp(t - tops[0]) for t in tops]
    den = e[0] + e[1] + e[2] + e[3]
    tw = jnp.zeros(logits.shape, f32)
    ti = jnp.zeros(logits.shape, jnp.int32)
    for k in range(TOP_K):
        tw = jnp.where(lane == k, e[k] / den, tw)
        ti = jnp.where(lane == k, ids[k], ti)
    topw_ref[...] = tw
    topi_ref[...] = ti


def merge_ln_router(o_att, o_rwkv, o_mem, gate, x, w_branch, w_out, ln_g, ln_b, w_router, b_router):
    n, d = x.shape
    tm = MERGE_TILE_M
    wr = jnp.zeros((d, LANES), jnp.float32).at[:, :N_EXPERTS].set(w_router)
    wr_hi = wr.astype(jnp.bfloat16)
    wr_lo = (wr - wr_hi.astype(jnp.float32)).astype(jnp.bfloat16)
    br = jnp.zeros((1, LANES), jnp.float32).at[0, :N_EXPERTS].set(b_router)
    rows = lambda w: pl.BlockSpec((tm, w), lambda i: (i, 0))
    full = lambda a: pl.BlockSpec(a.shape, lambda i: (0, 0))
    h, topw, topi = pl.pallas_call(
        _merge_kernel,
        out_shape=(jax.ShapeDtypeStruct((n, d), jnp.float32),
                   jax.ShapeDtypeStruct((n, LANES), jnp.float32),
                   jax.ShapeDtypeStruct((n, LANES), jnp.int32)),
        grid=(n // tm,),
        in_specs=[rows(ATT_OUT), rows(RWKV_WIDTH), rows(MEM_WIDTH), rows(3 * d), rows(d),
                  full(w_branch), full(w_out), pl.BlockSpec((1, d), lambda i: (0, 0)),
                  pl.BlockSpec((1, d), lambda i: (0, 0)), full(wr_hi), full(wr_lo), full(br)],
        out_specs=(rows(d), rows(LANES), rows(LANES)),
        compiler_params=pltpu.CompilerParams(
            dimension_semantics=("arbitrary",), vmem_limit_bytes=VMEM_LIMIT_BYTES),
        name="merge_ln_router",
    )(o_att, o_rwkv, o_mem, gate, x, w_branch, w_out, ln_g.reshape(1, d), ln_b.reshape(1, d), wr_hi, wr_lo, br)
    return h, topw[:, :TOP_K], topi[:, :TOP_K]


def _alibi_slope(head):
    return 2.0 ** (-8.0 * (head + 1) / N_ATT_HEADS)


def _dot_nt(a, b):
    return lax.dot_general(a, b, (((1,), (1,)), ((), ())), preferred_element_type=jnp.float32)


def _dil_attn_prompt_kernel(q_ref, kc_ref, kp_ref, vc_ref, vp_ref, o_ref, lse_ref, *, group, dil):
    bf16 = jnp.bfloat16
    blk = BAND_BLOCK
    i = pl.program_id(1)
    qi = lax.broadcasted_iota(jnp.int32, (blk, blk), 0)
    kj = lax.broadcasted_iota(jnp.int32, (blk, blk), 1)
    dist_c = qi - kj
    dist_p = dist_c + blk
    valid_c = dist_c >= 0
    valid_p = (dist_p <= blk) & (i > 0)
    scale = HEAD_DIM ** -0.5
    for h in range(HEADS_PER_GROUP):
        slope = _alibi_slope(group * HEADS_PER_GROUP + h) * dil
        q = q_ref[0, h].astype(bf16)
        s_c = _dot_nt(q, kc_ref[0, h].astype(bf16)) * scale - slope * dist_c.astype(jnp.float32)
        s_p = _dot_nt(q, kp_ref[0, h].astype(bf16)) * scale - slope * dist_p.astype(jnp.float32)
        s_c = jnp.where(valid_c, s_c, NEG_INF)
        s_p = jnp.where(valid_p, s_p, NEG_INF)
        m = jnp.maximum(jnp.max(s_c, -1, keepdims=True), jnp.max(s_p, -1, keepdims=True))
        p_c = jnp.exp(s_c - m)
        p_p = jnp.exp(s_p - m)
        den = jnp.sum(p_c, -1, keepdims=True) + jnp.sum(p_p, -1, keepdims=True)
        pv = (jnp.dot(p_c.astype(bf16), vc_ref[0, h].astype(bf16), preferred_element_type=jnp.float32)
              + jnp.dot(p_p.astype(bf16), vp_ref[0, h].astype(bf16), preferred_element_type=jnp.float32))
        o_ref[0, h] = pv / den
        lse_ref[0, h] = jnp.broadcast_to(m + jnp.log(den), (blk, HEAD_DIM))


def dilated_attention_prompt(q, k, v, group, dil):
    b, t, h, e = q.shape
    l = t // dil
    assert l % BAND_BLOCK == 0

    def by_residue(a):
        return a.reshape(b, l, dil, h, e).transpose(0, 2, 3, 1, 4).reshape(b * dil, h, l, e)

    def from_residue(a):
        return a.reshape(b, dil, h, l, e).transpose(0, 3, 1, 2, 4).reshape(b, t, h, e)

    cur = pl.BlockSpec((1, h, BAND_BLOCK, e), lambda n, i: (n, 0, i, 0))
    prev = pl.BlockSpec((1, h, BAND_BLOCK, e), lambda n, i: (n, 0, jnp.maximum(i - 1, 0), 0))
    qr, kr, vr = by_residue(q), by_residue(k), by_residue(v)
    o, lse = pl.pallas_call(
        functools.partial(_dil_attn_prompt_kernel, group=group, dil=dil),
        out_shape=(jax.ShapeDtypeStruct(qr.shape, jnp.float32), jax.ShapeDtypeStruct(qr.shape, jnp.float32)),
        grid=(b * dil, l // BAND_BLOCK),
        in_specs=[cur, cur, prev, cur, prev],
        out_specs=(cur, cur),
        compiler_params=pltpu.CompilerParams(
            dimension_semantics=("arbitrary", "arbitrary"), vmem_limit_bytes=VMEM_LIMIT_BYTES),
        name="dil_attn_prompt_g%d" % group,
    )(qr, kr, kr, vr, vr)
    return from_residue(o), from_residue(lse)


def _att_combine_kernel(o0, o1, o2, l0, l1, l2, out_ref):
    a, b, c = l0[...], l1[...], l2[...]
    m = jnp.maximum(jnp.maximum(a, b), c)
    ea, eb, ec = jnp.exp(a - m), jnp.exp(b - m), jnp.exp(c - m)
    out_ref[...] = (ea * o0[...] + eb * o1[...] + ec * o2[...]) / (ea + eb + ec)


def att_combine(outs, lses):
    n, w = outs[0].shape
    spec = pl.BlockSpec((ATT_COMBINE_TILE, w), lambda i: (i, 0))
    return pl.pallas_call(
        _att_combine_kernel,
        out_shape=jax.ShapeDtypeStruct((n, w), jnp.float32),
        grid=(n // ATT_COMBINE_TILE,),
        in_specs=[spec] * 6,
        out_specs=spec,
        compiler_params=pltpu.CompilerParams(dimension_semantics=("arbitrary",)),
        name="att_combine",
    )(*outs, *lses)


def _mem_attn_kernel(q_ref, kv_ref, o_ref):
    bf16 = jnp.bfloat16
    e = MEM_HEAD_DIM
    scale = e ** -0.5
    for n, h in [(n, h) for n in range(q_ref.shape[0]) for h in range(MEM_HEADS)]:
        q = q_ref[n, :, h * e:(h + 1) * e].astype(bf16)
        if len(kv_ref.shape) == 3:
            k = kv_ref[n, :, h * e:(h + 1) * e].astype(bf16)
            v = kv_ref[n, :, (MEM_HEADS + h) * e:(MEM_HEADS + h + 1) * e].astype(bf16)
        else:
            k = kv_ref[n, :, 0, h, :].astype(bf16)
            v = kv_ref[n, :, 1, h, :].astype(bf16)
        s = _dot_nt(q, k) * scale
        m = jnp.max(s, -1, keepdims=True)
        p = jnp.exp(s - m)
        p = p / jnp.sum(p, -1, keepdims=True)
        o_ref[n, :, h * e:(h + 1) * e] = jnp.dot(p.astype(bf16), v, preferred_element_type=jnp.float32)


def memory_attention(q, q_col_block, mem_kv, blocks_per_seq, n_seq=1):
    b = mem_kv.shape[0]
    tq = q.shape[1]
    w = MEM_WIDTH
    assert n_seq == 1 or blocks_per_seq == 1
    kv_zeros = (0,) * (mem_kv.ndim - 1)
    return pl.pallas_call(
        _mem_attn_kernel,
        out_shape=jax.ShapeDtypeStruct((b * blocks_per_seq, tq, w), jnp.float32),
        grid=(b // n_seq, blocks_per_seq),
        in_specs=[pl.BlockSpec((n_seq, tq, w), lambda n, i: (n * blocks_per_seq + i, 0, q_col_block)),
                  pl.BlockSpec((n_seq,) + mem_kv.shape[1:], lambda n, i: (n,) + kv_zeros)],
        out_specs=pl.BlockSpec((n_seq, tq, w), lambda n, i: (n * blocks_per_seq + i, 0, 0)),
        compiler_params=pltpu.CompilerParams(
            dimension_semantics=("arbitrary", "arbitrary"), vmem_limit_bytes=VMEM_LIMIT_BYTES),
        name="mem_attn",
    )(q, mem_kv)


def _dil_attn_step_kernel(qkv_ref, c0_ref, c1_ref, c2_ref, o_ref, *, n_new):
    bf16, f32 = jnp.bfloat16, jnp.float32
    scale = HEAD_DIM ** -0.5
    caches = (c0_ref, c1_ref, c2_ref)
    tq = lax.broadcasted_iota(jnp.int32, (n_new, n_new), 0)
    tk = lax.broadcasted_iota(jnp.int32, (n_new, n_new), 1)
    for b, h in [(b, h) for b in range(qkv_ref.shape[0]) for h in range(HEADS_PER_GROUP)]:
        outs, lses = [], []
        for g, (win, dil) in enumerate(DIL_GROUPS):
            head = g * HEADS_PER_GROUP + h
            slope = _alibi_slope(head)
            lanes = slice(head * HEAD_DIM, (head + 1) * HEAD_DIM)
            q = qkv_ref[b, :, lanes].astype(bf16)
            k_new = qkv_ref[b, :, ATT_WIDTH + head * HEAD_DIM:ATT_WIDTH + (head + 1) * HEAD_DIM].astype(bf16)
            v_new = qkv_ref[b, :, 2 * ATT_WIDTH + head * HEAD_DIM:2 * ATT_WIDTH + (head + 1) * HEAD_DIM].astype(bf16)
            s = jnp.dot(q, caches[g][b, 0, h].astype(bf16), preferred_element_type=f32) * scale
            dist = (win + lax.broadcasted_iota(jnp.int32, (n_new, win), 0)
                    - lax.broadcasted_iota(jnp.int32, (n_new, win), 1))
            valid = ((dist & (dil - 1)) == 0) & (dist <= N_BACK * dil)
            s = jnp.where(valid, s - slope * dist.astype(f32), NEG_INF)
            sn = _dot_nt(q, k_new) * scale
            dn = tq - tk
            sn = jnp.where((dn >= 0) & ((dn & (dil - 1)) == 0), sn - slope * dn.astype(f32), NEG_INF)
            m = jnp.maximum(jnp.max(s, axis=1, keepdims=True), jnp.max(sn, axis=1, keepdims=True))
            p = jnp.exp(s - m)
            pn = jnp.exp(sn - m)
            den = jnp.sum(p, axis=1, keepdims=True) + jnp.sum(pn, axis=1, keepdims=True)
            pv = (_dot_nt(p.astype(bf16), caches[g][b, 1, h].astype(bf16))
                  + jnp.dot(pn.astype(bf16), v_new, preferred_element_type=f32))
            outs.append(pv / den)
            lses.append(m + jnp.log(den))
        mm = jnp.maximum(jnp.maximum(lses[0], lses[1]), lses[2])
        w = [jnp.exp(l - mm) for l in lses]
        o_ref[b, :, h * HEAD_DIM:(h + 1) * HEAD_DIM] = (
            (w[0] * outs[0] + w[1] * outs[1] + w[2] * outs[2]) / (w[0] + w[1] + w[2]))


def dilated_attention_step(p_s, cache0, cache1, cache2):
    b, t, _ = p_s.shape
    feat_major = lambda c: jnp.transpose(c, (0, 2, 3, 4, 1))
    ns = ATTN_STEP_SEQS
    win_spec = lambda w: pl.BlockSpec((ns, 2, HEADS_PER_GROUP, HEAD_DIM, w), lambda n: (n, 0, 0, 0, 0))
    return pl.pallas_call(
        functools.partial(_dil_attn_step_kernel, n_new=t),
        out_shape=jax.ShapeDtypeStruct((b, t, GROUP_WIDTH), jnp.float32),
        grid=(b // ns,),
        in_specs=[pl.BlockSpec((ns, t, 3 * ATT_WIDTH), lambda n: (n, 0, 0))] + [win_spec(w) for w, _ in DIL_GROUPS],
        out_specs=pl.BlockSpec((ns, t, GROUP_WIDTH), lambda n: (n, 0, 0)),
        compiler_params=pltpu.CompilerParams(
            dimension_semantics=("arbitrary",), vmem_limit_bytes=VMEM_LIMIT_BYTES),
        name="dil_attn_step",
    )(p_s, feat_major(cache0), feat_major(cache1), feat_major(cache2))


def _rwkv_pre_kernel(u_ref, up_ref, p_ref, pp_ref, mu_rkv_ref, mu_wag_ref, w0_ref, w1_ref, w2_ref,
                     a0_ref, a1_ref, a2_ref, g1_ref, g2_ref, kk_ref, ka_ref, rk_ref,
                     r_ref, lw_ref, k_ref, v_ref, an_ref, b_ref, g_ref, bonus_ref, *, rolled):
    bf16, f32 = jnp.bfloat16, jnp.float32
    dot = functools.partial(jnp.dot, preferred_element_type=f32)
    w = RWKV_WIDTH
    u = u_ref[...]
    p = p_ref[...]
    if rolled:
        first = lax.broadcasted_iota(jnp.int32, (u.shape[0], 1), 0) == 0
        u_prev = jnp.where(first, up_ref[0], pltpu.roll(u, 1, 0))
        p_prev = jnp.where(first, pp_ref[0], pltpu.roll(p, 1, 0))
    else:
        u_prev, p_prev = up_ref[...], pp_ref[...]
    du = u_prev - u
    rkv = p + (p_prev - p) * mu_rkv_ref[...]
    r, k, v = rkv[:, :w], rkv[:, w:2 * w], rkv[:, 2 * w:]
    xw = (u + du * mu_wag_ref[0:1]).astype(bf16)
    xa = (u + du * mu_wag_ref[1:2]).astype(bf16)
    xg = (u + du * mu_wag_ref[2:3]).astype(bf16)
    y = -(w0_ref[...] + dot(jnp.tanh(dot(xw, w1_ref[...])).astype(bf16), w2_ref[...]))
    softplus = jnp.maximum(y, 0.0) + jnp.log1p(jnp.exp(-jnp.abs(y)))
    lw_ref[...] = -jnp.exp(-softplus - 0.5)
    a = jax.nn.sigmoid(a0_ref[...] + dot(dot(xa, a1_ref[...]).astype(bf16), a2_ref[...]))
    g_ref[...] = dot(jax.nn.sigmoid(dot(xg, g1_ref[...])).astype(bf16), g2_ref[...])

    rr = lax.broadcasted_iota(jnp.int32, (LANES, LANES), 0) // RWKV_HEAD_DIM
    cc = lax.broadcasted_iota(jnp.int32, (LANES, LANES), 1) // RWKV_HEAD_DIM
    seg_ones = jnp.where(rr == cc, 1.0, 0.0).astype(bf16)

    def head_sum(x):
        hi = x.astype(bf16)
        lo = (x - hi.astype(f32)).astype(bf16)
        cols = [dot(hi[:, j:j + LANES], seg_ones) + dot(lo[:, j:j + LANES], seg_ones)
                for j in range(0, w, LANES)]
        return jnp.concatenate(cols, axis=1)

    kk = k * kk_ref[...]
    kk = kk / jnp.maximum(jnp.sqrt(head_sum(kk * kk)), 1e-12)
    k2 = k * (1.0 + (a - 1.0) * ka_ref[...])
    r_ref[...] = r
    k_ref[...] = k2
    v_ref[...] = v
    an_ref[...] = -kk
    b_ref[...] = kk * a
    bonus_ref[...] = head_sum(r * k2 * rk_ref[...]) * v


def rwkv_pre(u, u_prev, proj, p_prev, lp, *, rolled):
    n, d = u.shape
    tm = RWKV_PRE_TILE_M
    w3 = 3 * RWKV_WIDTH
    if rolled:
        prev_specs = [pl.BlockSpec((1, 1, d), lambda i: (i, 0, 0)), pl.BlockSpec((1, 1, w3), lambda i: (i, 0, 0))]
    else:
        prev_specs = [pl.BlockSpec((tm, d), lambda i: (i, 0)), pl.BlockSpec((tm, w3), lambda i: (i, 0))]
    bf16 = jnp.bfloat16
    row = lambda x: x.reshape(1, -1)
    consts = [row(lp["mu_rkv"]), lp["mu_wag"], row(lp["w0"]), lp["w1"].astype(bf16), lp["w2"].astype(bf16),
              row(lp["a0"]), lp["a1"].astype(bf16), lp["a2"].astype(bf16), lp["g1"].astype(bf16),
              lp["g2"].astype(bf16), row(lp["k_k"]), row(lp["k_a"]), row(lp["r_k"])]
    full = lambda a: pl.BlockSpec(a.shape, lambda i: (0, 0))
    out_spec = pl.BlockSpec((tm, RWKV_WIDTH), lambda i: (i, 0))
    return pl.pallas_call(
        functools.partial(_rwkv_pre_kernel, rolled=rolled),
        out_shape=tuple(jax.ShapeDtypeStruct((n, RWKV_WIDTH), jnp.float32) for _ in range(8)),
        grid=(n // tm,),
        in_specs=[pl.BlockSpec((tm, d), lambda i: (i, 0)), prev_specs[0],
                  pl.BlockSpec((tm, w3), lambda i: (i, 1)), prev_specs[1]]
                 + [full(c) for c in consts],
        out_specs=tuple(out_spec for _ in range(8)),
        compiler_params=pltpu.CompilerParams(
            dimension_semantics=("arbitrary",), vmem_limit_bytes=VMEM_LIMIT_BYTES),
        name="rwkv_pre",
    )(u, u_prev, proj, p_prev, *consts)


def _bmm(a, b):
    return lax.dot_general(a, b, (((2,), (1,)), ((0,), (0,))), preferred_element_type=jnp.float32)


def _bmm_nt(a, b):
    return lax.dot_general(a, b, (((2,), (2,)), ((0,), (0,))), preferred_element_type=jnp.float32)


def _bmm_tn(a, b):
    return lax.dot_general(a, b, (((1,), (1,)), ((0,), (0,))), preferred_element_type=jnp.float32)


def _rwkv_chunk_kernel(r_ref, lw_ref, k_ref, v_ref, a_ref, b_ref, g_ref, bonus_ref, gng_ref, gnb_ref, s0_ref,
                       o_ref, sT_ref, s_scr, *, n_seq, seq_rows, chunk):
    bf16, f32 = jnp.bfloat16, jnp.float32
    c = pl.program_id(1)
    C = chunk
    H, N = RWKV_HEADS, RWKV_HEAD_DIM

    @pl.when(c == 0)
    def _():
        s_scr[...] = s0_ref[...]

    row = lax.broadcasted_iota(jnp.int32, (C, C), 0)
    col = lax.broadcasted_iota(jnp.int32, (C, C), 1)
    strict = (row > col)[None]
    incl = (row >= col)[None]
    tri = jnp.broadcast_to(jnp.where(incl, 1.0, 0.0).astype(bf16), (H, C, C))

    def by_head(ref, i):
        x = ref[i]
        if seq_rows < C:
            x = jnp.concatenate([x, jnp.zeros((C - seq_rows, x.shape[1]), f32)], axis=0)
        return jnp.stack([x[:, h * N:(h + 1) * N] for h in range(H)], axis=0)

    def one_seq(i, carry):
        r, lw, k, v, a, b = (by_head(ref, i) for ref in (r_ref, lw_ref, k_ref, v_ref, a_ref, b_ref))
        s0 = s_scr[i]
        hi = lw.astype(bf16)
        r1 = lw - hi.astype(f32)
        mid = r1.astype(bf16)
        lo = (r1 - mid.astype(f32)).astype(bf16)
        lc = _bmm(tri, hi) + _bmm(tri, mid) + _bmm(tri, lo)
        e_pos = jnp.exp(lc)
        e_neg = jnp.exp(-lc)
        a_t = (a * jnp.exp(lc - lw)).astype(bf16)
        r_t = (r * e_pos).astype(bf16)
        b_t = (b * e_neg).astype(bf16)
        k_t = (k * e_neg).astype(bf16)
        v_b = v.astype(bf16)
        s0_b = s0.astype(bf16)

        a_ab = jnp.where(strict, _bmm_nt(a_t, b_t), 0.0)
        a_ak = jnp.where(strict, _bmm_nt(a_t, k_t), 0.0)
        a_rb = jnp.where(incl, _bmm_nt(r_t, b_t), 0.0)
        a_rk = jnp.where(incl, _bmm_nt(r_t, k_t), 0.0)

        x = _bmm_nt(a_t, s0_b) + _bmm(a_ak.astype(bf16), v_b)
        p = a_ab
        n = 1
        while n < C:
            p_b = p.astype(bf16)
            x = x + _bmm(p_b, x.astype(bf16))
            n *= 2
            if n < C:
                p = _bmm(p_b, p_b)
        u_b = x.astype(bf16)
        o = _bmm_nt(r_t, s0_b) + _bmm(a_rb.astype(bf16), u_b) + _bmm(a_rk.astype(bf16), v_b)
        s1 = s0 + _bmm_tn(u_b, b_t) + _bmm_tn(v_b, k_t)
        s_scr[i] = s1 * e_pos[:, C - 1:C, :]

        mean = jnp.mean(o, -1, keepdims=True)
        oc = o - mean
        var = jnp.mean(oc * oc, -1, keepdims=True)
        on = oc * lax.rsqrt(var + GN_EPS)
        on = jnp.concatenate([on[h] for h in range(H)], axis=1)[:seq_rows]
        o_ref[i] = (on * gng_ref[...] + gnb_ref[...] + bonus_ref[i]) * g_ref[i]
        return carry

    lax.fori_loop(0, n_seq, one_seq, 0)

    @pl.when(c == pl.num_programs(1) - 1)
    def _():
        sT_ref[...] = s_scr[...]


def rwkv_recurrence(s0, seqs, gn_g, gn_b, *, n_batch, n_chunks, seq_rows, n_seq, chunk):
    h, n = RWKV_HEADS, RWKV_HEAD_DIM
    w = RWKV_WIDTH
    seq_spec = pl.BlockSpec((n_seq, seq_rows, w), lambda i, c: (i * n_chunks + c, 0, 0))
    st_spec = pl.BlockSpec((n_seq, h, n, n), lambda i, c: (i, 0, 0, 0))
    vec_spec = pl.BlockSpec((1, w), lambda i, c: (0, 0))
    o, s_t = pl.pallas_call(
        functools.partial(_rwkv_chunk_kernel, n_seq=n_seq, seq_rows=seq_rows, chunk=chunk),
        out_shape=(jax.ShapeDtypeStruct((n_batch * n_chunks, seq_rows, w), jnp.float32),
                   jax.ShapeDtypeStruct((n_batch, h, n, n), jnp.float32)),
        grid=(n_batch // n_seq, n_chunks),
        in_specs=[seq_spec] * 8 + [vec_spec, vec_spec, st_spec],
        out_specs=(seq_spec, st_spec),
        scratch_shapes=[pltpu.VMEM((n_seq, h, n, n), jnp.float32)],
        compiler_params=pltpu.CompilerParams(
            dimension_semantics=("arbitrary", "arbitrary"), vmem_limit_bytes=VMEM_LIMIT_BYTES),
        name="rwkv_chunk",
    )(*seqs, gn_g.reshape(1, w), gn_b.reshape(1, w), s0)
    return o, s_t


def _moe_scatter_kernel(slot_ref, h_ref, init_hbm, out_hbm, stage, sems):
    del init_hbm
    i = pl.program_id(0)
    nt = pl.num_programs(0)
    tt = h_ref.shape[0]
    slot = i % 2

    def wait_buffer(s):
        for _ in range(TOP_K):
            pltpu.make_async_copy(stage.at[s], out_hbm.at[pl.ds(0, tt)], sems.at[s]).wait()

    @pl.when(i >= 2)
    def _():
        wait_buffer(slot)

    stage[slot] = h_ref[...]
    for j in range(tt):
        for k in range(TOP_K):
            dst = slot_ref[(i * tt + j) * TOP_K + k]
            pltpu.make_async_copy(stage.at[slot, pl.ds(j, 1)], out_hbm.at[pl.ds(dst, 1)], sems.at[slot]).start()

    @pl.when(i == nt - 1)
    def _():
        wait_buffer(slot)

        @pl.when(nt >= 2)
        def _():
            wait_buffer(1 - slot)


def _moe_scatter_rows(h, pair_slot, p_pad):
    n, d = h.shape
    tt = MOE_ROUTE_TILE
    return pl.pallas_call(
        _moe_scatter_kernel,
        out_shape=jax.ShapeDtypeStruct((p_pad, d), jnp.float32),
        grid_spec=pltpu.PrefetchScalarGridSpec(
            num_scalar_prefetch=1,
            grid=(n // tt,),
            in_specs=[pl.BlockSpec((tt, d), lambda i, ps: (i, 0)),
                      pl.BlockSpec(memory_space=pl.ANY)],
            out_specs=pl.BlockSpec(memory_space=pl.ANY),
            scratch_shapes=[pltpu.VMEM((2, tt, d), jnp.float32), pltpu.SemaphoreType.DMA((2,))],
        ),
        input_output_aliases={2: 0},
        compiler_params=pltpu.CompilerParams(
            dimension_semantics=("arbitrary",), vmem_limit_bytes=VMEM_LIMIT_BYTES),
        name="moe_scatter",
    )(pair_slot.reshape(-1), h, jnp.zeros((p_pad, d), jnp.float32))


def _moe_up_kernel(tile_expert_ref, tile_valid_ref, x_ref, wg_ref, wu_ref, bg_ref, bu_ref,
                   h_ref, wg_bf, wu_bf):
    i = pl.program_id(1)
    e = tile_expert_ref[i]
    prev = tile_expert_ref[jnp.maximum(i - 1, 0)]

    @pl.when((i == 0) | (e != prev))
    def _():
        wg_bf[...] = wg_ref[0].astype(jnp.bfloat16)
        wu_bf[...] = wu_ref[0].astype(jnp.bfloat16)

    @pl.when(tile_valid_ref[i] != 0)
    def _():
        x = x_ref[...].astype(jnp.bfloat16)
        hg = jnp.dot(x, wg_bf[...], preferred_element_type=jnp.float32) + bg_ref[0]
        hu = jnp.dot(x, wu_bf[...], preferred_element_type=jnp.float32) + bu_ref[0]
        glu = jnp.minimum(hg, SWIGLU_LIMIT)
        up = jnp.clip(hu, -SWIGLU_LIMIT, SWIGLU_LIMIT)
        act = (up + 1.0) * glu * jax.nn.sigmoid(SWIGLU_ALPHA * glu)
        h_ref[...] = act.astype(h_ref.dtype)

    @pl.when(tile_valid_ref[i] == 0)
    def _():
        h_ref[...] = jnp.zeros_like(h_ref)


def _moe_down_kernel(tile_expert_ref, tile_valid_ref, h_ref, wd_ref, bd_ref, y_ref, wd_bf):
    i = pl.program_id(1)
    e = tile_expert_ref[i]
    prev = tile_expert_ref[jnp.maximum(i - 1, 0)]

    @pl.when((i == 0) | (e != prev))
    def _():
        wd_bf[...] = wd_ref[0].astype(jnp.bfloat16)

    @pl.when(tile_valid_ref[i] != 0)
    def _():
        y_ref[...] = jnp.dot(h_ref[...], wd_bf[...], preferred_element_type=jnp.float32) + bd_ref[0]

    @pl.when(tile_valid_ref[i] == 0)
    def _():
        y_ref[...] = jnp.zeros_like(y_ref)


def _moe_grouped(x_sorted, tile_expert, tile_valid, w_gate_up, b_gate_up, w_down, b_down):
    p_pad, d = x_sorted.shape
    n_exp, _, two_f = w_gate_up.shape
    f = two_f // 2
    tm, fc, dc = MOE_TILE_M, MOE_F_CHUNK, MOE_D_CHUNK
    nt = p_pad // tm
    nfc = f // fc
    ndc = d // dc
    bgu = b_gate_up.reshape(n_exp, 1, two_f)
    bdn = b_down.reshape(n_exp, 1, d)

    h = pl.pallas_call(
        _moe_up_kernel,
        out_shape=jax.ShapeDtypeStruct((p_pad, f), jnp.bfloat16),
        grid_spec=pltpu.PrefetchScalarGridSpec(
            num_scalar_prefetch=2,
            grid=(nfc, nt),
            in_specs=[
                pl.BlockSpec((tm, d), lambda c, i, te, tv: (i, 0)),
                pl.BlockSpec((1, d, fc), lambda c, i, te, tv: (te[i], 0, c)),
                pl.BlockSpec((1, d, fc), lambda c, i, te, tv: (te[i], 0, nfc + c)),
                pl.BlockSpec((1, 1, fc), lambda c, i, te, tv: (te[i], 0, c)),
                pl.BlockSpec((1, 1, fc), lambda c, i, te, tv: (te[i], 0, nfc + c)),
            ],
            out_specs=pl.BlockSpec((tm, fc), lambda c, i, te, tv: (i, c)),
            scratch_shapes=[pltpu.VMEM((d, fc), jnp.bfloat16), pltpu.VMEM((d, fc), jnp.bfloat16)],
        ),
        compiler_params=pltpu.CompilerParams(
            dimension_semantics=("arbitrary", "arbitrary"),
            vmem_limit_bytes=VMEM_LIMIT_BYTES),
        name="moe_up",
    )(tile_expert, tile_valid, x_sorted, w_gate_up, w_gate_up, bgu, bgu)

    y = pl.pallas_call(
        _moe_down_kernel,
        out_shape=jax.ShapeDtypeStruct((p_pad, d), jnp.float32),
        grid_spec=pltpu.PrefetchScalarGridSpec(
            num_scalar_prefetch=2,
            grid=(ndc, nt),
            in_specs=[
                pl.BlockSpec((tm, f), lambda c, i, te, tv: (i, 0)),
                pl.BlockSpec((1, f, dc), lambda c, i, te, tv: (te[i], 0, c)),
                pl.BlockSpec((1, 1, dc), lambda c, i, te, tv: (te[i], 0, c)),
            ],
            out_specs=pl.BlockSpec((tm, dc), lambda c, i, te, tv: (i, c)),
            scratch_shapes=[pltpu.VMEM((f, dc), jnp.bfloat16)],
        ),
        compiler_params=pltpu.CompilerParams(
            dimension_semantics=("arbitrary", "arbitrary"),
            vmem_limit_bytes=VMEM_LIMIT_BYTES),
        name="moe_down",
    )(tile_expert, tile_valid, h, w_down, bdn)
    return y


def _moe_combine_kernel(slot_ref, h_ref, w_ref, g_ref, b_ref, y_hbm, out_ref, buf, sems):
    i = pl.program_id(0)
    nt = pl.num_programs(0)
    tt = h_ref.shape[0]

    def start_tile(t, s):
        for j in range(tt):
            for k in range(TOP_K):
                src = slot_ref[(t * tt + j) * TOP_K + k]
                pltpu.make_async_copy(y_hbm.at[pl.ds(src, 1)], buf.at[s, pl.ds(k * tt + j, 1)], sems.at[s]).start()

    @pl.when(i == 0)
    def _():
        start_tile(0, 0)

    @pl.when(i + 1 < nt)
    def _():
        start_tile(i + 1, (i + 1) % 2)

    slot = i % 2
    pltpu.make_async_copy(y_hbm.at[pl.ds(0, TOP_K * tt)], buf.at[slot], sems.at[slot]).wait()
    w = w_ref[...]
    ffn = buf[slot, 0:tt] * w[:, 0:1]
    for k in range(1, TOP_K):
        ffn = ffn + buf[slot, k * tt:(k + 1) * tt] * w[:, k:k + 1]
    x = DEEPNORM_ALPHA * h_ref[...] + ffn
    mu = jnp.mean(x, -1, keepdims=True)
    xc = x - mu
    var = jnp.mean(xc * xc, -1, keepdims=True)
    out_ref[...] = xc * lax.rsqrt(var + LN_EPS) * g_ref[...] + b_ref[...]


def _moe_combine_ln(h, y, pair_slot, top_w, ln_g, ln_b):
    n, d = h.shape
    tt = MOE_ROUTE_TILE
    return pl.pallas_call(
        _moe_combine_kernel,
        out_shape=jax.ShapeDtypeStruct((n, d), jnp.float32),
        grid_spec=pltpu.PrefetchScalarGridSpec(
            num_scalar_prefetch=1,
            grid=(n // tt,),
            in_specs=[pl.BlockSpec((tt, d), lambda i, ps: (i, 0)),
                      pl.BlockSpec((tt, TOP_K), lambda i, ps: (i, 0)),
                      pl.BlockSpec((1, d), lambda i, ps: (0, 0)),
                      pl.BlockSpec((1, d), lambda i, ps: (0, 0)),
                      pl.BlockSpec(memory_space=pl.ANY)],
            out_specs=pl.BlockSpec((tt, d), lambda i, ps: (i, 0)),
            scratch_shapes=[pltpu.VMEM((2, TOP_K * tt, d), jnp.float32), pltpu.SemaphoreType.DMA((2,))],
        ),
        compiler_params=pltpu.CompilerParams(
            dimension_semantics=("arbitrary",), vmem_limit_bytes=VMEM_LIMIT_BYTES),
        name="moe_combine",
    )(pair_slot.reshape(-1), h, top_w, ln_g.reshape(1, d), ln_b.reshape(1, d), y)


def moe_ffn_ln(x, top_w, top_idx, w_gate_up, b_gate_up, w_down, b_down, ln_g, ln_b):
    n, d = x.shape
    tm = MOE_TILE_M
    nt = n * TOP_K // tm + N_EXPERTS
    onehot = (top_idx[:, :, None] == jnp.arange(N_EXPERTS, dtype=top_idx.dtype)).astype(jnp.int32)
    routed = jnp.sum(onehot, axis=1)
    before = jnp.cumsum(routed, axis=0) - routed
    counts = before[-1] + routed[-1]
    tiles_per = (counts + tm - 1) // tm
    tile_end = jnp.cumsum(tiles_per)
    first_row = (tile_end - tiles_per) * tm
    pair_slot = jnp.sum(onehot * (before + first_row)[:, None, :], axis=2)
    tile_ids = jnp.arange(nt, dtype=jnp.int32)
    tile_expert = jnp.minimum(jnp.sum((tile_ids[:, None] >= tile_end[None, :]).astype(jnp.int32), axis=1),
                              N_EXPERTS - 1).astype(jnp.int32)
    tile_valid = (tile_ids < tile_end[-1]).astype(jnp.int32)

    x_sorted = _moe_scatter_rows(x, pair_slot, nt * tm)
    y = _moe_grouped(x_sorted, tile_expert, tile_valid, w_gate_up, b_gate_up, w_down, b_down)
    return _moe_combine_ln(x, y, pair_slot, top_w, ln_g, ln_b)


def kernel(x_prompt, x_sample, cache_win0_kv, cache_win1_kv, cache_win2_kv, cache_mem_kv,
           state_wkv, state_shift, mem_prompt,
           w_in, w_gate, b_gate, w_branch, w_out, w_mem_kv,
           mu_rkv, mu_wag, w0, w1, w2, a0, a1, a2, g1, g2, k_k, k_a, r_k, gn_g, gn_b,
           ln1_g, ln1_b, w_router, b_router, w_gate_up, b_gate_up, w_down, b_down, ln2_g, ln2_b):
    bf16 = jnp.bfloat16
    xp, xs = x_prompt, x_sample
    bp, tp, d = xp.shape
    bs, ts, _ = xs.shape
    n_prompt = bp * tp
    l = 0
    lp = {"mu_rkv": mu_rkv[l], "mu_wag": mu_wag[l],
          "w0": w0[l], "w1": w1[l], "w2": w2[l], "a0": a0[l], "a1": a1[l], "a2": a2[l],
          "g1": g1[l], "g2": g2[l], "k_k": k_k[l], "k_a": k_a[l], "r_k": r_k[l]}

    xp2, xs2 = xp.reshape(-1, d), xs.reshape(-1, d)
    u_all = jnp.concatenate([xp2, xs2], axis=0)
    w_in_bf = w_in[l].astype(bf16)
    proj_p2 = matmul_bias(xp2.astype(bf16), w_in_bf, tile_m=PROJ_PROMPT_TILE_M, name="proj_in")
    proj_s2 = matmul_bias(xs2.astype(bf16), w_in_bf, tile_m=xs2.shape[0], name="proj_in_step")
    gate = matmul_bias(u_all.astype(bf16), w_gate[l].astype(bf16), b_gate[l], tile_m=PROJ_TILE_M, sigmoid=True,
                       name="proj_gate")
    mem_rows = mem_prompt.reshape(-1, d)
    mem_kv_p = matmul_bias(mem_rows.astype(bf16), w_mem_kv[l].astype(bf16), tile_m=mem_rows.shape[0], name="proj_mem")
    proj_p = proj_p2.reshape(bp, tp, PROJ_WIDTH)
    proj_s = proj_s2.reshape(bs, ts, PROJ_WIDTH)
    rkv0, qm0 = 3 * ATT_WIDTH, 3 * ATT_WIDTH + 3 * RWKV_WIDTH

    heads = lambda t: t.reshape(t.shape[0], t.shape[1], N_ATT_HEADS, HEAD_DIM)
    qa_p, ka_p, va_p = (heads(proj_p[..., i * ATT_WIDTH:(i + 1) * ATT_WIDTH]) for i in range(3))
    ka_s, va_s = (heads(proj_s[..., i * ATT_WIDTH:(i + 1) * ATT_WIDTH]) for i in (1, 2))
    outs, lses, win_p, win_s = [], [], [], []
    for g, (win, dil) in enumerate(DIL_GROUPS):
        hs = slice(g * HEADS_PER_GROUP, (g + 1) * HEADS_PER_GROUP)
        o, lse = dilated_attention_prompt(qa_p[:, :, hs], ka_p[:, :, hs], va_p[:, :, hs], g, dil)
        outs.append(o.reshape(n_prompt, ATT_OUT))
        lses.append(lse.reshape(n_prompt, ATT_OUT))
        keep = min(win, tp)
        win_p.append(jnp.stack([ka_p[:, tp - keep:, hs], va_p[:, tp - keep:, hs]], axis=2))
        win_s.append(jnp.stack([ka_s[:, :, hs], va_s[:, :, hs]], axis=2))
    o_att_p = att_combine(outs, lses)
    o_att_s = dilated_attention_step(proj_s, cache_win0_kv[l], cache_win1_kv[l], cache_win2_kv[l])

    tq = MEM_ATTN_TILE_Q
    o_mem_p = memory_attention(proj_p2.reshape(-1, tq, PROJ_WIDTH), qm0 // MEM_WIDTH,
                               mem_kv_p.reshape(bp, MEM_TOKENS, 2 * MEM_WIDTH), tp // tq)
    o_mem_s = memory_attention(proj_s, qm0 // MEM_WIDTH, cache_mem_kv[l], 1, n_seq=MEM_ATTN_STEP_SEQS)

    tm = RWKV_PRE_TILE_M

    def tile_prev_rows(x):
        last = x[:, tm - 1::tm][:, :-1]
        return jnp.concatenate([jnp.zeros_like(last[:, :1]), last], axis=1).reshape(-1, 1, x.shape[-1])

    rwkv_in_p = rwkv_pre(xp2, tile_prev_rows(xp), proj_p2, tile_prev_rows(proj_p[..., rkv0:qm0]), lp, rolled=True)
    u_last_s = state_shift[l].astype(xs.dtype)
    p_last_s = matmul_bias(u_last_s.astype(bf16), w_in_bf[:, rkv0:qm0], tile_m=bs, name="proj_shift")
    shift = lambda first, rest: jnp.concatenate([first[:, None], rest[:, :-1]], axis=1).reshape(-1, rest.shape[-1])
    rwkv_in_s = rwkv_pre(xs2, shift(u_last_s, xs), proj_s2, shift(p_last_s, proj_s[..., rkv0:qm0]), lp, rolled=False)
    s0 = jnp.zeros((bp, RWKV_HEADS, RWKV_HEAD_DIM, RWKV_HEAD_DIM), jnp.float32)
    o_rwkv_p, wkv_p = rwkv_recurrence(
        s0, [x.reshape(-1, RWKV_CHUNK, RWKV_WIDTH) for x in rwkv_in_p], gn_g[l], gn_b[l],
        n_batch=bp, n_chunks=tp // RWKV_CHUNK, seq_rows=RWKV_CHUNK, n_seq=1, chunk=RWKV_CHUNK)
    o_rwkv_s, wkv_s = rwkv_recurrence(
        state_wkv[l].astype(jnp.float32), [x.reshape(bs, ts, RWKV_WIDTH) for x in rwkv_in_s], gn_g[l], gn_b[l],
        n_batch=bs, n_chunks=1, seq_rows=ts, n_seq=RWKV_SEQ_PER_STEP, chunk=-(-ts // SUBLANES) * SUBLANES)

    cat = lambda a, b, w: jnp.concatenate([a.reshape(-1, w), b.reshape(-1, w)], axis=0)
    h, top_w, top_idx = merge_ln_router(
        cat(o_att_p, o_att_s, ATT_OUT), cat(o_rwkv_p, o_rwkv_s, RWKV_WIDTH), cat(o_mem_p, o_mem_s, MEM_WIDTH),
        gate, u_all, w_branch[l].astype(bf16), w_out[l].astype(bf16), ln1_g[l], ln1_b[l], w_router[l], b_router[l])
    y_all = moe_ffn_ln(h, top_w, top_idx, w_gate_up[l], b_gate_up[l], w_down[l], b_down[l], ln2_g[l], ln2_b[l])
    yp = y_all[:n_prompt].reshape(bp, tp, d)
    ys = y_all[n_prompt:].reshape(bs, ts, d)
    mem_kv_out = mem_kv_p.reshape(bp, MEM_TOKENS, 2, MEM_HEADS, MEM_HEAD_DIM)
    return (yp, ys,
            win_p[0][None], win_p[1][None], win_p[2][None], wkv_p.astype(xp.dtype)[None],
            xp[:, -1][None], mem_kv_out[None],
            win_s[0].astype(cache_win0_kv.dtype)[None], win_s[1].astype(cache_win0_kv.dtype)[None],
            win_s[2].astype(cache_win0_kv.dtype)[None], wkv_s.astype(state_wkv.dtype)[None],
            xs[:, -1].astype(state_shift.dtype)[None])
```

```python
import functools

import jax
import jax.numpy as jnp
from jax import lax
from jax.experimental import pallas as pl
from jax.experimental.pallas import tpu as pltpu

D_MODEL = 2048
DEPTH = 1
HEAD_DIM = 64
DIL_GROUPS = ((128, 1), (512, 4), (2048, 16))
HEADS_PER_GROUP = 4
N_ATT_HEADS = HEADS_PER_GROUP * len(DIL_GROUPS)
ATT_WIDTH = N_ATT_HEADS * HEAD_DIM
ATT_OUT = HEADS_PER_GROUP * HEAD_DIM
BAND_BLOCK = 128
N_BACK = 128
GROUP_WIDTH = HEADS_PER_GROUP * HEAD_DIM
RWKV_HEADS = 12
RWKV_HEAD_DIM = 64
RWKV_WIDTH = RWKV_HEADS * RWKV_HEAD_DIM
GN_EPS = 64e-5
MEM_TOKENS = 256
MEM_HEADS = 4
MEM_HEAD_DIM = 128
MEM_WIDTH = MEM_HEADS * MEM_HEAD_DIM
PROJ_WIDTH = 3 * ATT_WIDTH + 3 * RWKV_WIDTH + MEM_WIDTH
N_BRANCHES = 3
N_EXPERTS = 32
TOP_K = 4
D_EXPERT = 2048
SWIGLU_LIMIT = 7.0
SWIGLU_ALPHA = 1.702
LN_EPS = 1e-5
DEEPNORM_ALPHA = (2.0 * DEPTH) ** 0.25
NEG_INF = -1e30

VMEM_LIMIT_BYTES = 56 * 1024 * 1024
SUBLANES = 8
LANES = 128
MOE_TILE_M = 256
MOE_F_CHUNK = 1024
MOE_D_CHUNK = 1024
MOE_ROUTE_TILE = 64
MOE_VMEM_LIMIT_BYTES = 58 * 1024 * 1024
SCHED_COLS = 7
RWKV_CHUNK = 64
RWKV_SEQ_PER_STEP = 8
RWKV_PRE_TILE_M = 256
PROJ_TILE_M = 1088
PROJ_PROMPT_TILE_M = 1024
MEM_ATTN_STEP_SEQS = 8
MATMUL_TILE_N = 512
MERGE_TILE_M = 256
ATT_COMBINE_TILE = 512
MEM_ATTN_TILE_Q = 512
ATTN_STEP_SEQS = 2


def _matmul_kernel(x_ref, w_ref, b_ref, o_ref, *, sigmoid):
    y = jnp.dot(x_ref[...], w_ref[...], preferred_element_type=jnp.float32) + b_ref[...]
    o_ref[...] = jax.nn.sigmoid(y) if sigmoid else y


def matmul_bias(x, w, bias=None, *, tile_m, sigmoid=False, name="matmul"):
    m, k = x.shape
    n = w.shape[1]
    tn = MATMUL_TILE_N if n % MATMUL_TILE_N == 0 else n
    if bias is None:
        bias = jnp.zeros((n,), jnp.float32)
    return pl.pallas_call(
        functools.partial(_matmul_kernel, sigmoid=sigmoid),
        out_shape=jax.ShapeDtypeStruct((m, n), jnp.float32),
        grid=(n // tn, m // tile_m),
        in_specs=[pl.BlockSpec((tile_m, k), lambda j, i: (i, 0)),
                  pl.BlockSpec((k, tn), lambda j, i: (0, j)),
                  pl.BlockSpec((1, tn), lambda j, i: (0, j))],
        out_specs=pl.BlockSpec((tile_m, tn), lambda j, i: (i, j)),
        compiler_params=pltpu.CompilerParams(
            dimension_semantics=("arbitrary", "arbitrary"), vmem_limit_bytes=VMEM_LIMIT_BYTES),
        name=name,
    )(x, w, bias.reshape(1, n))


def _merge_kernel(att_ref, rwkv_ref, mem_ref, gate_ref, x_ref, wb_ref, wo_ref, g_ref, b_ref,
                  wr_hi_ref, wr_lo_ref, br_ref, h_ref, topw_ref, topi_ref):
    bf16, f32 = jnp.bfloat16, jnp.float32
    d = x_ref.shape[1]
    dot = functools.partial(jnp.dot, preferred_element_type=f32)
    y = dot(att_ref[...].astype(bf16), wb_ref[0:ATT_OUT]) * gate_ref[:, 0:d]
    y = y + dot(rwkv_ref[...].astype(bf16), wb_ref[ATT_OUT:ATT_OUT + RWKV_WIDTH]) * gate_ref[:, d:2 * d]
    y = y + dot(mem_ref[...].astype(bf16), wb_ref[ATT_OUT + RWKV_WIDTH:]) * gate_ref[:, 2 * d:]
    z = DEEPNORM_ALPHA * x_ref[...] + dot(y.astype(bf16), wo_ref[...])
    mu = jnp.mean(z, -1, keepdims=True)
    zc = z - mu
    var = jnp.mean(zc * zc, -1, keepdims=True)
    h = zc * lax.rsqrt(var + LN_EPS) * g_ref[...] + b_ref[...]
    h_ref[...] = h

    h_hi = h.astype(bf16)
    h_lo = (h - h_hi.astype(f32)).astype(bf16)
    logits = (dot(h_hi, wr_hi_ref[...]) + dot(h_hi, wr_lo_ref[...]) + dot(h_lo, wr_hi_ref[...])) + br_ref[...]
    lane = lax.broadcasted_iota(jnp.int32, logits.shape, 1)
    work = jnp.where(lane < N_EXPERTS, logits, -jnp.inf)
    tops, ids = [], []
    for _ in range(TOP_K):
        m = jnp.max(work, -1, keepdims=True)
        idx = jnp.min(jnp.where(work == m, lane, LANES), -1, keepdims=True)
        tops.append(m)
        ids.append(idx)
        work = jnp.where(lane == idx, -jnp.inf, work)
    e = [jnp.exp(t - tops[0]) for t in tops]
    den = e[0] + e[1] + e[2] + e[3]
    tw = jnp.zeros(logits.shape, f32)
    ti = jnp.zeros(logits.shape, jnp.int32)
    for k in range(TOP_K):
        tw = jnp.where(lane == k, e[k] / den, tw)
        ti = jnp.where(lane == k, ids[k], ti)
    topw_ref[...] = tw
    topi_ref[...] = ti


def merge_ln_router(o_att, o_rwkv, o_mem, gate, x, w_branch, w_out, ln_g, ln_b, w_router, b_router):
    n, d = x.shape
    tm = MERGE_TILE_M
    wr = jnp.zeros((d, LANES), jnp.float32).at[:, :N_EXPERTS].set(w_router)
    wr_hi = wr.astype(jnp.bfloat16)
    wr_lo = (wr - wr_hi.astype(jnp.float32)).astype(jnp.bfloat16)
    br = jnp.zeros((1, LANES), jnp.float32).at[0, :N_EXPERTS].set(b_router)
    rows = lambda w: pl.BlockSpec((tm, w), lambda i: (i, 0))
    full = lambda a: pl.BlockSpec(a.shape, lambda i: (0, 0))
    h, topw, topi = pl.pallas_call(
        _merge_kernel,
        out_shape=(jax.ShapeDtypeStruct((n, d), jnp.float32),
                   jax.ShapeDtypeStruct((n, LANES), jnp.float32),
                   jax.ShapeDtypeStruct((n, LANES), jnp.int32)),
        grid=(n // tm,),
        in_specs=[rows(ATT_OUT), rows(RWKV_WIDTH), rows(MEM_WIDTH), rows(3 * d), rows(d),
                  full(w_branch), full(w_out), pl.BlockSpec((1, d), lambda i: (0, 0)),
                  pl.BlockSpec((1, d), lambda i: (0, 0)), full(wr_hi), full(wr_lo), full(br)],
        out_specs=(rows(d), rows(LANES), rows(LANES)),
        compiler_params=pltpu.CompilerParams(
            dimension_semantics=("arbitrary",), vmem_limit_bytes=VMEM_LIMIT_BYTES),
        name="merge_ln_router",
    )(o_att, o_rwkv, o_mem, gate, x, w_branch, w_out, ln_g.reshape(1, d), ln_b.reshape(1, d), wr_hi, wr_lo, br)
    return h, topw[:, :TOP_K], topi[:, :TOP_K]


def _alibi_slope(head):
    return 2.0 ** (-8.0 * (head + 1) / N_ATT_HEADS)


def _dot_nt(a, b):
    return lax.dot_general(a, b, (((1,), (1,)), ((), ())), preferred_element_type=jnp.float32)


def _dil_attn_prompt_kernel(q_ref, kc_ref, kp_ref, vc_ref, vp_ref, o_ref, lse_ref, *, group, dil):
    bf16 = jnp.bfloat16
    blk = BAND_BLOCK
    i = pl.program_id(1)
    qi = lax.broadcasted_iota(jnp.int32, (blk, blk), 0)
    kj = lax.broadcasted_iota(jnp.int32, (blk, blk), 1)
    dist_c = qi - kj
    dist_p = dist_c + blk
    valid_c = dist_c >= 0
    valid_p = (dist_p <= blk) & (i > 0)
    scale = HEAD_DIM ** -0.5
    for h in range(HEADS_PER_GROUP):
        slope = _alibi_slope(group * HEADS_PER_GROUP + h) * dil
        q = q_ref[0, h].astype(bf16)
        s_c = _dot_nt(q, kc_ref[0, h].astype(bf16)) * scale - slope * dist_c.astype(jnp.float32)
        s_p = _dot_nt(q, kp_ref[0, h].astype(bf16)) * scale - slope * dist_p.astype(jnp.float32)
        s_c = jnp.where(valid_c, s_c, NEG_INF)
        s_p = jnp.where(valid_p, s_p, NEG_INF)
        m = jnp.maximum(jnp.max(s_c, -1, keepdims=True), jnp.max(s_p, -1, keepdims=True))
        p_c = jnp.exp(s_c - m)
        p_p = jnp.exp(s_p - m)
        den = jnp.sum(p_c, -1, keepdims=True) + jnp.sum(p_p, -1, keepdims=True)
        pv = (jnp.dot(p_c.astype(bf16), vc_ref[0, h].astype(bf16), preferred_element_type=jnp.float32)
              + jnp.dot(p_p.astype(bf16), vp_ref[0, h].astype(bf16), preferred_element_type=jnp.float32))
        o_ref[0, h] = pv / den
        lse_ref[0, h] = jnp.broadcast_to(m + jnp.log(den), (blk, HEAD_DIM))


def dilated_attention_prompt(q, k, v, group, dil):
    b, t, h, e = q.shape
    l = t // dil
    assert l % BAND_BLOCK == 0

    def by_residue(a):
        return a.reshape(b, l, dil, h, e).transpose(0, 2, 3, 1, 4).reshape(b * dil, h, l, e)

    def from_residue(a):
        return a.reshape(b, dil, h, l, e).transpose(0, 3, 1, 2, 4).reshape(b, t, h, e)

    cur = pl.BlockSpec((1, h, BAND_BLOCK, e), lambda n, i: (n, 0, i, 0))
    prev = pl.BlockSpec((1, h, BAND_BLOCK, e), lambda n, i: (n, 0, jnp.maximum(i - 1, 0), 0))
    qr, kr, vr = by_residue(q), by_residue(k), by_residue(v)
    o, lse = pl.pallas_call(
        functools.partial(_dil_attn_prompt_kernel, group=group, dil=dil),
        out_shape=(jax.ShapeDtypeStruct(qr.shape, jnp.float32), jax.ShapeDtypeStruct(qr.shape, jnp.float32)),
        grid=(b * dil, l // BAND_BLOCK),
        in_specs=[cur, cur, prev, cur, prev],
        out_specs=(cur, cur),
        compiler_params=pltpu.CompilerParams(
            dimension_semantics=("arbitrary", "arbitrary"), vmem_limit_bytes=VMEM_LIMIT_BYTES),
        name="dil_attn_prompt_g%d" % group,
    )(qr, kr, kr, vr, vr)
    return from_residue(o), from_residue(lse)


def _att_combine_kernel(o0, o1, o2, l0, l1, l2, out_ref):
    a, b, c = l0[...], l1[...], l2[...]
    m = jnp.maximum(jnp.maximum(a, b), c)
    ea, eb, ec = jnp.exp(a - m), jnp.exp(b - m), jnp.exp(c - m)
    out_ref[...] = (ea * o0[...] + eb * o1[...] + ec * o2[...]) / (ea + eb + ec)


def att_combine(outs, lses):
    n, w = outs[0].shape
    spec = pl.BlockSpec((ATT_COMBINE_TILE, w), lambda i: (i, 0))
    return pl.pallas_call(
        _att_combine_kernel,
        out_shape=jax.ShapeDtypeStruct((n, w), jnp.float32),
        grid=(n // ATT_COMBINE_TILE,),
        in_specs=[spec] * 6,
        out_specs=spec,
        compiler_params=pltpu.CompilerParams(dimension_semantics=("arbitrary",)),
        name="att_combine",
    )(*outs, *lses)


def _mem_attn_kernel(q_ref, kv_ref, o_ref):
    bf16 = jnp.bfloat16
    e = MEM_HEAD_DIM
    scale = e ** -0.5
    for n, h in [(n, h) for n in range(q_ref.shape[0]) for h in range(MEM_HEADS)]:
        q = q_ref[n, :, h * e:(h + 1) * e].astype(bf16)
        if len(kv_ref.shape) == 3:
            k = kv_ref[n, :, h * e:(h + 1) * e].astype(bf16)
            v = kv_ref[n, :, (MEM_HEADS + h) * e:(MEM_HEADS + h + 1) * e].astype(bf16)
        else:
            k = kv_ref[n, :, 0, h, :].astype(bf16)
            v = kv_ref[n, :, 1, h, :].astype(bf16)
        s = _dot_nt(q, k) * scale
        m = jnp.max(s, -1, keepdims=True)
        p = jnp.exp(s - m)
        p = p / jnp.sum(p, -1, keepdims=True)
        o_ref[n, :, h * e:(h + 1) * e] = jnp.dot(p.astype(bf16), v, preferred_element_type=jnp.float32)


def memory_attention(q, q_col_block, mem_kv, blocks_per_seq, n_seq=1):
    b = mem_kv.shape[0]
    tq = q.shape[1]
    w = MEM_WIDTH
    assert n_seq == 1 or blocks_per_seq == 1
    kv_zeros = (0,) * (mem_kv.ndim - 1)
    return pl.pallas_call(
        _mem_attn_kernel,
        out_shape=jax.ShapeDtypeStruct((b * blocks_per_seq, tq, w), jnp.float32),
        grid=(b // n_seq, blocks_per_seq),
        in_specs=[pl.BlockSpec((n_seq, tq, w), lambda n, i: (n * blocks_per_seq + i, 0, q_col_block)),
                  pl.BlockSpec((n_seq,) + mem_kv.shape[1:], lambda n, i: (n,) + kv_zeros)],
        out_specs=pl.BlockSpec((n_seq, tq, w), lambda n, i: (n * blocks_per_seq + i, 0, 0)),
        compiler_params=pltpu.CompilerParams(
            dimension_semantics=("arbitrary", "arbitrary"), vmem_limit_bytes=VMEM_LIMIT_BYTES),
        name="mem_attn",
    )(q, mem_kv)


def _dil_attn_step_kernel(qkv_ref, c0_ref, c1_ref, c2_ref, o_ref, *, n_new):
    bf16, f32 = jnp.bfloat16, jnp.float32
    scale = HEAD_DIM ** -0.5
    caches = (c0_ref, c1_ref, c2_ref)
    tq = lax.broadcasted_iota(jnp.int32, (n_new, n_new), 0)
    tk = lax.broadcasted_iota(jnp.int32, (n_new, n_new), 1)
    for b, h in [(b, h) for b in range(qkv_ref.shape[0]) for h in range(HEADS_PER_GROUP)]:
        outs, lses = [], []
        for g, (win, dil) in enumerate(DIL_GROUPS):
            head = g * HEADS_PER_GROUP + h
            slope = _alibi_slope(head)
            lanes = slice(head * HEAD_DIM, (head + 1) * HEAD_DIM)
            q = qkv_ref[b, :, lanes].astype(bf16)
            k_new = qkv_ref[b, :, ATT_WIDTH + head * HEAD_DIM:ATT_WIDTH + (head + 1) * HEAD_DIM].astype(bf16)
            v_new = qkv_ref[b, :, 2 * ATT_WIDTH + head * HEAD_DIM:2 * ATT_WIDTH + (head + 1) * HEAD_DIM].astype(bf16)
            s = jnp.dot(q, caches[g][b, 0, h].astype(bf16), preferred_element_type=f32) * scale
            dist = (win + lax.broadcasted_iota(jnp.int32, (n_new, win), 0)
                    - lax.broadcasted_iota(jnp.int32, (n_new, win), 1))
            valid = ((dist & (dil - 1)) == 0) & (dist <= N_BACK * dil)
            s = jnp.where(valid, s - slope * dist.astype(f32), NEG_INF)
            sn = _dot_nt(q, k_new) * scale
            dn = tq - tk
            sn = jnp.where((dn >= 0) & ((dn & (dil - 1)) == 0), sn - slope * dn.astype(f32), NEG_INF)
            m = jnp.maximum(jnp.max(s, axis=1, keepdims=True), jnp.max(sn, axis=1, keepdims=True))
            p = jnp.exp(s - m)
            pn = jnp.exp(sn - m)
            den = jnp.sum(p, axis=1, keepdims=True) + jnp.sum(pn, axis=1, keepdims=True)
            pv = (_dot_nt(p.astype(bf16), caches[g][b, 1, h].astype(bf16))
                  + jnp.dot(pn.astype(bf16), v_new, preferred_element_type=f32))
            outs.append(pv / den)
            lses.append(m + jnp.log(den))
        mm = jnp.maximum(jnp.maximum(lses[0], lses[1]), lses[2])
        w = [jnp.exp(l - mm) for l in lses]
        o_ref[b, :, h * HEAD_DIM:(h + 1) * HEAD_DIM] = (
            (w[0] * outs[0] + w[1] * outs[1] + w[2] * outs[2]) / (w[0] + w[1] + w[2]))


def dilated_attention_step(p_s, cache0, cache1, cache2):
    b, t, _ = p_s.shape
    feat_major = lambda c: jnp.transpose(c, (0, 2, 3, 4, 1))
    ns = ATTN_STEP_SEQS
    win_spec = lambda w: pl.BlockSpec((ns, 2, HEADS_PER_GROUP, HEAD_DIM, w), lambda n: (n, 0, 0, 0, 0))
    return pl.pallas_call(
        functools.partial(_dil_attn_step_kernel, n_new=t),
        out_shape=jax.ShapeDtypeStruct((b, t, GROUP_WIDTH), jnp.float32),
        grid=(b // ns,),
        in_specs=[pl.BlockSpec((ns, t, 3 * ATT_WIDTH), lambda n: (n, 0, 0))] + [win_spec(w) for w, _ in DIL_GROUPS],
        out_specs=pl.BlockSpec((ns, t, GROUP_WIDTH), lambda n: (n, 0, 0)),
        compiler_params=pltpu.CompilerParams(
            dimension_semantics=("arbitrary",), vmem_limit_bytes=VMEM_LIMIT_BYTES),
        name="dil_attn_step",
    )(p_s, feat_major(cache0), feat_major(cache1), feat_major(cache2))


def _rwkv_pre_kernel(u_ref, up_ref, p_ref, pp_ref, mu_rkv_ref, mu_wag_ref, w0_ref, w1_ref, w2_ref,
                     a0_ref, a1_ref, a2_ref, g1_ref, g2_ref, kk_ref, ka_ref, rk_ref,
                     r_ref, lw_ref, k_ref, v_ref, an_ref, b_ref, g_ref, bonus_ref, *, rolled):
    bf16, f32 = jnp.bfloat16, jnp.float32
    dot = functools.partial(jnp.dot, preferred_element_type=f32)
    w = RWKV_WIDTH
    u = u_ref[...]
    p = p_ref[...]
    if rolled:
        first = lax.broadcasted_iota(jnp.int32, (u.shape[0], 1), 0) == 0
        u_prev = jnp.where(first, up_ref[0], pltpu.roll(u, 1, 0))
        p_prev = jnp.where(first, pp_ref[0], pltpu.roll(p, 1, 0))
    else:
        u_prev, p_prev = up_ref[...], pp_ref[...]
    du = u_prev - u
    rkv = p + (p_prev - p) * mu_rkv_ref[...]
    r, k, v = rkv[:, :w], rkv[:, w:2 * w], rkv[:, 2 * w:]
    xw = (u + du * mu_wag_ref[0:1]).astype(bf16)
    xa = (u + du * mu_wag_ref[1:2]).astype(bf16)
    xg = (u + du * mu_wag_ref[2:3]).astype(bf16)
    y = -(w0_ref[...] + dot(jnp.tanh(dot(xw, w1_ref[...])).astype(bf16), w2_ref[...]))
    softplus = jnp.maximum(y, 0.0) + jnp.log1p(jnp.exp(-jnp.abs(y)))
    lw_ref[...] = -jnp.exp(-softplus - 0.5)
    a = jax.nn.sigmoid(a0_ref[...] + dot(dot(xa, a1_ref[...]).astype(bf16), a2_ref[...]))
    g_ref[...] = dot(jax.nn.sigmoid(dot(xg, g1_ref[...])).astype(bf16), g2_ref[...])

    rr = lax.broadcasted_iota(jnp.int32, (LANES, LANES), 0) // RWKV_HEAD_DIM
    cc = lax.broadcasted_iota(jnp.int32, (LANES, LANES), 1) // RWKV_HEAD_DIM
    seg_ones = jnp.where(rr == cc, 1.0, 0.0).astype(bf16)

    def head_sum(x):
        hi = x.astype(bf16)
        lo = (x - hi.astype(f32)).astype(bf16)
        cols = [dot(hi[:, j:j + LANES], seg_ones) + dot(lo[:, j:j + LANES], seg_ones)
                for j in range(0, w, LANES)]
        return jnp.concatenate(cols, axis=1)

    kk = k * kk_ref[...]
    kk = kk / jnp.maximum(jnp.sqrt(head_sum(kk * kk)), 1e-12)
    k2 = k * (1.0 + (a - 1.0) * ka_ref[...])
    r_ref[...] = r
    k_ref[...] = k2
    v_ref[...] = v
    an_ref[...] = -kk
    b_ref[...] = kk * a
    bonus_ref[...] = head_sum(r * k2 * rk_ref[...]) * v


def rwkv_pre(u, u_prev, proj, p_prev, lp, *, rolled):
    n, d = u.shape
    tm = RWKV_PRE_TILE_M
    w3 = 3 * RWKV_WIDTH
    if rolled:
        prev_specs = [pl.BlockSpec((1, 1, d), lambda i: (i, 0, 0)), pl.BlockSpec((1, 1, w3), lambda i: (i, 0, 0))]
    else:
        prev_specs = [pl.BlockSpec((tm, d), lambda i: (i, 0)), pl.BlockSpec((tm, w3), lambda i: (i, 0))]
    bf16 = jnp.bfloat16
    row = lambda x: x.reshape(1, -1)
    consts = [row(lp["mu_rkv"]), lp["mu_wag"], row(lp["w0"]), lp["w1"].astype(bf16), lp["w2"].astype(bf16),
              row(lp["a0"]), lp["a1"].astype(bf16), lp["a2"].astype(bf16), lp["g1"].astype(bf16),
              lp["g2"].astype(bf16), row(lp["k_k"]), row(lp["k_a"]), row(lp["r_k"])]
    full = lambda a: pl.BlockSpec(a.shape, lambda i: (0, 0))
    out_spec = pl.BlockSpec((tm, RWKV_WIDTH), lambda i: (i, 0))
    return pl.pallas_call(
        functools.partial(_rwkv_pre_kernel, rolled=rolled),
        out_shape=tuple(jax.ShapeDtypeStruct((n, RWKV_WIDTH), jnp.float32) for _ in range(8)),
        grid=(n // tm,),
        in_specs=[pl.BlockSpec((tm, d), lambda i: (i, 0)), prev_specs[0],
                  pl.BlockSpec((tm, w3), lambda i: (i, 1)), prev_specs[1]]
                 + [full(c) for c in consts],
        out_specs=tuple(out_spec for _ in range(8)),
        compiler_params=pltpu.CompilerParams(
            dimension_semantics=("arbitrary",), vmem_limit_bytes=VMEM_LIMIT_BYTES),
        name="rwkv_pre",
    )(u, u_prev, proj, p_prev, *consts)


def _bmm(a, b):
    return lax.dot_general(a, b, (((2,), (1,)), ((0,), (0,))), preferred_element_type=jnp.float32)


def _bmm_nt(a, b):
    return lax.dot_general(a, b, (((2,), (2,)), ((0,), (0,))), preferred_element_type=jnp.float32)


def _bmm_tn(a, b):
    return lax.dot_general(a, b, (((1,), (1,)), ((0,), (0,))), preferred_element_type=jnp.float32)


def _rwkv_chunk_kernel(r_ref, lw_ref, k_ref, v_ref, a_ref, b_ref, g_ref, bonus_ref, gng_ref, gnb_ref, s0_ref,
                       o_ref, sT_ref, s_scr, *, n_seq, seq_rows, chunk):
    bf16, f32 = jnp.bfloat16, jnp.float32
    c = pl.program_id(1)
    C = chunk
    H, N = RWKV_HEADS, RWKV_HEAD_DIM

    @pl.when(c == 0)
    def _():
        s_scr[...] = s0_ref[...]

    row = lax.broadcasted_iota(jnp.int32, (C, C), 0)
    col = lax.broadcasted_iota(jnp.int32, (C, C), 1)
    strict = (row > col)[None]
    incl = (row >= col)[None]
    tri = jnp.broadcast_to(jnp.where(incl, 1.0, 0.0).astype(bf16), (H, C, C))

    def by_head(ref, i):
        x = ref[i]
        if seq_rows < C:
            x = jnp.concatenate([x, jnp.zeros((C - seq_rows, x.shape[1]), f32)], axis=0)
        return jnp.stack([x[:, h * N:(h + 1) * N] for h in range(H)], axis=0)

    def one_seq(i, carry):
        r, lw, k, v, a, b = (by_head(ref, i) for ref in (r_ref, lw_ref, k_ref, v_ref, a_ref, b_ref))
        s0 = s_scr[i]
        hi = lw.astype(bf16)
        r1 = lw - hi.astype(f32)
        mid = r1.astype(bf16)
        lo = (r1 - mid.astype(f32)).astype(bf16)
        lc = _bmm(tri, hi) + _bmm(tri, mid) + _bmm(tri, lo)
        e_pos = jnp.exp(lc)
        e_neg = jnp.exp(-lc)
        a_t = (a * jnp.exp(lc - lw)).astype(bf16)
        r_t = (r * e_pos).astype(bf16)
        b_t = (b * e_neg).astype(bf16)
        k_t = (k * e_neg).astype(bf16)
        v_b = v.astype(bf16)
        s0_b = s0.astype(bf16)

        a_ab = jnp.where(strict, _bmm_nt(a_t, b_t), 0.0)
        a_ak = jnp.where(strict, _bmm_nt(a_t, k_t), 0.0)
        a_rb = jnp.where(incl, _bmm_nt(r_t, b_t), 0.0)
        a_rk = jnp.where(incl, _bmm_nt(r_t, k_t), 0.0)

        x = _bmm_nt(a_t, s0_b) + _bmm(a_ak.astype(bf16), v_b)
        p = a_ab
        n = 1
        while n < C:
            p_b = p.astype(bf16)
            x = x + _bmm(p_b, x.astype(bf16))
            n *= 2
            if n < C:
                p = _bmm(p_b, p_b)
        u_b = x.astype(bf16)
        o = _bmm_nt(r_t, s0_b) + _bmm(a_rb.astype(bf16), u_b) + _bmm(a_rk.astype(bf16), v_b)
        s1 = s0 + _bmm_tn(u_b, b_t) + _bmm_tn(v_b, k_t)
        s_scr[i] = s1 * e_pos[:, C - 1:C, :]

        mean = jnp.mean(o, -1, keepdims=True)
        oc = o - mean
        var = jnp.mean(oc * oc, -1, keepdims=True)
        on = oc * lax.rsqrt(var + GN_EPS)
        on = jnp.concatenate([on[h] for h in range(H)], axis=1)[:seq_rows]
        o_ref[i] = (on * gng_ref[...] + gnb_ref[...] + bonus_ref[i]) * g_ref[i]
        return carry

    lax.fori_loop(0, n_seq, one_seq, 0)

    @pl.when(c == pl.num_programs(1) - 1)
    def _():
        sT_ref[...] = s_scr[...]


def rwkv_recurrence(s0, seqs, gn_g, gn_b, *, n_batch, n_chunks, seq_rows, n_seq, chunk):
    h, n = RWKV_HEADS, RWKV_HEAD_DIM
    w = RWKV_WIDTH
    seq_spec = pl.BlockSpec((n_seq, seq_rows, w), lambda i, c: (i * n_chunks + c, 0, 0))
    st_spec = pl.BlockSpec((n_seq, h, n, n), lambda i, c: (i, 0, 0, 0))
    vec_spec = pl.BlockSpec((1, w), lambda i, c: (0, 0))
    o, s_t = pl.pallas_call(
        functools.partial(_rwkv_chunk_kernel, n_seq=n_seq, seq_rows=seq_rows, chunk=chunk),
        out_shape=(jax.ShapeDtypeStruct((n_batch * n_chunks, seq_rows, w), jnp.float32),
                   jax.ShapeDtypeStruct((n_batch, h, n, n), jnp.float32)),
        grid=(n_batch // n_seq, n_chunks),
        in_specs=[seq_spec] * 8 + [vec_spec, vec_spec, st_spec],
        out_specs=(seq_spec, st_spec),
        scratch_shapes=[pltpu.VMEM((n_seq, h, n, n), jnp.float32)],
        compiler_params=pltpu.CompilerParams(
            dimension_semantics=("arbitrary", "arbitrary"), vmem_limit_bytes=VMEM_LIMIT_BYTES),
        name="rwkv_chunk",
    )(*seqs, gn_g.reshape(1, w), gn_b.reshape(1, w), s0)
    return o, s_t


def _moe_scatter_kernel(slot_ref, h_ref, init_hbm, out_hbm, stage, sems):
    del init_hbm
    i = pl.program_id(0)
    nt = pl.num_programs(0)
    tt = h_ref.shape[0]
    slot = i % 2

    def wait_buffer(s):
        for _ in range(TOP_K):
            pltpu.make_async_copy(stage.at[s], out_hbm.at[pl.ds(0, tt)], sems.at[s]).wait()

    @pl.when(i >= 2)
    def _():
        wait_buffer(slot)

    stage[slot] = h_ref[...]
    for j in range(tt):
        for k in range(TOP_K):
            dst = slot_ref[(i * tt + j) * TOP_K + k]
            pltpu.make_async_copy(stage.at[slot, pl.ds(j, 1)], out_hbm.at[pl.ds(dst, 1)], sems.at[slot]).start()

    @pl.when(i == nt - 1)
    def _():
        wait_buffer(slot)

        @pl.when(nt >= 2)
        def _():
            wait_buffer(1 - slot)


def _moe_scatter_rows(h, pair_slot, p_pad):
    n, d = h.shape
    tt = MOE_ROUTE_TILE
    return pl.pallas_call(
        _moe_scatter_kernel,
        out_shape=jax.ShapeDtypeStruct((p_pad, d), jnp.float32),
        grid_spec=pltpu.PrefetchScalarGridSpec(
            num_scalar_prefetch=1,
            grid=(n // tt,),
            in_specs=[pl.BlockSpec((tt, d), lambda i, ps: (i, 0)),
                      pl.BlockSpec(memory_space=pl.ANY)],
            out_specs=pl.BlockSpec(memory_space=pl.ANY),
            scratch_shapes=[pltpu.VMEM((2, tt, d), jnp.float32), pltpu.SemaphoreType.DMA((2,))],
        ),
        input_output_aliases={2: 0},
        compiler_params=pltpu.CompilerParams(
            dimension_semantics=("arbitrary",), vmem_limit_bytes=VMEM_LIMIT_BYTES),
        name="moe_scatter",
    )(pair_slot.reshape(-1), h, jnp.zeros((p_pad, d), jnp.float32))


def _weight_copies(w_hbm, stage, sem, expert, col0, width, slot, n_parts, part_stride):
    return [pltpu.make_async_copy(w_hbm.at[expert, :, pl.ds(pl.multiple_of(col0 + p * part_stride, LANES), width)],
                                  stage.at[slot, p], sem.at[slot]) for p in range(n_parts)]


def _stream_expert_weights(sched_ref, w_hbm, stage, sem, w_bf, *, width, n_parts, part_stride):
    s = pl.program_id(0) * pl.num_programs(1) + pl.program_id(1)
    field = lambda j: sched_ref[s * SCHED_COLS + j]

    @pl.when(field(0) != 0)
    def _():
        slot = field(1)
        cur = _weight_copies(w_hbm, stage, sem, field(2), field(3), width, slot, n_parts, part_stride)

        @pl.when(s == 0)
        def _():
            for cp in cur:
                cp.start()

        for cp in cur:
            cp.wait()
        for p in range(n_parts):
            w_bf[p] = stage[slot, p].astype(jnp.bfloat16)

        @pl.when(field(4) != 0)
        def _():
            for cp in _weight_copies(w_hbm, stage, sem, field(5), field(6), width, 1 - slot, n_parts, part_stride):
                cp.start()


def _weight_schedule(tile_expert, n_chunks, chunk_width):
    nt = tile_expert.shape[0]
    n = n_chunks * nt
    expert = jnp.tile(tile_expert, n_chunks)
    col0 = jnp.repeat(jnp.arange(n_chunks, dtype=jnp.int32) * chunk_width, nt)
    step = jnp.arange(n, dtype=jnp.int32)
    start = (step % nt == 0) | (expert != jnp.roll(expert, 1))
    slot = (jnp.cumsum(start.astype(jnp.int32)) - 1) % 2
    nxt = lax.cummin(jnp.where(start, step, n), axis=0, reverse=True)
    nxt = jnp.concatenate([nxt[1:], jnp.full((1,), n, jnp.int32)])
    has_next = nxt < n
    nxt_c = jnp.minimum(nxt, n - 1)
    return jnp.stack([start.astype(jnp.int32), slot, expert, col0, has_next.astype(jnp.int32),
                      expert[nxt_c], col0[nxt_c]], axis=1).astype(jnp.int32).reshape(-1)


def _moe_up_kernel(sched_ref, tile_valid_ref, x_ref, bg_ref, bu_ref, w_hbm, h_ref, stage, sem, w_bf):
    f = w_hbm.shape[2] // 2
    _stream_expert_weights(sched_ref, w_hbm, stage, sem, w_bf, width=h_ref.shape[1], n_parts=2, part_stride=f)
    i = pl.program_id(1)

    @pl.when(tile_valid_ref[i] != 0)
    def _():
        x = x_ref[...].astype(jnp.bfloat16)
        hg = jnp.dot(x, w_bf[0], preferred_element_type=jnp.float32) + bg_ref[0]
        hu = jnp.dot(x, w_bf[1], preferred_element_type=jnp.float32) + bu_ref[0]
        glu = jnp.minimum(hg, SWIGLU_LIMIT)
        up = jnp.clip(hu, -SWIGLU_LIMIT, SWIGLU_LIMIT)
        act = (up + 1.0) * glu * jax.nn.sigmoid(SWIGLU_ALPHA * glu)
        h_ref[...] = act.astype(h_ref.dtype)

    @pl.when(tile_valid_ref[i] == 0)
    def _():
        h_ref[...] = jnp.zeros_like(h_ref)


def _moe_down_kernel(sched_ref, tile_valid_ref, h_ref, bd_ref, w_hbm, y_ref, stage, sem, w_bf):
    _stream_expert_weights(sched_ref, w_hbm, stage, sem, w_bf, width=y_ref.shape[1], n_parts=1, part_stride=0)
    i = pl.program_id(1)

    @pl.when(tile_valid_ref[i] != 0)
    def _():
        y_ref[...] = jnp.dot(h_ref[...], w_bf[0], preferred_element_type=jnp.float32) + bd_ref[0]

    @pl.when(tile_valid_ref[i] == 0)
    def _():
        y_ref[...] = jnp.zeros_like(y_ref)


def _moe_grouped(x_sorted, tile_expert, tile_valid, w_gate_up, b_gate_up, w_down, b_down):
    p_pad, d = x_sorted.shape
    n_exp, _, two_f = w_gate_up.shape
    f = two_f // 2
    tm, fc, dc = MOE_TILE_M, MOE_F_CHUNK, MOE_D_CHUNK
    nt = p_pad // tm
    nfc = f // fc
    ndc = d // dc
    bgu = b_gate_up.reshape(n_exp, 1, two_f)
    bdn = b_down.reshape(n_exp, 1, d)
    params = pltpu.CompilerParams(dimension_semantics=("arbitrary", "arbitrary"),
                                  vmem_limit_bytes=MOE_VMEM_LIMIT_BYTES)
    te_of = lambda sched, c, i: sched[(c * nt + i) * SCHED_COLS + 2]

    h = pl.pallas_call(
        _moe_up_kernel,
        out_shape=jax.ShapeDtypeStruct((p_pad, f), jnp.bfloat16),
        grid_spec=pltpu.PrefetchScalarGridSpec(
            num_scalar_prefetch=2,
            grid=(nfc, nt),
            in_specs=[
                pl.BlockSpec((tm, d), lambda c, i, sc, tv: (i, 0)),
                pl.BlockSpec((1, 1, fc), lambda c, i, sc, tv: (te_of(sc, c, i), 0, c)),
                pl.BlockSpec((1, 1, fc), lambda c, i, sc, tv: (te_of(sc, c, i), 0, nfc + c)),
                pl.BlockSpec(memory_space=pl.ANY),
            ],
            out_specs=pl.BlockSpec((tm, fc), lambda c, i, sc, tv: (i, c)),
            scratch_shapes=[pltpu.VMEM((2, 2, d, fc), jnp.float32), pltpu.SemaphoreType.DMA((2,)),
                            pltpu.VMEM((2, d, fc), jnp.bfloat16)],
        ),
        compiler_params=params,
        name="moe_up",
    )(_weight_schedule(tile_expert, nfc, fc), tile_valid, x_sorted, bgu, bgu, w_gate_up)

    return pl.pallas_call(
        _moe_down_kernel,
        out_shape=jax.ShapeDtypeStruct((p_pad, d), jnp.float32),
        grid_spec=pltpu.PrefetchScalarGridSpec(
            num_scalar_prefetch=2,
            grid=(ndc, nt),
            in_specs=[
                pl.BlockSpec((tm, f), lambda c, i, sc, tv: (i, 0)),
                pl.BlockSpec((1, 1, dc), lambda c, i, sc, tv: (te_of(sc, c, i), 0, c)),
                pl.BlockSpec(memory_space=pl.ANY),
            ],
            out_specs=pl.BlockSpec((tm, dc), lambda c, i, sc, tv: (i, c)),
            scratch_shapes=[pltpu.VMEM((2, 1, f, dc), jnp.float32), pltpu.SemaphoreType.DMA((2,)),
                            pltpu.VMEM((1, f, dc), jnp.bfloat16)],
        ),
        compiler_params=params,
        name="moe_down",
    )(_weight_schedule(tile_expert, ndc, dc), tile_valid, h, bdn, w_down)


def _moe_combine_kernel(slot_ref, h_ref, w_ref, g_ref, b_ref, y_hbm, out_ref, buf, sems):
    i = pl.program_id(0)
    nt = pl.num_programs(0)
    tt = h_ref.shape[0]

    def start_tile(t, s):
        for j in range(tt):
            for k in range(TOP_K):
                src = slot_ref[(t * tt + j) * TOP_K + k]
                pltpu.make_async_copy(y_hbm.at[pl.ds(src, 1)], buf.at[s, pl.ds(k * tt + j, 1)], sems.at[s]).start()

    @pl.when(i == 0)
    def _():
        start_tile(0, 0)

    @pl.when(i + 1 < nt)
    def _():
        start_tile(i + 1, (i + 1) % 2)

    slot = i % 2
    pltpu.make_async_copy(y_hbm.at[pl.ds(0, TOP_K * tt)], buf.at[slot], sems.at[slot]).wait()
    w = w_ref[...]
    ffn = buf[slot, 0:tt] * w[:, 0:1]
    for k in range(1, TOP_K):
        ffn = ffn + buf[slot, k * tt:(k + 1) * tt] * w[:, k:k + 1]
    x = DEEPNORM_ALPHA * h_ref[...] + ffn
    mu = jnp.mean(x, -1, keepdims=True)
    xc = x - mu
    var = jnp.mean(xc * xc, -1, keepdims=True)
    out_ref[...] = xc * lax.rsqrt(var + LN_EPS) * g_ref[...] + b_ref[...]


def _moe_combine_ln(h, y, pair_slot, top_w, ln_g, ln_b):
    n, d = h.shape
    tt = MOE_ROUTE_TILE
    return pl.pallas_call(
        _moe_combine_kernel,
        out_shape=jax.ShapeDtypeStruct((n, d), jnp.float32),
        grid_spec=pltpu.PrefetchScalarGridSpec(
            num_scalar_prefetch=1,
            grid=(n // tt,),
            in_specs=[pl.BlockSpec((tt, d), lambda i, ps: (i, 0)),
                      pl.BlockSpec((tt, TOP_K), lambda i, ps: (i, 0)),
                      pl.BlockSpec((1, d), lambda i, ps: (0, 0)),
                      pl.BlockSpec((1, d), lambda i, ps: (0, 0)),
                      pl.BlockSpec(memory_space=pl.ANY)],
            out_specs=pl.BlockSpec((tt, d), lambda i, ps: (i, 0)),
            scratch_shapes=[pltpu.VMEM((2, TOP_K * tt, d), jnp.float32), pltpu.SemaphoreType.DMA((2,))],
        ),
        compiler_params=pltpu.CompilerParams(
            dimension_semantics=("arbitrary",), vmem_limit_bytes=VMEM_LIMIT_BYTES),
        name="moe_combine",
    )(pair_slot.reshape(-1), h, top_w, ln_g.reshape(1, d), ln_b.reshape(1, d), y)


def moe_ffn_ln(x, top_w, top_idx, w_gate_up, b_gate_up, w_down, b_down, ln_g, ln_b):
    n, d = x.shape
    tm = MOE_TILE_M
    nt = n * TOP_K // tm + N_EXPERTS
    onehot = (top_idx[:, :, None] == jnp.arange(N_EXPERTS, dtype=top_idx.dtype)).astype(jnp.int32)
    routed = jnp.sum(onehot, axis=1)
    before = jnp.cumsum(routed, axis=0) - routed
    counts = before[-1] + routed[-1]
    tiles_per = (counts + tm - 1) // tm
    tile_end = jnp.cumsum(tiles_per)
    first_row = (tile_end - tiles_per) * tm
    pair_slot = jnp.sum(onehot * (before + first_row)[:, None, :], axis=2)
    tile_ids = jnp.arange(nt, dtype=jnp.int32)
    tile_expert = jnp.minimum(jnp.sum((tile_ids[:, None] >= tile_end[None, :]).astype(jnp.int32), axis=1),
                              N_EXPERTS - 1).astype(jnp.int32)
    tile_valid = (tile_ids < tile_end[-1]).astype(jnp.int32)

    x_sorted = _moe_scatter_rows(x, pair_slot, nt * tm)
    y = _moe_grouped(x_sorted, tile_expert, tile_valid, w_gate_up, b_gate_up, w_down, b_down)
    return _moe_combine_ln(x, y, pair_slot, top_w, ln_g, ln_b)


def kernel(x_prompt, x_sample, cache_win0_kv, cache_win1_kv, cache_win2_kv, cache_mem_kv,
           state_wkv, state_shift, mem_prompt,
           w_in, w_gate, b_gate, w_branch, w_out, w_mem_kv,
           mu_rkv, mu_wag, w0, w1, w2, a0, a1, a2, g1, g2, k_k, k_a, r_k, gn_g, gn_b,
           ln1_g, ln1_b, w_router, b_router, w_gate_up, b_gate_up, w_down, b_down, ln2_g, ln2_b):
    bf16 = jnp.bfloat16
    xp, xs = x_prompt, x_sample
    bp, tp, d = xp.shape
    bs, ts, _ = xs.shape
    n_prompt = bp * tp
    l = 0
    lp = {"mu_rkv": mu_rkv[l], "mu_wag": mu_wag[l],
          "w0": w0[l], "w1": w1[l], "w2": w2[l], "a0": a0[l], "a1": a1[l], "a2": a2[l],
          "g1": g1[l], "g2": g2[l], "k_k": k_k[l], "k_a": k_a[l], "r_k": r_k[l]}

    xp2, xs2 = xp.reshape(-1, d), xs.reshape(-1, d)
    u_all = jnp.concatenate([xp2, xs2], axis=0)
    w_in_bf = w_in[l].astype(bf16)
    proj_p2 = matmul_bias(xp2.astype(bf16), w_in_bf, tile_m=PROJ_PROMPT_TILE_M, name="proj_in")
    proj_s2 = matmul_bias(xs2.astype(bf16), w_in_bf, tile_m=xs2.shape[0], name="proj_in_step")
    gate = matmul_bias(u_all.astype(bf16), w_gate[l].astype(bf16), b_gate[l], tile_m=PROJ_TILE_M, sigmoid=True,
                       name="proj_gate")
    mem_rows = mem_prompt.reshape(-1, d)
    mem_kv_p = matmul_bias(mem_rows.astype(bf16), w_mem_kv[l].astype(bf16), tile_m=mem_rows.shape[0], name="proj_mem")
    proj_p = proj_p2.reshape(bp, tp, PROJ_WIDTH)
    proj_s = proj_s2.reshape(bs, ts, PROJ_WIDTH)
    rkv0, qm0 = 3 * ATT_WIDTH, 3 * ATT_WIDTH + 3 * RWKV_WIDTH

    heads = lambda t: t.reshape(t.shape[0], t.shape[1], N_ATT_HEADS, HEAD_DIM)
    qa_p, ka_p, va_p = (heads(proj_p[..., i * ATT_WIDTH:(i + 1) * ATT_WIDTH]) for i in range(3))
    ka_s, va_s = (heads(proj_s[..., i * ATT_WIDTH:(i + 1) * ATT_WIDTH]) for i in (1, 2))
    outs, lses, win_p, win_s = [], [], [], []
    for g, (win, dil) in enumerate(DIL_GROUPS):
        hs = slice(g * HEADS_PER_GROUP, (g + 1) * HEADS_PER_GROUP)
        o, lse = dilated_attention_prompt(qa_p[:, :, hs], ka_p[:, :, hs], va_p[:, :, hs], g, dil)
        outs.append(o.reshape(n_prompt, ATT_OUT))
        lses.append(lse.reshape(n_prompt, ATT_OUT))
        keep = min(win, tp)
        win_p.append(jnp.stack([ka_p[:, tp - keep:, hs], va_p[:, tp - keep:, hs]], axis=2))
        win_s.append(jnp.stack([ka_s[:, :, hs], va_s[:, :, hs]], axis=2))
    o_att_p = att_combine(outs, lses)
    o_att_s = dilated_attention_step(proj_s, cache_win0_kv[l], cache_win1_kv[l], cache_win2_kv[l])

    tq = MEM_ATTN_TILE_Q
    o_mem_p = memory_attention(proj_p2.reshape(-1, tq, PROJ_WIDTH), qm0 // MEM_WIDTH,
                               mem_kv_p.reshape(bp, MEM_TOKENS, 2 * MEM_WIDTH), tp // tq)
    o_mem_s = memory_attention(proj_s, qm0 // MEM_WIDTH, cache_mem_kv[l], 1, n_seq=MEM_ATTN_STEP_SEQS)

    tm = RWKV_PRE_TILE_M

    def tile_prev_rows(x):
        last = x[:, tm - 1::tm][:, :-1]
        return jnp.concatenate([jnp.zeros_like(last[:, :1]), last], axis=1).reshape(-1, 1, x.shape[-1])

    rwkv_in_p = rwkv_pre(xp2, tile_prev_rows(xp), proj_p2, tile_prev_rows(proj_p[..., rkv0:qm0]), lp, rolled=True)
    u_last_s = state_shift[l].astype(xs.dtype)
    p_last_s = matmul_bias(u_last_s.astype(bf16), w_in_bf[:, rkv0:qm0], tile_m=bs, name="proj_shift")
    shift = lambda first, rest: jnp.concatenate([first[:, None], rest[:, :-1]], axis=1).reshape(-1, rest.shape[-1])
    rwkv_in_s = rwkv_pre(xs2, shift(u_last_s, xs), proj_s2, shift(p_last_s, proj_s[..., rkv0:qm0]), lp, rolled=False)
    s0 = jnp.zeros((bp, RWKV_HEADS, RWKV_HEAD_DIM, RWKV_HEAD_DIM), jnp.float32)
    o_rwkv_p, wkv_p = rwkv_recurrence(
        s0, [x.reshape(-1, RWKV_CHUNK, RWKV_WIDTH) for x in rwkv_in_p], gn_g[l], gn_b[l],
        n_batch=bp, n_chunks=tp // RWKV_CHUNK, seq_rows=RWKV_CHUNK, n_seq=1, chunk=RWKV_CHUNK)
    o_rwkv_s, wkv_s = rwkv_recurrence(
        state_wkv[l].astype(jnp.float32), [x.reshape(bs, ts, RWKV_WIDTH) for x in rwkv_in_s], gn_g[l], gn_b[l],
        n_batch=bs, n_chunks=1, seq_rows=ts, n_seq=RWKV_SEQ_PER_STEP, chunk=-(-ts // SUBLANES) * SUBLANES)

    cat = lambda a, b, w: jnp.concatenate([a.reshape(-1, w), b.reshape(-1, w)], axis=0)
    h, top_w, top_idx = merge_ln_router(
        cat(o_att_p, o_att_s, ATT_OUT), cat(o_rwkv_p, o_rwkv_s, RWKV_WIDTH), cat(o_mem_p, o_mem_s, MEM_WIDTH),
        gate, u_all, w_branch[l].astype(bf16), w_out[l].astype(bf16), ln1_g[l], ln1_b[l], w_router[l], b_router[l])
    y_all = moe_ffn_ln(h, top_w, top_idx, w_gate_up[l], b_gate_up[l], w_down[l], b_down[l], ln2_g[l], ln2_b[l])
    yp = y_all[:n_prompt].reshape(bp, tp, d)
    ys = y_all[n_prompt:].reshape(bs, ts, d)
    mem_kv_out = mem_kv_p.reshape(bp, MEM_TOKENS, 2, MEM_HEADS, MEM_HEAD_DIM)
    return (yp, ys,
            win_p[0][None], win_p[1][None], win_p[2][None], wkv_p.astype(xp.dtype)[None],
            xp[:, -1][None], mem_kv_out[None],
            win_s[0].astype(cache_win0_kv.dtype)[None], win_s[1].astype(cache_win0_kv.dtype)[None],
            win_s[2].astype(cache_win0_kv.dtype)[None], wkv_s.astype(state_wkv.dtype)[None],
            xs[:, -1].astype(state_shift.dtype)[None])
```

```python
import functools

import jax
import jax.numpy as jnp
from jax import lax
from jax.experimental import pallas as pl
from jax.experimental.pallas import tpu as pltpu

D_MODEL = 2048
DEPTH = 1
HEAD_DIM = 64
DIL_GROUPS = ((128, 1), (512, 4), (2048, 16))
HEADS_PER_GROUP = 4
N_ATT_HEADS = HEADS_PER_GROUP * len(DIL_GROUPS)
ATT_WIDTH = N_ATT_HEADS * HEAD_DIM
ATT_OUT = HEADS_PER_GROUP * HEAD_DIM
BAND_BLOCK = 128
N_BACK = 128
GROUP_WIDTH = HEADS_PER_GROUP * HEAD_DIM
RWKV_HEADS = 12
RWKV_HEAD_DIM = 64
RWKV_WIDTH = RWKV_HEADS * RWKV_HEAD_DIM
GN_EPS = 64e-5
MEM_TOKENS = 256
MEM_HEADS = 4
MEM_HEAD_DIM = 128
MEM_WIDTH = MEM_HEADS * MEM_HEAD_DIM
PROJ_WIDTH = 3 * ATT_WIDTH + 3 * RWKV_WIDTH + MEM_WIDTH
N_BRANCHES = 3
N_EXPERTS = 32
TOP_K = 4
D_EXPERT = 2048
SWIGLU_LIMIT = 7.0
SWIGLU_ALPHA = 1.702
LN_EPS = 1e-5
DEEPNORM_ALPHA = (2.0 * DEPTH) ** 0.25
NEG_INF = -1e30

VMEM_LIMIT_BYTES = 56 * 1024 * 1024
SUBLANES = 8
LANES = 128
MOE_TILE_M = 256
MOE_F_CHUNK = 1024
MOE_D_CHUNK = 1024
MOE_ROUTE_TILE = 64
MOE_VMEM_LIMIT_BYTES = 58 * 1024 * 1024
SCHED_COLS = 7
RWKV_CHUNK = 64
RWKV_SEQ_PER_STEP = 8
RWKV_SEQ_GROUP = 4
RWKV_PRE_TILE_M = 256
PROJ_TILE_M = 1088
PROJ_PROMPT_TILE_M = 1024
MEM_ATTN_STEP_SEQS = 8
MATMUL_TILE_N = 512
MERGE_TILE_M = 256
ATT_COMBINE_TILE = 512
MEM_ATTN_TILE_Q = 512
ATTN_STEP_SEQS = 2


def _matmul_kernel(x_ref, w_ref, b_ref, o_ref, *, sigmoid):
    y = jnp.dot(x_ref[...], w_ref[...], preferred_element_type=jnp.float32) + b_ref[...]
    o_ref[...] = jax.nn.sigmoid(y) if sigmoid else y


def matmul_bias(x, w, bias=None, *, tile_m, sigmoid=False, name="matmul"):
    m, k = x.shape
    n = w.shape[1]
    tn = MATMUL_TILE_N if n % MATMUL_TILE_N == 0 else n
    if bias is None:
        bias = jnp.zeros((n,), jnp.float32)
    return pl.pallas_call(
        functools.partial(_matmul_kernel, sigmoid=sigmoid),
        out_shape=jax.ShapeDtypeStruct((m, n), jnp.float32),
        grid=(n // tn, m // tile_m),
        in_specs=[pl.BlockSpec((tile_m, k), lambda j, i: (i, 0)),
                  pl.BlockSpec((k, tn), lambda j, i: (0, j)),
                  pl.BlockSpec((1, tn), lambda j, i: (0, j))],
        out_specs=pl.BlockSpec((tile_m, tn), lambda j, i: (i, j)),
        compiler_params=pltpu.CompilerParams(
            dimension_semantics=("arbitrary", "arbitrary"), vmem_limit_bytes=VMEM_LIMIT_BYTES),
        name=name,
    )(x, w, bias.reshape(1, n))


def _merge_kernel(att_ref, rwkv_ref, mem_ref, gate_ref, x_ref, wb_ref, wo_ref, g_ref, b_ref,
                  wr_hi_ref, wr_lo_ref, br_ref, h_ref, topw_ref, topi_ref):
    bf16, f32 = jnp.bfloat16, jnp.float32
    d = x_ref.shape[1]
    dot = functools.partial(jnp.dot, preferred_element_type=f32)
    y = dot(att_ref[...].astype(bf16), wb_ref[0:ATT_OUT]) * gate_ref[:, 0:d]
    y = y + dot(rwkv_ref[...].astype(bf16), wb_ref[ATT_OUT:ATT_OUT + RWKV_WIDTH]) * gate_ref[:, d:2 * d]
    y = y + dot(mem_ref[...].astype(bf16), wb_ref[ATT_OUT + RWKV_WIDTH:]) * gate_ref[:, 2 * d:]
    z = DEEPNORM_ALPHA * x_ref[...] + dot(y.astype(bf16), wo_ref[...])
    mu = jnp.mean(z, -1, keepdims=True)
    zc = z - mu
    var = jnp.mean(zc * zc, -1, keepdims=True)
    h = zc * lax.rsqrt(var + LN_EPS) * g_ref[...] + b_ref[...]
    h_ref[...] = h

    h_hi = h.astype(bf16)
    h_lo = (h - h_hi.astype(f32)).astype(bf16)
    logits = (dot(h_hi, wr_hi_ref[...]) + dot(h_hi, wr_lo_ref[...]) + dot(h_lo, wr_hi_ref[...])) + br_ref[...]
    lane = lax.broadcasted_iota(jnp.int32, logits.shape, 1)
    work = jnp.where(lane < N_EXPERTS, logits, -jnp.inf)
    tops, ids = [], []
    for _ in range(TOP_K):
        m = jnp.max(work, -1, keepdims=True)
        idx = jnp.min(jnp.where(work == m, lane, LANES), -1, keepdims=True)
        tops.append(m)
        ids.append(idx)
        work = jnp.where(lane == idx, -jnp.inf, work)
    e = [jnp.exp(t - tops[0]) for t in tops]
    den = e[0] + e[1] + e[2] + e[3]
    tw = jnp.zeros(logits.shape, f32)
    ti = jnp.zeros(logits.shape, jnp.int32)
    for k in range(TOP_K):
        tw = jnp.where(lane == k, e[k] / den, tw)
        ti = jnp.where(lane == k, ids[k], ti)
    topw_ref[...] = tw
    topi_ref[...] = ti


def merge_ln_router(o_att, o_rwkv, o_mem, gate, x, w_branch, w_out, ln_g, ln_b, w_router, b_router):
    n, d = x.shape
    tm = MERGE_TILE_M
    wr = jnp.zeros((d, LANES), jnp.float32).at[:, :N_EXPERTS].set(w_router)
    wr_hi = wr.astype(jnp.bfloat16)
    wr_lo = (wr - wr_hi.astype(jnp.float32)).astype(jnp.bfloat16)
    br = jnp.zeros((1, LANES), jnp.float32).at[0, :N_EXPERTS].set(b_router)
    rows = lambda w: pl.BlockSpec((tm, w), lambda i: (i, 0))
    full = lambda a: pl.BlockSpec(a.shape, lambda i: (0, 0))
    h, topw, topi = pl.pallas_call(
        _merge_kernel,
        out_shape=(jax.ShapeDtypeStruct((n, d), jnp.float32),
                   jax.ShapeDtypeStruct((n, LANES), jnp.float32),
                   jax.ShapeDtypeStruct((n, LANES), jnp.int32)),
        grid=(n // tm,),
        in_specs=[rows(ATT_OUT), rows(RWKV_WIDTH), rows(MEM_WIDTH), rows(3 * d), rows(d),
                  full(w_branch), full(w_out), pl.BlockSpec((1, d), lambda i: (0, 0)),
                  pl.BlockSpec((1, d), lambda i: (0, 0)), full(wr_hi), full(wr_lo), full(br)],
        out_specs=(rows(d), rows(LANES), rows(LANES)),
        compiler_params=pltpu.CompilerParams(
            dimension_semantics=("arbitrary",), vmem_limit_bytes=VMEM_LIMIT_BYTES),
        name="merge_ln_router",
    )(o_att, o_rwkv, o_mem, gate, x, w_branch, w_out, ln_g.reshape(1, d), ln_b.reshape(1, d), wr_hi, wr_lo, br)
    return h, topw[:, :TOP_K], topi[:, :TOP_K]


def _alibi_slope(head):
    return 2.0 ** (-8.0 * (head + 1) / N_ATT_HEADS)


def _dot_nt(a, b):
    return lax.dot_general(a, b, (((1,), (1,)), ((), ())), preferred_element_type=jnp.float32)


def _dil_attn_prompt_kernel(q_ref, kc_ref, kp_ref, vc_ref, vp_ref, o_ref, lse_ref, *, group, dil):
    bf16 = jnp.bfloat16
    blk = BAND_BLOCK
    i = pl.program_id(1)
    qi = lax.broadcasted_iota(jnp.int32, (blk, blk), 0)
    kj = lax.broadcasted_iota(jnp.int32, (blk, blk), 1)
    dist_c = qi - kj
    dist_p = dist_c + blk
    valid_c = dist_c >= 0
    valid_p = (dist_p <= blk) & (i > 0)
    scale = HEAD_DIM ** -0.5
    heads = range(HEADS_PER_GROUP)
    st = []
    for h in heads:
        slope = _alibi_slope(group * HEADS_PER_GROUP + h) * dil
        q = q_ref[0, h].astype(bf16)
        s_c = _dot_nt(q, kc_ref[0, h].astype(bf16)) * scale - slope * dist_c.astype(jnp.float32)
        s_p = _dot_nt(q, kp_ref[0, h].astype(bf16)) * scale - slope * dist_p.astype(jnp.float32)
        st.append((jnp.where(valid_c, s_c, NEG_INF), jnp.where(valid_p, s_p, NEG_INF)))
    st = [(s_c, s_p, jnp.maximum(jnp.max(s_c, -1, keepdims=True), jnp.max(s_p, -1, keepdims=True))) for s_c, s_p in st]
    st = [(jnp.exp(s_c - m), jnp.exp(s_p - m), m) for s_c, s_p, m in st]
    st = [(p_c, p_p, m, jnp.sum(p_c, -1, keepdims=True) + jnp.sum(p_p, -1, keepdims=True)) for p_c, p_p, m in st]
    for h in heads:
        p_c, p_p, m, den = st[h]
        pv = (jnp.dot(p_c.astype(bf16), vc_ref[0, h].astype(bf16), preferred_element_type=jnp.float32)
              + jnp.dot(p_p.astype(bf16), vp_ref[0, h].astype(bf16), preferred_element_type=jnp.float32))
        o_ref[0, h] = pv / den
        lse_ref[0, h] = jnp.broadcast_to(m + jnp.log(den), (blk, HEAD_DIM))


def dilated_attention_prompt(q, k, v, group, dil):
    b, t, h, e = q.shape
    l = t // dil
    assert l % BAND_BLOCK == 0

    def by_residue(a):
        return a.reshape(b, l, dil, h, e).transpose(0, 2, 3, 1, 4).reshape(b * dil, h, l, e)

    def from_residue(a):
        return a.reshape(b, dil, h, l, e).transpose(0, 3, 1, 2, 4).reshape(b, t, h, e)

    cur = pl.BlockSpec((1, h, BAND_BLOCK, e), lambda n, i: (n, 0, i, 0))
    prev = pl.BlockSpec((1, h, BAND_BLOCK, e), lambda n, i: (n, 0, jnp.maximum(i - 1, 0), 0))
    qr, kr, vr = by_residue(q), by_residue(k), by_residue(v)
    o, lse = pl.pallas_call(
        functools.partial(_dil_attn_prompt_kernel, group=group, dil=dil),
        out_shape=(jax.ShapeDtypeStruct(qr.shape, jnp.float32), jax.ShapeDtypeStruct(qr.shape, jnp.float32)),
        grid=(b * dil, l // BAND_BLOCK),
        in_specs=[cur, cur, prev, cur, prev],
        out_specs=(cur, cur),
        compiler_params=pltpu.CompilerParams(
            dimension_semantics=("arbitrary", "arbitrary"), vmem_limit_bytes=VMEM_LIMIT_BYTES),
        name="dil_attn_prompt_g%d" % group,
    )(qr, kr, kr, vr, vr)
    return from_residue(o), from_residue(lse)


def _att_combine_kernel(o0, o1, o2, l0, l1, l2, out_ref):
    a, b, c = l0[...], l1[...], l2[...]
    m = jnp.maximum(jnp.maximum(a, b), c)
    ea, eb, ec = jnp.exp(a - m), jnp.exp(b - m), jnp.exp(c - m)
    out_ref[...] = (ea * o0[...] + eb * o1[...] + ec * o2[...]) / (ea + eb + ec)


def att_combine(outs, lses):
    n, w = outs[0].shape
    spec = pl.BlockSpec((ATT_COMBINE_TILE, w), lambda i: (i, 0))
    return pl.pallas_call(
        _att_combine_kernel,
        out_shape=jax.ShapeDtypeStruct((n, w), jnp.float32),
        grid=(n // ATT_COMBINE_TILE,),
        in_specs=[spec] * 6,
        out_specs=spec,
        compiler_params=pltpu.CompilerParams(dimension_semantics=("arbitrary",)),
        name="att_combine",
    )(*outs, *lses)


def _mem_attn_kernel(q_ref, kv_ref, o_ref):
    bf16 = jnp.bfloat16
    e = MEM_HEAD_DIM
    scale = e ** -0.5
    units = [(n, h) for n in range(q_ref.shape[0]) for h in range(MEM_HEADS)]
    if len(kv_ref.shape) == 3:
        key = lambda n, h: kv_ref[n, :, h * e:(h + 1) * e]
        val = lambda n, h: kv_ref[n, :, (MEM_HEADS + h) * e:(MEM_HEADS + h + 1) * e]
    else:
        key = lambda n, h: kv_ref[n, :, 0, h, :]
        val = lambda n, h: kv_ref[n, :, 1, h, :]
    s = [_dot_nt(q_ref[n, :, h * e:(h + 1) * e].astype(bf16), key(n, h).astype(bf16)) * scale for n, h in units]
    m = [jnp.max(x, -1, keepdims=True) for x in s]
    p = [jnp.exp(x - mx) for x, mx in zip(s, m)]
    p = [x / jnp.sum(x, -1, keepdims=True) for x in p]
    for (n, h), x in zip(units, p):
        o_ref[n, :, h * e:(h + 1) * e] = jnp.dot(x.astype(bf16), val(n, h).astype(bf16),
                                                 preferred_element_type=jnp.float32)


def memory_attention(q, q_col_block, mem_kv, blocks_per_seq, n_seq=1):
    b = mem_kv.shape[0]
    tq = q.shape[1]
    w = MEM_WIDTH
    assert n_seq == 1 or blocks_per_seq == 1
    kv_zeros = (0,) * (mem_kv.ndim - 1)
    return pl.pallas_call(
        _mem_attn_kernel,
        out_shape=jax.ShapeDtypeStruct((b * blocks_per_seq, tq, w), jnp.float32),
        grid=(b // n_seq, blocks_per_seq),
        in_specs=[pl.BlockSpec((n_seq, tq, w), lambda n, i: (n * blocks_per_seq + i, 0, q_col_block)),
                  pl.BlockSpec((n_seq,) + mem_kv.shape[1:], lambda n, i: (n,) + kv_zeros)],
        out_specs=pl.BlockSpec((n_seq, tq, w), lambda n, i: (n * blocks_per_seq + i, 0, 0)),
        compiler_params=pltpu.CompilerParams(
            dimension_semantics=("arbitrary", "arbitrary"), vmem_limit_bytes=VMEM_LIMIT_BYTES),
        name="mem_attn",
    )(q, mem_kv)


def _dil_attn_step_kernel(qkv_ref, c0_ref, c1_ref, c2_ref, o_ref, *, n_new):
    bf16, f32 = jnp.bfloat16, jnp.float32
    scale = HEAD_DIM ** -0.5
    caches = (c0_ref, c1_ref, c2_ref)
    n_groups = len(DIL_GROUPS)
    tq = lax.broadcasted_iota(jnp.int32, (n_new, n_new), 0)
    tk = lax.broadcasted_iota(jnp.int32, (n_new, n_new), 1)
    slots = [(b, h) for b in range(qkv_ref.shape[0]) for h in range(HEADS_PER_GROUP)]
    units = [(b, h, g) for b, h in slots for g in range(n_groups)]
    new_cols = lambda b, part, head: qkv_ref[b, :, part * ATT_WIDTH + head * HEAD_DIM:
                                             part * ATT_WIDTH + (head + 1) * HEAD_DIM].astype(bf16)
    st = {}
    for b, h, g in units:
        win, dil = DIL_GROUPS[g]
        head = g * HEADS_PER_GROUP + h
        slope = _alibi_slope(head)
        q = new_cols(b, 0, head)
        s = jnp.dot(q, caches[g][b, 0, h].astype(bf16), preferred_element_type=f32) * scale
        dist = (win + lax.broadcasted_iota(jnp.int32, (n_new, win), 0)
                - lax.broadcasted_iota(jnp.int32, (n_new, win), 1))
        valid = ((dist & (dil - 1)) == 0) & (dist <= N_BACK * dil)
        s = jnp.where(valid, s - slope * dist.astype(f32), NEG_INF)
        sn = _dot_nt(q, new_cols(b, 1, head)) * scale
        dn = tq - tk
        sn = jnp.where((dn >= 0) & ((dn & (dil - 1)) == 0), sn - slope * dn.astype(f32), NEG_INF)
        st[b, h, g] = (s, sn)
    for u in units:
        s, sn = st[u]
        st[u] = (s, sn, jnp.maximum(jnp.max(s, axis=1, keepdims=True), jnp.max(sn, axis=1, keepdims=True)))
    for u in units:
        s, sn, m = st[u]
        p = jnp.exp(s - m)
        pn = jnp.exp(sn - m)
        st[u] = (p, pn, m, jnp.sum(p, axis=1, keepdims=True) + jnp.sum(pn, axis=1, keepdims=True))
    for b, h, g in units:
        p, pn, m, den = st[b, h, g]
        head = g * HEADS_PER_GROUP + h
        pv = (_dot_nt(p.astype(bf16), caches[g][b, 1, h].astype(bf16))
              + jnp.dot(pn.astype(bf16), new_cols(b, 2, head), preferred_element_type=f32))
        st[b, h, g] = (pv / den, m + jnp.log(den))
    for b, h in slots:
        outs = [st[b, h, g][0] for g in range(n_groups)]
        lses = [st[b, h, g][1] for g in range(n_groups)]
        mm = jnp.maximum(jnp.maximum(lses[0], lses[1]), lses[2])
        w = [jnp.exp(l - mm) for l in lses]
        o_ref[b, :, h * HEAD_DIM:(h + 1) * HEAD_DIM] = (
            (w[0] * outs[0] + w[1] * outs[1] + w[2] * outs[2]) / (w[0] + w[1] + w[2]))


def dilated_attention_step(p_s, cache0, cache1, cache2):
    b, t, _ = p_s.shape
    feat_major = lambda c: jnp.transpose(c, (0, 2, 3, 4, 1))
    ns = ATTN_STEP_SEQS
    win_spec = lambda w: pl.BlockSpec((ns, 2, HEADS_PER_GROUP, HEAD_DIM, w), lambda n: (n, 0, 0, 0, 0))
    return pl.pallas_call(
        functools.partial(_dil_attn_step_kernel, n_new=t),
        out_shape=jax.ShapeDtypeStruct((b, t, GROUP_WIDTH), jnp.float32),
        grid=(b // ns,),
        in_specs=[pl.BlockSpec((ns, t, 3 * ATT_WIDTH), lambda n: (n, 0, 0))] + [win_spec(w) for w, _ in DIL_GROUPS],
        out_specs=pl.BlockSpec((ns, t, GROUP_WIDTH), lambda n: (n, 0, 0)),
        compiler_params=pltpu.CompilerParams(
            dimension_semantics=("arbitrary",), vmem_limit_bytes=VMEM_LIMIT_BYTES),
        name="dil_attn_step",
    )(p_s, feat_major(cache0), feat_major(cache1), feat_major(cache2))


def _rwkv_pre_kernel(u_ref, up_ref, p_ref, pp_ref, mu_rkv_ref, mu_wag_ref, w0_ref, w1_ref, w2_ref,
                     a0_ref, a1_ref, a2_ref, g1_ref, g2_ref, kk_ref, ka_ref, rk_ref,
                     r_ref, lw_ref, k_ref, v_ref, an_ref, b_ref, g_ref, bonus_ref, *, rolled):
    bf16, f32 = jnp.bfloat16, jnp.float32
    dot = functools.partial(jnp.dot, preferred_element_type=f32)
    w = RWKV_WIDTH
    u = u_ref[...]
    p = p_ref[...]
    if rolled:
        first = lax.broadcasted_iota(jnp.int32, (u.shape[0], 1), 0) == 0
        u_prev = jnp.where(first, up_ref[0], pltpu.roll(u, 1, 0))
        p_prev = jnp.where(first, pp_ref[0], pltpu.roll(p, 1, 0))
    else:
        u_prev, p_prev = up_ref[...], pp_ref[...]
    du = u_prev - u
    rkv = p + (p_prev - p) * mu_rkv_ref[...]
    r, k, v = rkv[:, :w], rkv[:, w:2 * w], rkv[:, 2 * w:]
    xw = (u + du * mu_wag_ref[0:1]).astype(bf16)
    xa = (u + du * mu_wag_ref[1:2]).astype(bf16)
    xg = (u + du * mu_wag_ref[2:3]).astype(bf16)
    y = -(w0_ref[...] + dot(jnp.tanh(dot(xw, w1_ref[...])).astype(bf16), w2_ref[...]))
    softplus = jnp.maximum(y, 0.0) + jnp.log1p(jnp.exp(-jnp.abs(y)))
    lw_ref[...] = -jnp.exp(-softplus - 0.5)
    a = jax.nn.sigmoid(a0_ref[...] + dot(dot(xa, a1_ref[...]).astype(bf16), a2_ref[...]))
    g_ref[...] = dot(jax.nn.sigmoid(dot(xg, g1_ref[...])).astype(bf16), g2_ref[...])

    rr = lax.broadcasted_iota(jnp.int32, (LANES, LANES), 0) // RWKV_HEAD_DIM
    cc = lax.broadcasted_iota(jnp.int32, (LANES, LANES), 1) // RWKV_HEAD_DIM
    seg_ones = jnp.where(rr == cc, 1.0, 0.0).astype(bf16)

    def head_sum(x):
        hi = x.astype(bf16)
        lo = (x - hi.astype(f32)).astype(bf16)
        cols = [dot(hi[:, j:j + LANES], seg_ones) + dot(lo[:, j:j + LANES], seg_ones)
                for j in range(0, w, LANES)]
        return jnp.concatenate(cols, axis=1)

    kk = k * kk_ref[...]
    kk = kk / jnp.maximum(jnp.sqrt(head_sum(kk * kk)), 1e-12)
    k2 = k * (1.0 + (a - 1.0) * ka_ref[...])
    r_ref[...] = r
    k_ref[...] = k2
    v_ref[...] = v
    an_ref[...] = -kk
    b_ref[...] = kk * a
    bonus_ref[...] = head_sum(r * k2 * rk_ref[...]) * v


def rwkv_pre(u, u_prev, proj, p_prev, lp, *, rolled):
    n, d = u.shape
    tm = RWKV_PRE_TILE_M
    w3 = 3 * RWKV_WIDTH
    if rolled:
        prev_specs = [pl.BlockSpec((1, 1, d), lambda i: (i, 0, 0)), pl.BlockSpec((1, 1, w3), lambda i: (i, 0, 0))]
    else:
        prev_specs = [pl.BlockSpec((tm, d), lambda i: (i, 0)), pl.BlockSpec((tm, w3), lambda i: (i, 0))]
    bf16 = jnp.bfloat16
    row = lambda x: x.reshape(1, -1)
    consts = [row(lp["mu_rkv"]), lp["mu_wag"], row(lp["w0"]), lp["w1"].astype(bf16), lp["w2"].astype(bf16),
              row(lp["a0"]), lp["a1"].astype(bf16), lp["a2"].astype(bf16), lp["g1"].astype(bf16),
              lp["g2"].astype(bf16), row(lp["k_k"]), row(lp["k_a"]), row(lp["r_k"])]
    full = lambda a: pl.BlockSpec(a.shape, lambda i: (0, 0))
    out_spec = pl.BlockSpec((tm, RWKV_WIDTH), lambda i: (i, 0))
    return pl.pallas_call(
        functools.partial(_rwkv_pre_kernel, rolled=rolled),
        out_shape=tuple(jax.ShapeDtypeStruct((n, RWKV_WIDTH), jnp.float32) for _ in range(8)),
        grid=(n // tm,),
        in_specs=[pl.BlockSpec((tm, d), lambda i: (i, 0)), prev_specs[0],
                  pl.BlockSpec((tm, w3), lambda i: (i, 1)), prev_specs[1]]
                 + [full(c) for c in consts],
        out_specs=tuple(out_spec for _ in range(8)),
        compiler_params=pltpu.CompilerParams(
            dimension_semantics=("arbitrary",), vmem_limit_bytes=VMEM_LIMIT_BYTES),
        name="rwkv_pre",
    )(u, u_prev, proj, p_prev, *consts)


def _bmm(a, b):
    return lax.dot_general(a, b, (((2,), (1,)), ((0,), (0,))), preferred_element_type=jnp.float32)


def _bmm_nt(a, b):
    return lax.dot_general(a, b, (((2,), (2,)), ((0,), (0,))), preferred_element_type=jnp.float32)


def _bmm_tn(a, b):
    return lax.dot_general(a, b, (((1,), (1,)), ((0,), (0,))), preferred_element_type=jnp.float32)


def _rwkv_chunk_kernel(r_ref, lw_ref, k_ref, v_ref, a_ref, b_ref, g_ref, bonus_ref, gng_ref, gnb_ref, s0_ref,
                       o_ref, sT_ref, s_scr, *, n_seq, seq_rows, chunk, group):
    bf16, f32 = jnp.bfloat16, jnp.float32
    c = pl.program_id(1)
    C = chunk
    H, N = RWKV_HEADS, RWKV_HEAD_DIM

    @pl.when(c == 0)
    def _():
        s_scr[...] = s0_ref[...]

    row = lax.broadcasted_iota(jnp.int32, (C, C), 0)
    col = lax.broadcasted_iota(jnp.int32, (C, C), 1)
    strict = (row > col)[None]
    incl = (row >= col)[None]
    G = group
    tri = jnp.broadcast_to(jnp.where(incl, 1.0, 0.0).astype(bf16), (G * H, C, C))

    def by_head(ref, i):
        x = ref[i]
        if seq_rows < C:
            x = jnp.concatenate([x, jnp.zeros((C - seq_rows, x.shape[1]), f32)], axis=0)
        return jnp.stack([x[:, h * N:(h + 1) * N] for h in range(H)], axis=0)

    def one_seq(j, carry):
        seqs = [j * G + q for q in range(G)]
        stack = lambda ref: jnp.concatenate([by_head(ref, i) for i in seqs], axis=0)
        r, lw, k, v, a, b = (stack(ref) for ref in (r_ref, lw_ref, k_ref, v_ref, a_ref, b_ref))
        s0 = jnp.concatenate([s_scr[i] for i in seqs], axis=0)
        hi = lw.astype(bf16)
        r1 = lw - hi.astype(f32)
        mid = r1.astype(bf16)
        lo = (r1 - mid.astype(f32)).astype(bf16)
        lc = _bmm(tri, hi) + _bmm(tri, mid) + _bmm(tri, lo)
        e_pos = jnp.exp(lc)
        e_neg = jnp.exp(-lc)
        a_t = (a * jnp.exp(lc - lw)).astype(bf16)
        r_t = (r * e_pos).astype(bf16)
        b_t = (b * e_neg).astype(bf16)
        k_t = (k * e_neg).astype(bf16)
        v_b = v.astype(bf16)
        s0_b = s0.astype(bf16)

        a_ab = jnp.where(strict, _bmm_nt(a_t, b_t), 0.0)
        a_ak = jnp.where(strict, _bmm_nt(a_t, k_t), 0.0)
        a_rb = jnp.where(incl, _bmm_nt(r_t, b_t), 0.0)
        a_rk = jnp.where(incl, _bmm_nt(r_t, k_t), 0.0)

        x = _bmm_nt(a_t, s0_b) + _bmm(a_ak.astype(bf16), v_b)
        p = a_ab
        n = 1
        while n < C:
            p_b = p.astype(bf16)
            x = x + _bmm(p_b, x.astype(bf16))
            n *= 2
            if n < C:
                p = _bmm(p_b, p_b)
        u_b = x.astype(bf16)
        o = _bmm_nt(r_t, s0_b) + _bmm(a_rb.astype(bf16), u_b) + _bmm(a_rk.astype(bf16), v_b)
        s1 = s0 + _bmm_tn(u_b, b_t) + _bmm_tn(v_b, k_t)
        s1 = s1 * e_pos[:, C - 1:C, :]
        mean = jnp.mean(o, -1, keepdims=True)
        oc = o - mean
        var = jnp.mean(oc * oc, -1, keepdims=True)
        on = oc * lax.rsqrt(var + GN_EPS)
        for q, i in enumerate(seqs):
            s_scr[i] = s1[q * H:(q + 1) * H]
            on_i = jnp.concatenate([on[q * H + h] for h in range(H)], axis=1)[:seq_rows]
            o_ref[i] = (on_i * gng_ref[...] + gnb_ref[...] + bonus_ref[i]) * g_ref[i]
        return carry

    lax.fori_loop(0, n_seq // G, one_seq, 0)

    @pl.when(c == pl.num_programs(1) - 1)
    def _():
        sT_ref[...] = s_scr[...]


def rwkv_recurrence(s0, seqs, gn_g, gn_b, *, n_batch, n_chunks, seq_rows, n_seq, chunk, group=1):
    h, n = RWKV_HEADS, RWKV_HEAD_DIM
    w = RWKV_WIDTH
    seq_spec = pl.BlockSpec((n_seq, seq_rows, w), lambda i, c: (i * n_chunks + c, 0, 0))
    st_spec = pl.BlockSpec((n_seq, h, n, n), lambda i, c: (i, 0, 0, 0))
    vec_spec = pl.BlockSpec((1, w), lambda i, c: (0, 0))
    o, s_t = pl.pallas_call(
        functools.partial(_rwkv_chunk_kernel, n_seq=n_seq, seq_rows=seq_rows, chunk=chunk, group=group),
        out_shape=(jax.ShapeDtypeStruct((n_batch * n_chunks, seq_rows, w), jnp.float32),
                   jax.ShapeDtypeStruct((n_batch, h, n, n), jnp.float32)),
        grid=(n_batch // n_seq, n_chunks),
        in_specs=[seq_spec] * 8 + [vec_spec, vec_spec, st_spec],
        out_specs=(seq_spec, st_spec),
        scratch_shapes=[pltpu.VMEM((n_seq, h, n, n), jnp.float32)],
        compiler_params=pltpu.CompilerParams(
            dimension_semantics=("arbitrary", "arbitrary"), vmem_limit_bytes=VMEM_LIMIT_BYTES),
        name="rwkv_chunk",
    )(*seqs, gn_g.reshape(1, w), gn_b.reshape(1, w), s0)
    return o, s_t


def _moe_scatter_kernel(slot_ref, h_ref, init_hbm, out_hbm, stage, sems):
    del init_hbm
    i = pl.program_id(0)
    nt = pl.num_programs(0)
    tt = h_ref.shape[0]
    slot = i % 2

    def wait_buffer(s):
        for _ in range(TOP_K):
            pltpu.make_async_copy(stage.at[s], out_hbm.at[pl.ds(0, tt)], sems.at[s]).wait()

    @pl.when(i >= 2)
    def _():
        wait_buffer(slot)

    stage[slot] = h_ref[...]
    for j in range(tt):
        for k in range(TOP_K):
            dst = slot_ref[(i * tt + j) * TOP_K + k]
            pltpu.make_async_copy(stage.at[slot, pl.ds(j, 1)], out_hbm.at[pl.ds(dst, 1)], sems.at[slot]).start()

    @pl.when(i == nt - 1)
    def _():
        wait_buffer(slot)

        @pl.when(nt >= 2)
        def _():
            wait_buffer(1 - slot)


def _moe_scatter_rows(h, pair_slot, p_pad):
    n, d = h.shape
    tt = MOE_ROUTE_TILE
    return pl.pallas_call(
        _moe_scatter_kernel,
        out_shape=jax.ShapeDtypeStruct((p_pad, d), jnp.float32),
        grid_spec=pltpu.PrefetchScalarGridSpec(
            num_scalar_prefetch=1,
            grid=(n // tt,),
            in_specs=[pl.BlockSpec((tt, d), lambda i, ps: (i, 0)),
                      pl.BlockSpec(memory_space=pl.ANY)],
            out_specs=pl.BlockSpec(memory_space=pl.ANY),
            scratch_shapes=[pltpu.VMEM((2, tt, d), jnp.float32), pltpu.SemaphoreType.DMA((2,))],
        ),
        input_output_aliases={2: 0},
        compiler_params=pltpu.CompilerParams(
            dimension_semantics=("arbitrary",), vmem_limit_bytes=VMEM_LIMIT_BYTES),
        name="moe_scatter",
    )(pair_slot.reshape(-1), h, jnp.zeros((p_pad, d), jnp.float32))


def _weight_copies(w_hbm, stage, sem, expert, col0, width, slot, n_parts, part_stride):
    return [pltpu.make_async_copy(w_hbm.at[expert, :, pl.ds(pl.multiple_of(col0 + p * part_stride, LANES), width)],
                                  stage.at[slot, p], sem.at[slot]) for p in range(n_parts)]


def _stream_expert_weights(sched_ref, w_hbm, stage, sem, w_bf, *, width, n_parts, part_stride):
    s = pl.program_id(0) * pl.num_programs(1) + pl.program_id(1)
    field = lambda j: sched_ref[s * SCHED_COLS + j]

    @pl.when(field(0) != 0)
    def _():
        slot = field(1)
        cur = _weight_copies(w_hbm, stage, sem, field(2), field(3), width, slot, n_parts, part_stride)

        @pl.when(s == 0)
        def _():
            for cp in cur:
                cp.start()

        for cp in cur:
            cp.wait()
        for p in range(n_parts):
            w_bf[p] = stage[slot, p].astype(jnp.bfloat16)

        @pl.when(field(4) != 0)
        def _():
            for cp in _weight_copies(w_hbm, stage, sem, field(5), field(6), width, 1 - slot, n_parts, part_stride):
                cp.start()


def _weight_schedule(tile_expert, n_chunks, chunk_width):
    nt = tile_expert.shape[0]
    n = n_chunks * nt
    expert = jnp.tile(tile_expert, n_chunks)
    col0 = jnp.repeat(jnp.arange(n_chunks, dtype=jnp.int32) * chunk_width, nt)
    step = jnp.arange(n, dtype=jnp.int32)
    start = (step % nt == 0) | (expert != jnp.roll(expert, 1))
    slot = (jnp.cumsum(start.astype(jnp.int32)) - 1) % 2
    nxt = lax.cummin(jnp.where(start, step, n), axis=0, reverse=True)
    nxt = jnp.concatenate([nxt[1:], jnp.full((1,), n, jnp.int32)])
    has_next = nxt < n
    nxt_c = jnp.minimum(nxt, n - 1)
    return jnp.stack([start.astype(jnp.int32), slot, expert, col0, has_next.astype(jnp.int32),
                      expert[nxt_c], col0[nxt_c]], axis=1).astype(jnp.int32).reshape(-1)


def _moe_up_kernel(sched_ref, tile_valid_ref, x_ref, bg_ref, bu_ref, w_hbm, h_ref, stage, sem, w_bf):
    f = w_hbm.shape[2] // 2
    _stream_expert_weights(sched_ref, w_hbm, stage, sem, w_bf, width=h_ref.shape[1], n_parts=2, part_stride=f)
    i = pl.program_id(1)

    @pl.when(tile_valid_ref[i] != 0)
    def _():
        x = x_ref[...].astype(jnp.bfloat16)
        hg = jnp.dot(x, w_bf[0], preferred_element_type=jnp.float32) + bg_ref[0]
        hu = jnp.dot(x, w_bf[1], preferred_element_type=jnp.float32) + bu_ref[0]
        glu = jnp.minimum(hg, SWIGLU_LIMIT)
        up = jnp.clip(hu, -SWIGLU_LIMIT, SWIGLU_LIMIT)
        act = (up + 1.0) * glu * jax.nn.sigmoid(SWIGLU_ALPHA * glu)
        h_ref[...] = act.astype(h_ref.dtype)

    @pl.when(tile_valid_ref[i] == 0)
    def _():
        h_ref[...] = jnp.zeros_like(h_ref)


def _moe_down_kernel(sched_ref, tile_valid_ref, h_ref, bd_ref, w_hbm, y_ref, stage, sem, w_bf):
    _stream_expert_weights(sched_ref, w_hbm, stage, sem, w_bf, width=y_ref.shape[1], n_parts=1, part_stride=0)
    i = pl.program_id(1)

    @pl.when(tile_valid_ref[i] != 0)
    def _():
        y_ref[...] = jnp.dot(h_ref[...], w_bf[0], preferred_element_type=jnp.float32) + bd_ref[0]

    @pl.when(tile_valid_ref[i] == 0)
    def _():
        y_ref[...] = jnp.zeros_like(y_ref)


def _moe_grouped(x_sorted, tile_expert, tile_valid, w_gate_up, b_gate_up, w_down, b_down):
    p_pad, d = x_sorted.shape
    n_exp, _, two_f = w_gate_up.shape
    f = two_f // 2
    tm, fc, dc = MOE_TILE_M, MOE_F_CHUNK, MOE_D_CHUNK
    nt = p_pad // tm
    nfc = f // fc
    ndc = d // dc
    bgu = b_gate_up.reshape(n_exp, 1, two_f)
    bdn = b_down.reshape(n_exp, 1, d)
    params = pltpu.CompilerParams(dimension_semantics=("arbitrary", "arbitrary"),
                                  vmem_limit_bytes=MOE_VMEM_LIMIT_BYTES)
    te_of = lambda sched, c, i: sched[(c * nt + i) * SCHED_COLS + 2]

    h = pl.pallas_call(
        _moe_up_kernel,
        out_shape=jax.ShapeDtypeStruct((p_pad, f), jnp.bfloat16),
        grid_spec=pltpu.PrefetchScalarGridSpec(
            num_scalar_prefetch=2,
            grid=(nfc, nt),
            in_specs=[
                pl.BlockSpec((tm, d), lambda c, i, sc, tv: (i, 0)),
                pl.BlockSpec((1, 1, fc), lambda c, i, sc, tv: (te_of(sc, c, i), 0, c)),
                pl.BlockSpec((1, 1, fc), lambda c, i, sc, tv: (te_of(sc, c, i), 0, nfc + c)),
                pl.BlockSpec(memory_space=pl.ANY),
            ],
            out_specs=pl.BlockSpec((tm, fc), lambda c, i, sc, tv: (i, c)),
            scratch_shapes=[pltpu.VMEM((2, 2, d, fc), jnp.float32), pltpu.SemaphoreType.DMA((2,)),
                            pltpu.VMEM((2, d, fc), jnp.bfloat16)],
        ),
        compiler_params=params,
        name="moe_up",
    )(_weight_schedule(tile_expert, nfc, fc), tile_valid, x_sorted, bgu, bgu, w_gate_up)

    return pl.pallas_call(
        _moe_down_kernel,
        out_shape=jax.ShapeDtypeStruct((p_pad, d), jnp.float32),
        grid_spec=pltpu.PrefetchScalarGridSpec(
            num_scalar_prefetch=2,
            grid=(ndc, nt),
            in_specs=[
                pl.BlockSpec((tm, f), lambda c, i, sc, tv: (i, 0)),
                pl.BlockSpec((1, 1, dc), lambda c, i, sc, tv: (te_of(sc, c, i), 0, c)),
                pl.BlockSpec(memory_space=pl.ANY),
            ],
            out_specs=pl.BlockSpec((tm, dc), lambda c, i, sc, tv: (i, c)),
            scratch_shapes=[pltpu.VMEM((2, 1, f, dc), jnp.float32), pltpu.SemaphoreType.DMA((2,)),
                            pltpu.VMEM((1, f, dc), jnp.bfloat16)],
        ),
        compiler_params=params,
        name="moe_down",
    )(_weight_schedule(tile_expert, ndc, dc), tile_valid, h, bdn, w_down)


def _moe_combine_kernel(slot_ref, h_ref, w_ref, g_ref, b_ref, y_hbm, out_ref, buf, sems):
    i = pl.program_id(0)
    nt = pl.num_programs(0)
    tt = h_ref.shape[0]

    def start_tile(t, s):
        for j in range(tt):
            for k in range(TOP_K):
                src = slot_ref[(t * tt + j) * TOP_K + k]
                pltpu.make_async_copy(y_hbm.at[pl.ds(src, 1)], buf.at[s, pl.ds(k * tt + j, 1)], sems.at[s]).start()

    @pl.when(i == 0)
    def _():
        start_tile(0, 0)

    @pl.when(i + 1 < nt)
    def _():
        start_tile(i + 1, (i + 1) % 2)

    slot = i % 2
    pltpu.make_async_copy(y_hbm.at[pl.ds(0, TOP_K * tt)], buf.at[slot], sems.at[slot]).wait()
    w = w_ref[...]
    ffn = buf[slot, 0:tt] * w[:, 0:1]
    for k in range(1, TOP_K):
        ffn = ffn + buf[slot, k * tt:(k + 1) * tt] * w[:, k:k + 1]
    x = DEEPNORM_ALPHA * h_ref[...] + ffn
    mu = jnp.mean(x, -1, keepdims=True)
    xc = x - mu
    var = jnp.mean(xc * xc, -1, keepdims=True)
    out_ref[...] = xc * lax.rsqrt(var + LN_EPS) * g_ref[...] + b_ref[...]


def _moe_combine_ln(h, y, pair_slot, top_w, ln_g, ln_b):
    n, d = h.shape
    tt = MOE_ROUTE_TILE
    return pl.pallas_call(
        _moe_combine_kernel,
        out_shape=jax.ShapeDtypeStruct((n, d), jnp.float32),
        grid_spec=pltpu.PrefetchScalarGridSpec(
            num_scalar_prefetch=1,
            grid=(n // tt,),
            in_specs=[pl.BlockSpec((tt, d), lambda i, ps: (i, 0)),
                      pl.BlockSpec((tt, TOP_K), lambda i, ps: (i, 0)),
                      pl.BlockSpec((1, d), lambda i, ps: (0, 0)),
                      pl.BlockSpec((1, d), lambda i, ps: (0, 0)),
                      pl.BlockSpec(memory_space=pl.ANY)],
            out_specs=pl.BlockSpec((tt, d), lambda i, ps: (i, 0)),
            scratch_shapes=[pltpu.VMEM((2, TOP_K * tt, d), jnp.float32), pltpu.SemaphoreType.DMA((2,))],
        ),
        compiler_params=pltpu.CompilerParams(
            dimension_semantics=("arbitrary",), vmem_limit_bytes=VMEM_LIMIT_BYTES),
        name="moe_combine",
    )(pair_slot.reshape(-1), h, top_w, ln_g.reshape(1, d), ln_b.reshape(1, d), y)


def moe_ffn_ln(x, top_w, top_idx, w_gate_up, b_gate_up, w_down, b_down, ln_g, ln_b):
    n, d = x.shape
    tm = MOE_TILE_M
    nt = n * TOP_K // tm + N_EXPERTS
    onehot = (top_idx[:, :, None] == jnp.arange(N_EXPERTS, dtype=top_idx.dtype)).astype(jnp.int32)
    routed = jnp.sum(onehot, axis=1)
    before = jnp.cumsum(routed, axis=0) - routed
    counts = before[-1] + routed[-1]
    tiles_per = (counts + tm - 1) // tm
    tile_end = jnp.cumsum(tiles_per)
    first_row = (tile_end - tiles_per) * tm
    pair_slot = jnp.sum(onehot * (before + first_row)[:, None, :], axis=2)
    tile_ids = jnp.arange(nt, dtype=jnp.int32)
    tile_expert = jnp.minimum(jnp.sum((tile_ids[:, None] >= tile_end[None, :]).astype(jnp.int32), axis=1),
                              N_EXPERTS - 1).astype(jnp.int32)
    tile_valid = (tile_ids < tile_end[-1]).astype(jnp.int32)

    x_sorted = _moe_scatter_rows(x, pair_slot, nt * tm)
    y = _moe_grouped(x_sorted, tile_expert, tile_valid, w_gate_up, b_gate_up, w_down, b_down)
    return _moe_combine_ln(x, y, pair_slot, top_w, ln_g, ln_b)


def kernel(x_prompt, x_sample, cache_win0_kv, cache_win1_kv, cache_win2_kv, cache_mem_kv,
           state_wkv, state_shift, mem_prompt,
           w_in, w_gate, b_gate, w_branch, w_out, w_mem_kv,
           mu_rkv, mu_wag, w0, w1, w2, a0, a1, a2, g1, g2, k_k, k_a, r_k, gn_g, gn_b,
           ln1_g, ln1_b, w_router, b_router, w_gate_up, b_gate_up, w_down, b_down, ln2_g, ln2_b):
    bf16 = jnp.bfloat16
    xp, xs = x_prompt, x_sample
    bp, tp, d = xp.shape
    bs, ts, _ = xs.shape
    n_prompt = bp * tp
    l = 0
    lp = {"mu_rkv": mu_rkv[l], "mu_wag": mu_wag[l],
          "w0": w0[l], "w1": w1[l], "w2": w2[l], "a0": a0[l], "a1": a1[l], "a2": a2[l],
          "g1": g1[l], "g2": g2[l], "k_k": k_k[l], "k_a": k_a[l], "r_k": r_k[l]}

    xp2, xs2 = xp.reshape(-1, d), xs.reshape(-1, d)
    u_all = jnp.concatenate([xp2, xs2], axis=0)
    w_in_bf = w_in[l].astype(bf16)
    proj_p2 = matmul_bias(xp2.astype(bf16), w_in_bf, tile_m=PROJ_PROMPT_TILE_M, name="proj_in")
    proj_s2 = matmul_bias(xs2.astype(bf16), w_in_bf, tile_m=xs2.shape[0], name="proj_in_step")
    gate = matmul_bias(u_all.astype(bf16), w_gate[l].astype(bf16), b_gate[l], tile_m=PROJ_TILE_M, sigmoid=True,
                       name="proj_gate")
    mem_rows = mem_prompt.reshape(-1, d)
    mem_kv_p = matmul_bias(mem_rows.astype(bf16), w_mem_kv[l].astype(bf16), tile_m=mem_rows.shape[0], name="proj_mem")
    proj_p = proj_p2.reshape(bp, tp, PROJ_WIDTH)
    proj_s = proj_s2.reshape(bs, ts, PROJ_WIDTH)
    rkv0, qm0 = 3 * ATT_WIDTH, 3 * ATT_WIDTH + 3 * RWKV_WIDTH

    heads = lambda t: t.reshape(t.shape[0], t.shape[1], N_ATT_HEADS, HEAD_DIM)
    qa_p, ka_p, va_p = (heads(proj_p[..., i * ATT_WIDTH:(i + 1) * ATT_WIDTH]) for i in range(3))
    ka_s, va_s = (heads(proj_s[..., i * ATT_WIDTH:(i + 1) * ATT_WIDTH]) for i in (1, 2))
    outs, lses, win_p, win_s = [], [], [], []
    for g, (win, dil) in enumerate(DIL_GROUPS):
        hs = slice(g * HEADS_PER_GROUP, (g + 1) * HEADS_PER_GROUP)
        o, lse = dilated_attention_prompt(qa_p[:, :, hs], ka_p[:, :, hs], va_p[:, :, hs], g, dil)
        outs.append(o.reshape(n_prompt, ATT_OUT))
        lses.append(lse.reshape(n_prompt, ATT_OUT))
        keep = min(win, tp)
        win_p.append(jnp.stack([ka_p[:, tp - keep:, hs], va_p[:, tp - keep:, hs]], axis=2))
        win_s.append(jnp.stack([ka_s[:, :, hs], va_s[:, :, hs]], axis=2))
    o_att_p = att_combine(outs, lses)
    o_att_s = dilated_attention_step(proj_s, cache_win0_kv[l], cache_win1_kv[l], cache_win2_kv[l])

    tq = MEM_ATTN_TILE_Q
    o_mem_p = memory_attention(proj_p2.reshape(-1, tq, PROJ_WIDTH), qm0 // MEM_WIDTH,
                               mem_kv_p.reshape(bp, MEM_TOKENS, 2 * MEM_WIDTH), tp // tq)
    o_mem_s = memory_attention(proj_s, qm0 // MEM_WIDTH, cache_mem_kv[l], 1, n_seq=MEM_ATTN_STEP_SEQS)

    tm = RWKV_PRE_TILE_M

    def tile_prev_rows(x):
        last = x[:, tm - 1::tm][:, :-1]
        return jnp.concatenate([jnp.zeros_like(last[:, :1]), last], axis=1).reshape(-1, 1, x.shape[-1])

    rwkv_in_p = rwkv_pre(xp2, tile_prev_rows(xp), proj_p2, tile_prev_rows(proj_p[..., rkv0:qm0]), lp, rolled=True)
    u_last_s = state_shift[l].astype(xs.dtype)
    p_last_s = matmul_bias(u_last_s.astype(bf16), w_in_bf[:, rkv0:qm0], tile_m=bs, name="proj_shift")
    shift = lambda first, rest: jnp.concatenate([first[:, None], rest[:, :-1]], axis=1).reshape(-1, rest.shape[-1])
    rwkv_in_s = rwkv_pre(xs2, shift(u_last_s, xs), proj_s2, shift(p_last_s, proj_s[..., rkv0:qm0]), lp, rolled=False)
    s0 = jnp.zeros((bp, RWKV_HEADS, RWKV_HEAD_DIM, RWKV_HEAD_DIM), jnp.float32)
    o_rwkv_p, wkv_p = rwkv_recurrence(
        s0, [x.reshape(-1, RWKV_CHUNK, RWKV_WIDTH) for x in rwkv_in_p], gn_g[l], gn_b[l],
        n_batch=bp, n_chunks=tp // RWKV_CHUNK, seq_rows=RWKV_CHUNK, n_seq=1, chunk=RWKV_CHUNK)
    o_rwkv_s, wkv_s = rwkv_recurrence(
        state_wkv[l].astype(jnp.float32), [x.reshape(bs, ts, RWKV_WIDTH) for x in rwkv_in_s], gn_g[l], gn_b[l],
        n_batch=bs, n_chunks=1, seq_rows=ts, n_seq=RWKV_SEQ_PER_STEP, chunk=-(-ts // SUBLANES) * SUBLANES,
        group=RWKV_SEQ_GROUP)

    cat = lambda a, b, w: jnp.concatenate([a.reshape(-1, w), b.reshape(-1, w)], axis=0)
    h, top_w, top_idx = merge_ln_router(
        cat(o_att_p, o_att_s, ATT_OUT), cat(o_rwkv_p, o_rwkv_s, RWKV_WIDTH), cat(o_mem_p, o_mem_s, MEM_WIDTH),
        gate, u_all, w_branch[l].astype(bf16), w_out[l].astype(bf16), ln1_g[l], ln1_b[l], w_router[l], b_router[l])
    y_all = moe_ffn_ln(h, top_w, top_idx, w_gate_up[l], b_gate_up[l], w_down[l], b_down[l], ln2_g[l], ln2_b[l])
    yp = y_all[:n_prompt].reshape(bp, tp, d)
    ys = y_all[n_prompt:].reshape(bs, ts, d)
    mem_kv_out = mem_kv_p.reshape(bp, MEM_TOKENS, 2, MEM_HEADS, MEM_HEAD_DIM)
    return (yp, ys,
            win_p[0][None], win_p[1][None], win_p[2][None], wkv_p.astype(xp.dtype)[None],
            xp[:, -1][None], mem_kv_out[None],
            win_s[0].astype(cache_win0_kv.dtype)[None], win_s[1].astype(cache_win0_kv.dtype)[None],
            win_s[2].astype(cache_win0_kv.dtype)[None], wkv_s.astype(state_wkv.dtype)[None],
            xs[:, -1].astype(state_shift.dtype)[None])
```

```python
import functools

import jax
import jax.numpy as jnp
from jax import lax
from jax.experimental import pallas as pl
from jax.experimental.pallas import tpu as pltpu

D_MODEL = 2048
DEPTH = 1
HEAD_DIM = 64
DIL_GROUPS = ((128, 1), (512, 4), (2048, 16))
HEADS_PER_GROUP = 4
N_ATT_HEADS = HEADS_PER_GROUP * len(DIL_GROUPS)
ATT_WIDTH = N_ATT_HEADS * HEAD_DIM
ATT_OUT = HEADS_PER_GROUP * HEAD_DIM
BAND_BLOCK = 128
N_BACK = 128
GROUP_WIDTH = HEADS_PER_GROUP * HEAD_DIM
RWKV_HEADS = 12
RWKV_HEAD_DIM = 64
RWKV_WIDTH = RWKV_HEADS * RWKV_HEAD_DIM
GN_EPS = 64e-5
MEM_TOKENS = 256
MEM_HEADS = 4
MEM_HEAD_DIM = 128
MEM_WIDTH = MEM_HEADS * MEM_HEAD_DIM
PROJ_WIDTH = 3 * ATT_WIDTH + 3 * RWKV_WIDTH + MEM_WIDTH
N_BRANCHES = 3
N_EXPERTS = 32
TOP_K = 4
D_EXPERT = 2048
SWIGLU_LIMIT = 7.0
SWIGLU_ALPHA = 1.702
LN_EPS = 1e-5
DEEPNORM_ALPHA = (2.0 * DEPTH) ** 0.25
NEG_INF = -1e30

VMEM_LIMIT_BYTES = 56 * 1024 * 1024
SUBLANES = 8
LANES = 128
MOE_TILE_M = 256
MOE_F_CHUNK = 1024
MOE_D_CHUNK = 1024
MOE_ROUTE_TILE = 64
MOE_VMEM_LIMIT_BYTES = 58 * 1024 * 1024
SCHED_COLS = 7
RWKV_CHUNK = 64
RWKV_SEQ_PER_STEP = 8
RWKV_SEQ_GROUP = 4
RWKV_PRE_TILE_M = 256
PROJ_TILE_M = 1088
PROJ_PROMPT_TILE_M = 1024
MEM_ATTN_STEP_SEQS = 8
MATMUL_TILE_N = 512
MERGE_TILE_M = 256
ATT_COMBINE_TILE = 512
MEM_ATTN_TILE_Q = 512
ATTN_STEP_SEQS = 2


def _matmul_kernel(x_ref, w_ref, b_ref, o_ref, *, sigmoid):
    y = jnp.dot(x_ref[...], w_ref[...], preferred_element_type=jnp.float32) + b_ref[...]
    o_ref[...] = jax.nn.sigmoid(y) if sigmoid else y


def matmul_bias(x, w, bias=None, *, tile_m, sigmoid=False, name="matmul"):
    m, k = x.shape
    n = w.shape[1]
    tn = MATMUL_TILE_N if n % MATMUL_TILE_N == 0 else n
    if bias is None:
        bias = jnp.zeros((n,), jnp.float32)
    return pl.pallas_call(
        functools.partial(_matmul_kernel, sigmoid=sigmoid),
        out_shape=jax.ShapeDtypeStruct((m, n), jnp.float32),
        grid=(n // tn, m // tile_m),
        in_specs=[pl.BlockSpec((tile_m, k), lambda j, i: (i, 0)),
                  pl.BlockSpec((k, tn), lambda j, i: (0, j)),
                  pl.BlockSpec((1, tn), lambda j, i: (0, j))],
        out_specs=pl.BlockSpec((tile_m, tn), lambda j, i: (i, j)),
        compiler_params=pltpu.CompilerParams(
            dimension_semantics=("arbitrary", "arbitrary"), vmem_limit_bytes=VMEM_LIMIT_BYTES),
        name=name,
    )(x, w, bias.reshape(1, n))


def _merge_kernel(att_ref, rwkv_ref, mem_ref, gate_ref, x_ref, wb_ref, wo_ref, g_ref, b_ref,
                  wr_hi_ref, wr_lo_ref, br_ref, h_ref, topw_ref, topi_ref):
    bf16, f32 = jnp.bfloat16, jnp.float32
    d = x_ref.shape[1]
    dot = functools.partial(jnp.dot, preferred_element_type=f32)
    y = dot(att_ref[...].astype(bf16), wb_ref[0:ATT_OUT]) * gate_ref[:, 0:d]
    y = y + dot(rwkv_ref[...].astype(bf16), wb_ref[ATT_OUT:ATT_OUT + RWKV_WIDTH]) * gate_ref[:, d:2 * d]
    y = y + dot(mem_ref[...].astype(bf16), wb_ref[ATT_OUT + RWKV_WIDTH:]) * gate_ref[:, 2 * d:]
    z = DEEPNORM_ALPHA * x_ref[...] + dot(y.astype(bf16), wo_ref[...])
    mu = jnp.mean(z, -1, keepdims=True)
    zc = z - mu
    var = jnp.mean(zc * zc, -1, keepdims=True)
    h = zc * lax.rsqrt(var + LN_EPS) * g_ref[...] + b_ref[...]
    h_ref[...] = h

    h_hi = h.astype(bf16)
    h_lo = (h - h_hi.astype(f32)).astype(bf16)
    logits = (dot(h_hi, wr_hi_ref[...]) + dot(h_hi, wr_lo_ref[...]) + dot(h_lo, wr_hi_ref[...])) + br_ref[...]
    lane = lax.broadcasted_iota(jnp.int32, logits.shape, 1)
    work = jnp.where(lane < N_EXPERTS, logits, -jnp.inf)
    tops, ids = [], []
    for _ in range(TOP_K):
        m = jnp.max(work, -1, keepdims=True)
        idx = jnp.min(jnp.where(work == m, lane, LANES), -1, keepdims=True)
        tops.append(m)
        ids.append(idx)
        work = jnp.where(lane == idx, -jnp.inf, work)
    e = [jnp.exp(t - tops[0]) for t in tops]
    den = e[0] + e[1] + e[2] + e[3]
    tw = jnp.zeros(logits.shape, f32)
    ti = jnp.zeros(logits.shape, jnp.int32)
    for k in range(TOP_K):
        tw = jnp.where(lane == k, e[k] / den, tw)
        ti = jnp.where(lane == k, ids[k], ti)
    topw_ref[...] = tw
    topi_ref[...] = ti


def merge_ln_router(o_att, o_rwkv, o_mem, gate, x, w_branch, w_out, ln_g, ln_b, w_router, b_router):
    n, d = x.shape
    tm = MERGE_TILE_M
    wr = jnp.zeros((d, LANES), jnp.float32).at[:, :N_EXPERTS].set(w_router)
    wr_hi = wr.astype(jnp.bfloat16)
    wr_lo = (wr - wr_hi.astype(jnp.float32)).astype(jnp.bfloat16)
    br = jnp.zeros((1, LANES), jnp.float32).at[0, :N_EXPERTS].set(b_router)
    rows = lambda w: pl.BlockSpec((tm, w), lambda i: (i, 0))
    full = lambda a: pl.BlockSpec(a.shape, lambda i: (0, 0))
    h, topw, topi = pl.pallas_call(
        _merge_kernel,
        out_shape=(jax.ShapeDtypeStruct((n, d), jnp.float32),
                   jax.ShapeDtypeStruct((n, LANES), jnp.float32),
                   jax.ShapeDtypeStruct((n, LANES), jnp.int32)),
        grid=(n // tm,),
        in_specs=[rows(ATT_OUT), rows(RWKV_WIDTH), rows(MEM_WIDTH), rows(3 * d), rows(d),
                  full(w_branch), full(w_out), pl.BlockSpec((1, d), lambda i: (0, 0)),
                  pl.BlockSpec((1, d), lambda i: (0, 0)), full(wr_hi), full(wr_lo), full(br)],
        out_specs=(rows(d), rows(LANES), rows(LANES)),
        compiler_params=pltpu.CompilerParams(
            dimension_semantics=("arbitrary",), vmem_limit_bytes=VMEM_LIMIT_BYTES),
        name="merge_ln_router",
    )(o_att, o_rwkv, o_mem, gate, x, w_branch, w_out, ln_g.reshape(1, d), ln_b.reshape(1, d), wr_hi, wr_lo, br)
    return h, topw[:, :TOP_K], topi[:, :TOP_K]


def _alibi_slope(head):
    return 2.0 ** (-8.0 * (head + 1) / N_ATT_HEADS)


def _dot_nt(a, b):
    return lax.dot_general(a, b, (((1,), (1,)), ((), ())), preferred_element_type=jnp.float32)


def _dil_attn_prompt_kernel(q_ref, kc_ref, kp_ref, vc_ref, vp_ref, o_ref, lse_ref, *, group, dil):
    bf16 = jnp.bfloat16
    blk = BAND_BLOCK
    i = pl.program_id(1)
    qi = lax.broadcasted_iota(jnp.int32, (blk, blk), 0)
    kj = lax.broadcasted_iota(jnp.int32, (blk, blk), 1)
    dist_c = qi - kj
    dist_p = dist_c + blk
    valid_c = dist_c >= 0
    valid_p = (dist_p <= blk) & (i > 0)
    scale = HEAD_DIM ** -0.5
    heads = range(HEADS_PER_GROUP)
    head = lambda ref, h: ref[0, :, h * HEAD_DIM:(h + 1) * HEAD_DIM].astype(bf16)
    st = []
    for h in heads:
        slope = _alibi_slope(group * HEADS_PER_GROUP + h) * dil
        q = head(q_ref, h)
        s_c = _dot_nt(q, head(kc_ref, h)) * scale - slope * dist_c.astype(jnp.float32)
        s_p = _dot_nt(q, head(kp_ref, h)) * scale - slope * dist_p.astype(jnp.float32)
        st.append((jnp.where(valid_c, s_c, NEG_INF), jnp.where(valid_p, s_p, NEG_INF)))
    st = [(s_c, s_p, jnp.maximum(jnp.max(s_c, -1, keepdims=True), jnp.max(s_p, -1, keepdims=True))) for s_c, s_p in st]
    st = [(jnp.exp(s_c - m), jnp.exp(s_p - m), m) for s_c, s_p, m in st]
    st = [(p_c, p_p, m, jnp.sum(p_c, -1, keepdims=True) + jnp.sum(p_p, -1, keepdims=True)) for p_c, p_p, m in st]
    for h in heads:
        p_c, p_p, m, den = st[h]
        pv = (jnp.dot(p_c.astype(bf16), head(vc_ref, h), preferred_element_type=jnp.float32)
              + jnp.dot(p_p.astype(bf16), head(vp_ref, h), preferred_element_type=jnp.float32))
        o_ref[0, :, h * HEAD_DIM:(h + 1) * HEAD_DIM] = pv / den
        lse_ref[0, :, h * HEAD_DIM:(h + 1) * HEAD_DIM] = jnp.broadcast_to(m + jnp.log(den), (blk, HEAD_DIM))


def dilated_attention_prompt(proj, b, t, group, dil):
    l = t // dil
    assert l % BAND_BLOCK == 0 and PROJ_WIDTH % GROUP_WIDTH == 0
    cb = PROJ_WIDTH // GROUP_WIDTH
    kq, kk, kv = group, ATT_WIDTH // GROUP_WIDTH + group, 2 * ATT_WIDTH // GROUP_WIDTH + group
    view = proj.reshape(b, l, dil * PROJ_WIDTH)
    blk = (1, BAND_BLOCK, GROUP_WIDTH)
    cur = lambda col: pl.BlockSpec(blk, lambda n, i: (n // dil, i, (n % dil) * cb + col))
    prev = lambda col: pl.BlockSpec(blk, lambda n, i: (n // dil, jnp.maximum(i - 1, 0), (n % dil) * cb + col))
    out_spec = pl.BlockSpec(blk, lambda n, i: (n // dil, i, n % dil))
    out_shape = jax.ShapeDtypeStruct((b, l, dil * GROUP_WIDTH), jnp.float32)
    o, lse = pl.pallas_call(
        functools.partial(_dil_attn_prompt_kernel, group=group, dil=dil),
        out_shape=(out_shape, out_shape),
        grid=(b * dil, l // BAND_BLOCK),
        in_specs=[cur(kq), cur(kk), prev(kk), cur(kv), prev(kv)],
        out_specs=(out_spec, out_spec),
        compiler_params=pltpu.CompilerParams(
            dimension_semantics=("arbitrary", "arbitrary"), vmem_limit_bytes=VMEM_LIMIT_BYTES),
        name="dil_attn_prompt_g%d" % group,
    )(view, view, view, view, view)
    return o.reshape(b * t, GROUP_WIDTH), lse.reshape(b * t, GROUP_WIDTH)


def _att_combine_kernel(o0, o1, o2, l0, l1, l2, out_ref):
    a, b, c = l0[...], l1[...], l2[...]
    m = jnp.maximum(jnp.maximum(a, b), c)
    ea, eb, ec = jnp.exp(a - m), jnp.exp(b - m), jnp.exp(c - m)
    out_ref[...] = (ea * o0[...] + eb * o1[...] + ec * o2[...]) / (ea + eb + ec)


def att_combine(outs, lses):
    n, w = outs[0].shape
    spec = pl.BlockSpec((ATT_COMBINE_TILE, w), lambda i: (i, 0))
    return pl.pallas_call(
        _att_combine_kernel,
        out_shape=jax.ShapeDtypeStruct((n, w), jnp.float32),
        grid=(n // ATT_COMBINE_TILE,),
        in_specs=[spec] * 6,
        out_specs=spec,
        compiler_params=pltpu.CompilerParams(dimension_semantics=("arbitrary",)),
        name="att_combine",
    )(*outs, *lses)


def _mem_attn_kernel(q_ref, kv_ref, o_ref):
    bf16 = jnp.bfloat16
    e = MEM_HEAD_DIM
    scale = e ** -0.5
    units = [(n, h) for n in range(q_ref.shape[0]) for h in range(MEM_HEADS)]
    if len(kv_ref.shape) == 3:
        key = lambda n, h: kv_ref[n, :, h * e:(h + 1) * e]
        val = lambda n, h: kv_ref[n, :, (MEM_HEADS + h) * e:(MEM_HEADS + h + 1) * e]
    else:
        key = lambda n, h: kv_ref[n, :, 0, h, :]
        val = lambda n, h: kv_ref[n, :, 1, h, :]
    s = [_dot_nt(q_ref[n, :, h * e:(h + 1) * e].astype(bf16), key(n, h).astype(bf16)) * scale for n, h in units]
    m = [jnp.max(x, -1, keepdims=True) for x in s]
    p = [jnp.exp(x - mx) for x, mx in zip(s, m)]
    p = [x / jnp.sum(x, -1, keepdims=True) for x in p]
    for (n, h), x in zip(units, p):
        o_ref[n, :, h * e:(h + 1) * e] = jnp.dot(x.astype(bf16), val(n, h).astype(bf16),
                                                 preferred_element_type=jnp.float32)


def memory_attention(q, q_col_block, mem_kv, blocks_per_seq, n_seq=1):
    b = mem_kv.shape[0]
    tq = q.shape[1]
    w = MEM_WIDTH
    assert n_seq == 1 or blocks_per_seq == 1
    kv_zeros = (0,) * (mem_kv.ndim - 1)
    return pl.pallas_call(
        _mem_attn_kernel,
        out_shape=jax.ShapeDtypeStruct((b * blocks_per_seq, tq, w), jnp.float32),
        grid=(b // n_seq, blocks_per_seq),
        in_specs=[pl.BlockSpec((n_seq, tq, w), lambda n, i: (n * blocks_per_seq + i, 0, q_col_block)),
                  pl.BlockSpec((n_seq,) + mem_kv.shape[1:], lambda n, i: (n,) + kv_zeros)],
        out_specs=pl.BlockSpec((n_seq, tq, w), lambda n, i: (n * blocks_per_seq + i, 0, 0)),
        compiler_params=pltpu.CompilerParams(
            dimension_semantics=("arbitrary", "arbitrary"), vmem_limit_bytes=VMEM_LIMIT_BYTES),
        name="mem_attn",
    )(q, mem_kv)


def _dil_attn_step_kernel(qkv_ref, c0_ref, c1_ref, c2_ref, o_ref, *, n_new):
    bf16, f32 = jnp.bfloat16, jnp.float32
    scale = HEAD_DIM ** -0.5
    caches = (c0_ref, c1_ref, c2_ref)
    n_groups = len(DIL_GROUPS)
    tq = lax.broadcasted_iota(jnp.int32, (n_new, n_new), 0)
    tk = lax.broadcasted_iota(jnp.int32, (n_new, n_new), 1)
    slots = [(b, h) for b in range(qkv_ref.shape[0]) for h in range(HEADS_PER_GROUP)]
    units = [(b, h, g) for b, h in slots for g in range(n_groups)]
    new_cols = lambda b, part, head: qkv_ref[b, :, part * ATT_WIDTH + head * HEAD_DIM:
                                             part * ATT_WIDTH + (head + 1) * HEAD_DIM].astype(bf16)
    st = {}
    for b, h, g in units:
        win, dil = DIL_GROUPS[g]
        head = g * HEADS_PER_GROUP + h
        slope = _alibi_slope(head)
        q = new_cols(b, 0, head)
        s = jnp.dot(q, caches[g][b, 0, h].astype(bf16), preferred_element_type=f32) * scale
        dist = (win + lax.broadcasted_iota(jnp.int32, (n_new, win), 0)
                - lax.broadcasted_iota(jnp.int32, (n_new, win), 1))
        valid = ((dist & (dil - 1)) == 0) & (dist <= N_BACK * dil)
        s = jnp.where(valid, s - slope * dist.astype(f32), NEG_INF)
        sn = _dot_nt(q, new_cols(b, 1, head)) * scale
        dn = tq - tk
        sn = jnp.where((dn >= 0) & ((dn & (dil - 1)) == 0), sn - slope * dn.astype(f32), NEG_INF)
        st[b, h, g] = (s, sn)
    for u in units:
        s, sn = st[u]
        st[u] = (s, sn, jnp.maximum(jnp.max(s, axis=1, keepdims=True), jnp.max(sn, axis=1, keepdims=True)))
    for u in units:
        s, sn, m = st[u]
        p = jnp.exp(s - m)
        pn = jnp.exp(sn - m)
        st[u] = (p, pn, m, jnp.sum(p, axis=1, keepdims=True) + jnp.sum(pn, axis=1, keepdims=True))
    for b, h, g in units:
        p, pn, m, den = st[b, h, g]
        head = g * HEADS_PER_GROUP + h
        pv = (_dot_nt(p.astype(bf16), caches[g][b, 1, h].astype(bf16))
              + jnp.dot(pn.astype(bf16), new_cols(b, 2, head), preferred_element_type=f32))
        st[b, h, g] = (pv / den, m + jnp.log(den))
    for b, h in slots:
        outs = [st[b, h, g][0] for g in range(n_groups)]
        lses = [st[b, h, g][1] for g in range(n_groups)]
        mm = jnp.maximum(jnp.maximum(lses[0], lses[1]), lses[2])
        w = [jnp.exp(l - mm) for l in lses]
        o_ref[b, :, h * HEAD_DIM:(h + 1) * HEAD_DIM] = (
            (w[0] * outs[0] + w[1] * outs[1] + w[2] * outs[2]) / (w[0] + w[1] + w[2]))


def dilated_attention_step(p_s, cache0, cache1, cache2):
    b, t, _ = p_s.shape
    feat_major = lambda c: jnp.transpose(c, (0, 2, 3, 4, 1))
    ns = ATTN_STEP_SEQS
    win_spec = lambda w: pl.BlockSpec((ns, 2, HEADS_PER_GROUP, HEAD_DIM, w), lambda n: (n, 0, 0, 0, 0))
    return pl.pallas_call(
        functools.partial(_dil_attn_step_kernel, n_new=t),
        out_shape=jax.ShapeDtypeStruct((b, t, GROUP_WIDTH), jnp.float32),
        grid=(b // ns,),
        in_specs=[pl.BlockSpec((ns, t, 3 * ATT_WIDTH), lambda n: (n, 0, 0))] + [win_spec(w) for w, _ in DIL_GROUPS],
        out_specs=pl.BlockSpec((ns, t, GROUP_WIDTH), lambda n: (n, 0, 0)),
        compiler_params=pltpu.CompilerParams(
            dimension_semantics=("arbitrary",), vmem_limit_bytes=VMEM_LIMIT_BYTES),
        name="dil_attn_step",
    )(p_s, feat_major(cache0), feat_major(cache1), feat_major(cache2))


def _rwkv_pre_kernel(u_ref, up_ref, p_ref, pp_ref, mu_rkv_ref, mu_wag_ref, w0_ref, w1_ref, w2_ref,
                     a0_ref, a1_ref, a2_ref, g1_ref, g2_ref, kk_ref, ka_ref, rk_ref,
                     r_ref, lw_ref, k_ref, v_ref, an_ref, b_ref, g_ref, bonus_ref, *, rolled):
    bf16, f32 = jnp.bfloat16, jnp.float32
    dot = functools.partial(jnp.dot, preferred_element_type=f32)
    w = RWKV_WIDTH
    u = u_ref[...]
    p = p_ref[...]
    if rolled:
        first = lax.broadcasted_iota(jnp.int32, (u.shape[0], 1), 0) == 0
        u_prev = jnp.where(first, up_ref[0], pltpu.roll(u, 1, 0))
        p_prev = jnp.where(first, pp_ref[0], pltpu.roll(p, 1, 0))
    else:
        u_prev, p_prev = up_ref[...], pp_ref[...]
    du = u_prev - u
    rkv = p + (p_prev - p) * mu_rkv_ref[...]
    r, k, v = rkv[:, :w], rkv[:, w:2 * w], rkv[:, 2 * w:]
    xw = (u + du * mu_wag_ref[0:1]).astype(bf16)
    xa = (u + du * mu_wag_ref[1:2]).astype(bf16)
    xg = (u + du * mu_wag_ref[2:3]).astype(bf16)
    y = -(w0_ref[...] + dot(jnp.tanh(dot(xw, w1_ref[...])).astype(bf16), w2_ref[...]))
    softplus = jnp.maximum(y, 0.0) + jnp.log1p(jnp.exp(-jnp.abs(y)))
    lw_ref[...] = -jnp.exp(-softplus - 0.5)
    a = jax.nn.sigmoid(a0_ref[...] + dot(dot(xa, a1_ref[...]).astype(bf16), a2_ref[...]))
    g_ref[...] = dot(jax.nn.sigmoid(dot(xg, g1_ref[...])).astype(bf16), g2_ref[...])

    rr = lax.broadcasted_iota(jnp.int32, (LANES, LANES), 0) // RWKV_HEAD_DIM
    cc = lax.broadcasted_iota(jnp.int32, (LANES, LANES), 1) // RWKV_HEAD_DIM
    seg_ones = jnp.where(rr == cc, 1.0, 0.0).astype(bf16)

    def head_sum(x):
        hi = x.astype(bf16)
        lo = (x - hi.astype(f32)).astype(bf16)
        cols = [dot(hi[:, j:j + LANES], seg_ones) + dot(lo[:, j:j + LANES], seg_ones)
                for j in range(0, w, LANES)]
        return jnp.concatenate(cols, axis=1)

    kk = k * kk_ref[...]
    kk = kk / jnp.maximum(jnp.sqrt(head_sum(kk * kk)), 1e-12)
    k2 = k * (1.0 + (a - 1.0) * ka_ref[...])
    r_ref[...] = r
    k_ref[...] = k2
    v_ref[...] = v
    an_ref[...] = -kk
    b_ref[...] = kk * a
    bonus_ref[...] = head_sum(r * k2 * rk_ref[...]) * v


def rwkv_pre(u, u_prev, proj, p_prev, lp, *, rolled):
    n, d = u.shape
    tm = RWKV_PRE_TILE_M
    w3 = 3 * RWKV_WIDTH
    if rolled:
        prev_specs = [pl.BlockSpec((1, 1, d), lambda i: (i, 0, 0)), pl.BlockSpec((1, 1, w3), lambda i: (i, 0, 0))]
    else:
        prev_specs = [pl.BlockSpec((tm, d), lambda i: (i, 0)), pl.BlockSpec((tm, w3), lambda i: (i, 0))]
    bf16 = jnp.bfloat16
    row = lambda x: x.reshape(1, -1)
    consts = [row(lp["mu_rkv"]), lp["mu_wag"], row(lp["w0"]), lp["w1"].astype(bf16), lp["w2"].astype(bf16),
              row(lp["a0"]), lp["a1"].astype(bf16), lp["a2"].astype(bf16), lp["g1"].astype(bf16),
              lp["g2"].astype(bf16), row(lp["k_k"]), row(lp["k_a"]), row(lp["r_k"])]
    full = lambda a: pl.BlockSpec(a.shape, lambda i: (0, 0))
    out_spec = pl.BlockSpec((tm, RWKV_WIDTH), lambda i: (i, 0))
    return pl.pallas_call(
        functools.partial(_rwkv_pre_kernel, rolled=rolled),
        out_shape=tuple(jax.ShapeDtypeStruct((n, RWKV_WIDTH), jnp.float32) for _ in range(8)),
        grid=(n // tm,),
        in_specs=[pl.BlockSpec((tm, d), lambda i: (i, 0)), prev_specs[0],
                  pl.BlockSpec((tm, w3), lambda i: (i, 1)), prev_specs[1]]
                 + [full(c) for c in consts],
        out_specs=tuple(out_spec for _ in range(8)),
        compiler_params=pltpu.CompilerParams(
            dimension_semantics=("arbitrary",), vmem_limit_bytes=VMEM_LIMIT_BYTES),
        name="rwkv_pre",
    )(u, u_prev, proj, p_prev, *consts)


def _bmm(a, b):
    return lax.dot_general(a, b, (((2,), (1,)), ((0,), (0,))), preferred_element_type=jnp.float32)


def _bmm_nt(a, b):
    return lax.dot_general(a, b, (((2,), (2,)), ((0,), (0,))), preferred_element_type=jnp.float32)


def _bmm_tn(a, b):
    return lax.dot_general(a, b, (((1,), (1,)), ((0,), (0,))), preferred_element_type=jnp.float32)


def _rwkv_chunk_kernel(r_ref, lw_ref, k_ref, v_ref, a_ref, b_ref, g_ref, bonus_ref, gng_ref, gnb_ref, s0_ref,
                       o_ref, sT_ref, s_scr, *, n_seq, seq_rows, chunk, group):
    bf16, f32 = jnp.bfloat16, jnp.float32
    c = pl.program_id(1)
    C = chunk
    H, N = RWKV_HEADS, RWKV_HEAD_DIM

    @pl.when(c == 0)
    def _():
        s_scr[...] = s0_ref[...]

    row = lax.broadcasted_iota(jnp.int32, (C, C), 0)
    col = lax.broadcasted_iota(jnp.int32, (C, C), 1)
    strict = (row > col)[None]
    incl = (row >= col)[None]
    G = group
    tri = jnp.broadcast_to(jnp.where(incl, 1.0, 0.0).astype(bf16), (G * H, C, C))

    def by_head(ref, i):
        x = ref[i]
        if seq_rows < C:
            x = jnp.concatenate([x, jnp.zeros((C - seq_rows, x.shape[1]), f32)], axis=0)
        return jnp.stack([x[:, h * N:(h + 1) * N] for h in range(H)], axis=0)

    def one_seq(j, carry):
        seqs = [j * G + q for q in range(G)]
        stack = lambda ref: jnp.concatenate([by_head(ref, i) for i in seqs], axis=0)
        r, lw, k, v, a, b = (stack(ref) for ref in (r_ref, lw_ref, k_ref, v_ref, a_ref, b_ref))
        s0 = jnp.concatenate([s_scr[i] for i in seqs], axis=0)
        hi = lw.astype(bf16)
        r1 = lw - hi.astype(f32)
        mid = r1.astype(bf16)
        lo = (r1 - mid.astype(f32)).astype(bf16)
        lc = _bmm(tri, hi) + _bmm(tri, mid) + _bmm(tri, lo)
        e_pos = jnp.exp(lc)
        e_neg = jnp.exp(-lc)
        a_t = (a * jnp.exp(lc - lw)).astype(bf16)
        r_t = (r * e_pos).astype(bf16)
        b_t = (b * e_neg).astype(bf16)
        k_t = (k * e_neg).astype(bf16)
        v_b = v.astype(bf16)
        s0_b = s0.astype(bf16)

        a_ab = jnp.where(strict, _bmm_nt(a_t, b_t), 0.0)
        a_ak = jnp.where(strict, _bmm_nt(a_t, k_t), 0.0)
        a_rb = jnp.where(incl, _bmm_nt(r_t, b_t), 0.0)
        a_rk = jnp.where(incl, _bmm_nt(r_t, k_t), 0.0)

        x = _bmm_nt(a_t, s0_b) + _bmm(a_ak.astype(bf16), v_b)
        p = a_ab
        n = 1
        while n < C:
            p_b = p.astype(bf16)
            x = x + _bmm(p_b, x.astype(bf16))
            n *= 2
            if n < C:
                p = _bmm(p_b, p_b)
        u_b = x.astype(bf16)
        o = _bmm_nt(r_t, s0_b) + _bmm(a_rb.astype(bf16), u_b) + _bmm(a_rk.astype(bf16), v_b)
        s1 = s0 + _bmm_tn(u_b, b_t) + _bmm_tn(v_b, k_t)
        s1 = s1 * e_pos[:, C - 1:C, :]
        mean = jnp.mean(o, -1, keepdims=True)
        oc = o - mean
        var = jnp.mean(oc * oc, -1, keepdims=True)
        on = oc * lax.rsqrt(var + GN_EPS)
        for q, i in enumerate(seqs):
            s_scr[i] = s1[q * H:(q + 1) * H]
            on_i = jnp.concatenate([on[q * H + h] for h in range(H)], axis=1)[:seq_rows]
            o_ref[i] = (on_i * gng_ref[...] + gnb_ref[...] + bonus_ref[i]) * g_ref[i]
        return carry

    lax.fori_loop(0, n_seq // G, one_seq, 0)

    @pl.when(c == pl.num_programs(1) - 1)
    def _():
        sT_ref[...] = s_scr[...]


def rwkv_recurrence(s0, seqs, gn_g, gn_b, *, n_batch, n_chunks, seq_rows, n_seq, chunk, group=1):
    h, n = RWKV_HEADS, RWKV_HEAD_DIM
    w = RWKV_WIDTH
    seq_spec = pl.BlockSpec((n_seq, seq_rows, w), lambda i, c: (i * n_chunks + c, 0, 0))
    st_spec = pl.BlockSpec((n_seq, h, n, n), lambda i, c: (i, 0, 0, 0))
    vec_spec = pl.BlockSpec((1, w), lambda i, c: (0, 0))
    o, s_t = pl.pallas_call(
        functools.partial(_rwkv_chunk_kernel, n_seq=n_seq, seq_rows=seq_rows, chunk=chunk, group=group),
        out_shape=(jax.ShapeDtypeStruct((n_batch * n_chunks, seq_rows, w), jnp.float32),
                   jax.ShapeDtypeStruct((n_batch, h, n, n), jnp.float32)),
        grid=(n_batch // n_seq, n_chunks),
        in_specs=[seq_spec] * 8 + [vec_spec, vec_spec, st_spec],
        out_specs=(seq_spec, st_spec),
        scratch_shapes=[pltpu.VMEM((n_seq, h, n, n), jnp.float32)],
        compiler_params=pltpu.CompilerParams(
            dimension_semantics=("arbitrary", "arbitrary"), vmem_limit_bytes=VMEM_LIMIT_BYTES),
        name="rwkv_chunk",
    )(*seqs, gn_g.reshape(1, w), gn_b.reshape(1, w), s0)
    return o, s_t


def _moe_scatter_kernel(slot_ref, h_ref, init_hbm, out_hbm, stage, sems):
    del init_hbm
    i = pl.program_id(0)
    nt = pl.num_programs(0)
    tt = h_ref.shape[0]
    slot = i % 2

    def wait_buffer(s):
        for _ in range(TOP_K):
            pltpu.make_async_copy(stage.at[s], out_hbm.at[pl.ds(0, tt)], sems.at[s]).wait()

    @pl.when(i >= 2)
    def _():
        wait_buffer(slot)

    stage[slot] = h_ref[...]
    for j in range(tt):
        for k in range(TOP_K):
            dst = slot_ref[(i * tt + j) * TOP_K + k]
            pltpu.make_async_copy(stage.at[slot, pl.ds(j, 1)], out_hbm.at[pl.ds(dst, 1)], sems.at[slot]).start()

    @pl.when(i == nt - 1)
    def _():
        wait_buffer(slot)

        @pl.when(nt >= 2)
        def _():
            wait_buffer(1 - slot)


def _moe_scatter_rows(h, pair_slot, p_pad):
    n, d = h.shape
    tt = MOE_ROUTE_TILE
    return pl.pallas_call(
        _moe_scatter_kernel,
        out_shape=jax.ShapeDtypeStruct((p_pad, d), jnp.float32),
        grid_spec=pltpu.PrefetchScalarGridSpec(
            num_scalar_prefetch=1,
            grid=(n // tt,),
            in_specs=[pl.BlockSpec((tt, d), lambda i, ps: (i, 0)),
                      pl.BlockSpec(memory_space=pl.ANY)],
            out_specs=pl.BlockSpec(memory_space=pl.ANY),
            scratch_shapes=[pltpu.VMEM((2, tt, d), jnp.float32), pltpu.SemaphoreType.DMA((2,))],
        ),
        input_output_aliases={2: 0},
        compiler_params=pltpu.CompilerParams(
            dimension_semantics=("arbitrary",), vmem_limit_bytes=VMEM_LIMIT_BYTES),
        name="moe_scatter",
    )(pair_slot.reshape(-1), h, jnp.zeros((p_pad, d), jnp.float32))


def _weight_copies(w_hbm, stage, sem, expert, col0, width, slot, n_parts, part_stride):
    return [pltpu.make_async_copy(w_hbm.at[expert, :, pl.ds(pl.multiple_of(col0 + p * part_stride, LANES), width)],
                                  stage.at[slot, p], sem.at[slot]) for p in range(n_parts)]


def _stream_expert_weights(sched_ref, w_hbm, stage, sem, w_bf, *, width, n_parts, part_stride):
    s = pl.program_id(0) * pl.num_programs(1) + pl.program_id(1)
    field = lambda j: sched_ref[s * SCHED_COLS + j]

    @pl.when(field(0) != 0)
    def _():
        slot = field(1)
        cur = _weight_copies(w_hbm, stage, sem, field(2), field(3), width, slot, n_parts, part_stride)

        @pl.when(s == 0)
        def _():
            for cp in cur:
                cp.start()

        for cp in cur:
            cp.wait()
        for p in range(n_parts):
            w_bf[p] = stage[slot, p].astype(jnp.bfloat16)

        @pl.when(field(4) != 0)
        def _():
            for cp in _weight_copies(w_hbm, stage, sem, field(5), field(6), width, 1 - slot, n_parts, part_stride):
                cp.start()


def _weight_schedule(tile_expert, n_chunks, chunk_width):
    nt = tile_expert.shape[0]
    n = n_chunks * nt
    expert = jnp.tile(tile_expert, n_chunks)
    col0 = jnp.repeat(jnp.arange(n_chunks, dtype=jnp.int32) * chunk_width, nt)
    step = jnp.arange(n, dtype=jnp.int32)
    start = (step % nt == 0) | (expert != jnp.roll(expert, 1))
    slot = (jnp.cumsum(start.astype(jnp.int32)) - 1) % 2
    nxt = lax.cummin(jnp.where(start, step, n), axis=0, reverse=True)
    nxt = jnp.concatenate([nxt[1:], jnp.full((1,), n, jnp.int32)])
    has_next = nxt < n
    nxt_c = jnp.minimum(nxt, n - 1)
    return jnp.stack([start.astype(jnp.int32), slot, expert, col0, has_next.astype(jnp.int32),
                      expert[nxt_c], col0[nxt_c]], axis=1).astype(jnp.int32).reshape(-1)


def _moe_up_kernel(sched_ref, tile_valid_ref, x_ref, bg_ref, bu_ref, w_hbm, h_ref, stage, sem, w_bf):
    f = w_hbm.shape[2] // 2
    _stream_expert_weights(sched_ref, w_hbm, stage, sem, w_bf, width=h_ref.shape[1], n_parts=2, part_stride=f)
    i = pl.program_id(1)

    @pl.when(tile_valid_ref[i] != 0)
    def _():
        x = x_ref[...].astype(jnp.bfloat16)
        hg = jnp.dot(x, w_bf[0], preferred_element_type=jnp.float32) + bg_ref[0]
        hu = jnp.dot(x, w_bf[1], preferred_element_type=jnp.float32) + bu_ref[0]
        glu = jnp.minimum(hg, SWIGLU_LIMIT)
        up = jnp.clip(hu, -SWIGLU_LIMIT, SWIGLU_LIMIT)
        act = (up + 1.0) * glu * jax.nn.sigmoid(SWIGLU_ALPHA * glu)
        h_ref[...] = act.astype(h_ref.dtype)

    @pl.when(tile_valid_ref[i] == 0)
    def _():
        h_ref[...] = jnp.zeros_like(h_ref)


def _moe_down_kernel(sched_ref, tile_valid_ref, h_ref, bd_ref, w_hbm, y_ref, stage, sem, w_bf):
    _stream_expert_weights(sched_ref, w_hbm, stage, sem, w_bf, width=y_ref.shape[1], n_parts=1, part_stride=0)
    i = pl.program_id(1)

    @pl.when(tile_valid_ref[i] != 0)
    def _():
        y_ref[...] = jnp.dot(h_ref[...], w_bf[0], preferred_element_type=jnp.float32) + bd_ref[0]

    @pl.when(tile_valid_ref[i] == 0)
    def _():
        y_ref[...] = jnp.zeros_like(y_ref)


def _moe_grouped(x_sorted, tile_expert, tile_valid, w_gate_up, b_gate_up, w_down, b_down):
    p_pad, d = x_sorted.shape
    n_exp, _, two_f = w_gate_up.shape
    f = two_f // 2
    tm, fc, dc = MOE_TILE_M, MOE_F_CHUNK, MOE_D_CHUNK
    nt = p_pad // tm
    nfc = f // fc
    ndc = d // dc
    bgu = b_gate_up.reshape(n_exp, 1, two_f)
    bdn = b_down.reshape(n_exp, 1, d)
    params = pltpu.CompilerParams(dimension_semantics=("arbitrary", "arbitrary"),
                                  vmem_limit_bytes=MOE_VMEM_LIMIT_BYTES)
    te_of = lambda sched, c, i: sched[(c * nt + i) * SCHED_COLS + 2]

    h = pl.pallas_call(
        _moe_up_kernel,
        out_shape=jax.ShapeDtypeStruct((p_pad, f), jnp.bfloat16),
        grid_spec=pltpu.PrefetchScalarGridSpec(
            num_scalar_prefetch=2,
            grid=(nfc, nt),
            in_specs=[
                pl.BlockSpec((tm, d), lambda c, i, sc, tv: (i, 0)),
                pl.BlockSpec((1, 1, fc), lambda c, i, sc, tv: (te_of(sc, c, i), 0, c)),
                pl.BlockSpec((1, 1, fc), lambda c, i, sc, tv: (te_of(sc, c, i), 0, nfc + c)),
                pl.BlockSpec(memory_space=pl.ANY),
            ],
            out_specs=pl.BlockSpec((tm, fc), lambda c, i, sc, tv: (i, c)),
            scratch_shapes=[pltpu.VMEM((2, 2, d, fc), jnp.float32), pltpu.SemaphoreType.DMA((2,)),
                            pltpu.VMEM((2, d, fc), jnp.bfloat16)],
        ),
        compiler_params=params,
        name="moe_up",
    )(_weight_schedule(tile_expert, nfc, fc), tile_valid, x_sorted, bgu, bgu, w_gate_up)

    return pl.pallas_call(
        _moe_down_kernel,
        out_shape=jax.ShapeDtypeStruct((p_pad, d), jnp.float32),
        grid_spec=pltpu.PrefetchScalarGridSpec(
            num_scalar_prefetch=2,
            grid=(ndc, nt),
            in_specs=[
                pl.BlockSpec((tm, f), lambda c, i, sc, tv: (i, 0)),
                pl.BlockSpec((1, 1, dc), lambda c, i, sc, tv: (te_of(sc, c, i), 0, c)),
                pl.BlockSpec(memory_space=pl.ANY),
            ],
            out_specs=pl.BlockSpec((tm, dc), lambda c, i, sc, tv: (i, c)),
            scratch_shapes=[pltpu.VMEM((2, 1, f, dc), jnp.float32), pltpu.SemaphoreType.DMA((2,)),
                            pltpu.VMEM((1, f, dc), jnp.bfloat16)],
        ),
        compiler_params=params,
        name="moe_down",
    )(_weight_schedule(tile_expert, ndc, dc), tile_valid, h, bdn, w_down)


def _moe_combine_kernel(slot_ref, h_ref, w_ref, g_ref, b_ref, y_hbm, out_ref, buf, sems):
    i = pl.program_id(0)
    nt = pl.num_programs(0)
    tt = h_ref.shape[0]

    def start_tile(t, s):
        for j in range(tt):
            for k in range(TOP_K):
                src = slot_ref[(t * tt + j) * TOP_K + k]
                pltpu.make_async_copy(y_hbm.at[pl.ds(src, 1)], buf.at[s, pl.ds(k * tt + j, 1)], sems.at[s]).start()

    @pl.when(i == 0)
    def _():
        start_tile(0, 0)

    @pl.when(i + 1 < nt)
    def _():
        start_tile(i + 1, (i + 1) % 2)

    slot = i % 2
    pltpu.make_async_copy(y_hbm.at[pl.ds(0, TOP_K * tt)], buf.at[slot], sems.at[slot]).wait()
    w = w_ref[...]
    ffn = buf[slot, 0:tt] * w[:, 0:1]
    for k in range(1, TOP_K):
        ffn = ffn + buf[slot, k * tt:(k + 1) * tt] * w[:, k:k + 1]
    x = DEEPNORM_ALPHA * h_ref[...] + ffn
    mu = jnp.mean(x, -1, keepdims=True)
    xc = x - mu
    var = jnp.mean(xc * xc, -1, keepdims=True)
    out_ref[...] = xc * lax.rsqrt(var + LN_EPS) * g_ref[...] + b_ref[...]


def _moe_combine_ln(h, y, pair_slot, top_w, ln_g, ln_b):
    n, d = h.shape
    tt = MOE_ROUTE_TILE
    return pl.pallas_call(
        _moe_combine_kernel,
        out_shape=jax.ShapeDtypeStruct((n, d), jnp.float32),
        grid_spec=pltpu.PrefetchScalarGridSpec(
            num_scalar_prefetch=1,
            grid=(n // tt,),
            in_specs=[pl.BlockSpec((tt, d), lambda i, ps: (i, 0)),
                      pl.BlockSpec((tt, TOP_K), lambda i, ps: (i, 0)),
                      pl.BlockSpec((1, d), lambda i, ps: (0, 0)),
                      pl.BlockSpec((1, d), lambda i, ps: (0, 0)),
                      pl.BlockSpec(memory_space=pl.ANY)],
            out_specs=pl.BlockSpec((tt, d), lambda i, ps: (i, 0)),
            scratch_shapes=[pltpu.VMEM((2, TOP_K * tt, d), jnp.float32), pltpu.SemaphoreType.DMA((2,))],
        ),
        compiler_params=pltpu.CompilerParams(
            dimension_semantics=("arbitrary",), vmem_limit_bytes=VMEM_LIMIT_BYTES),
        name="moe_combine",
    )(pair_slot.reshape(-1), h, top_w, ln_g.reshape(1, d), ln_b.reshape(1, d), y)


def moe_ffn_ln(x, top_w, top_idx, w_gate_up, b_gate_up, w_down, b_down, ln_g, ln_b):
    n, d = x.shape
    tm = MOE_TILE_M
    nt = n * TOP_K // tm + N_EXPERTS
    onehot = (top_idx[:, :, None] == jnp.arange(N_EXPERTS, dtype=top_idx.dtype)).astype(jnp.int32)
    routed = jnp.sum(onehot, axis=1)
    before = jnp.cumsum(routed, axis=0) - routed
    counts = before[-1] + routed[-1]
    tiles_per = (counts + tm - 1) // tm
    tile_end = jnp.cumsum(tiles_per)
    first_row = (tile_end - tiles_per) * tm
    pair_slot = jnp.sum(onehot * (before + first_row)[:, None, :], axis=2)
    tile_ids = jnp.arange(nt, dtype=jnp.int32)
    tile_expert = jnp.minimum(jnp.sum((tile_ids[:, None] >= tile_end[None, :]).astype(jnp.int32), axis=1),
                              N_EXPERTS - 1).astype(jnp.int32)
    tile_valid = (tile_ids < tile_end[-1]).astype(jnp.int32)

    x_sorted = _moe_scatter_rows(x, pair_slot, nt * tm)
    y = _moe_grouped(x_sorted, tile_expert, tile_valid, w_gate_up, b_gate_up, w_down, b_down)
    return _moe_combine_ln(x, y, pair_slot, top_w, ln_g, ln_b)


def kernel(x_prompt, x_sample, cache_win0_kv, cache_win1_kv, cache_win2_kv, cache_mem_kv,
           state_wkv, state_shift, mem_prompt,
           w_in, w_gate, b_gate, w_branch, w_out, w_mem_kv,
           mu_rkv, mu_wag, w0, w1, w2, a0, a1, a2, g1, g2, k_k, k_a, r_k, gn_g, gn_b,
           ln1_g, ln1_b, w_router, b_router, w_gate_up, b_gate_up, w_down, b_down, ln2_g, ln2_b):
    bf16 = jnp.bfloat16
    xp, xs = x_prompt, x_sample
    bp, tp, d = xp.shape
    bs, ts, _ = xs.shape
    n_prompt = bp * tp
    l = 0
    lp = {"mu_rkv": mu_rkv[l], "mu_wag": mu_wag[l],
          "w0": w0[l], "w1": w1[l], "w2": w2[l], "a0": a0[l], "a1": a1[l], "a2": a2[l],
          "g1": g1[l], "g2": g2[l], "k_k": k_k[l], "k_a": k_a[l], "r_k": r_k[l]}

    xp2, xs2 = xp.reshape(-1, d), xs.reshape(-1, d)
    u_all = jnp.concatenate([xp2, xs2], axis=0)
    w_in_bf = w_in[l].astype(bf16)
    proj_p2 = matmul_bias(xp2.astype(bf16), w_in_bf, tile_m=PROJ_PROMPT_TILE_M, name="proj_in")
    proj_s2 = matmul_bias(xs2.astype(bf16), w_in_bf, tile_m=xs2.shape[0], name="proj_in_step")
    gate = matmul_bias(u_all.astype(bf16), w_gate[l].astype(bf16), b_gate[l], tile_m=PROJ_TILE_M, sigmoid=True,
                       name="proj_gate")
    mem_rows = mem_prompt.reshape(-1, d)
    mem_kv_p = matmul_bias(mem_rows.astype(bf16), w_mem_kv[l].astype(bf16), tile_m=mem_rows.shape[0], name="proj_mem")
    proj_p = proj_p2.reshape(bp, tp, PROJ_WIDTH)
    proj_s = proj_s2.reshape(bs, ts, PROJ_WIDTH)
    rkv0, qm0 = 3 * ATT_WIDTH, 3 * ATT_WIDTH + 3 * RWKV_WIDTH

    heads = lambda t: t.reshape(t.shape[0], t.shape[1], N_ATT_HEADS, HEAD_DIM)
    ka_p, va_p = (heads(proj_p[..., i * ATT_WIDTH:(i + 1) * ATT_WIDTH]) for i in (1, 2))
    ka_s, va_s = (heads(proj_s[..., i * ATT_WIDTH:(i + 1) * ATT_WIDTH]) for i in (1, 2))
    outs, lses, win_p, win_s = [], [], [], []
    for g, (win, dil) in enumerate(DIL_GROUPS):
        hs = slice(g * HEADS_PER_GROUP, (g + 1) * HEADS_PER_GROUP)
        o, lse = dilated_attention_prompt(proj_p2, bp, tp, g, dil)
        outs.append(o)
        lses.append(lse)
        keep = min(win, tp)
        win_p.append(jnp.stack([ka_p[:, tp - keep:, hs], va_p[:, tp - keep:, hs]], axis=2))
        win_s.append(jnp.stack([ka_s[:, :, hs], va_s[:, :, hs]], axis=2))
    o_att_p = att_combine(outs, lses)
    o_att_s = dilated_attention_step(proj_s, cache_win0_kv[l], cache_win1_kv[l], cache_win2_kv[l])

    tq = MEM_ATTN_TILE_Q
    o_mem_p = memory_attention(proj_p2.reshape(-1, tq, PROJ_WIDTH), qm0 // MEM_WIDTH,
                               mem_kv_p.reshape(bp, MEM_TOKENS, 2 * MEM_WIDTH), tp // tq)
    o_mem_s = memory_attention(proj_s, qm0 // MEM_WIDTH, cache_mem_kv[l], 1, n_seq=MEM_ATTN_STEP_SEQS)

    tm = RWKV_PRE_TILE_M

    def tile_prev_rows(x):
        last = x[:, tm - 1::tm][:, :-1]
        return jnp.concatenate([jnp.zeros_like(last[:, :1]), last], axis=1).reshape(-1, 1, x.shape[-1])

    rwkv_in_p = rwkv_pre(xp2, tile_prev_rows(xp), proj_p2, tile_prev_rows(proj_p[..., rkv0:qm0]), lp, rolled=True)
    u_last_s = state_shift[l].astype(xs.dtype)
    p_last_s = matmul_bias(u_last_s.astype(bf16), w_in_bf[:, rkv0:qm0], tile_m=bs, name="proj_shift")
    shift = lambda first, rest: jnp.concatenate([first[:, None], rest[:, :-1]], axis=1).reshape(-1, rest.shape[-1])
    rwkv_in_s = rwkv_pre(xs2, shift(u_last_s, xs), proj_s2, shift(p_last_s, proj_s[..., rkv0:qm0]), lp, rolled=False)
    s0 = jnp.zeros((bp, RWKV_HEADS, RWKV_HEAD_DIM, RWKV_HEAD_DIM), jnp.float32)
    o_rwkv_p, wkv_p = rwkv_recurrence(
        s0, [x.reshape(-1, RWKV_CHUNK, RWKV_WIDTH) for x in rwkv_in_p], gn_g[l], gn_b[l],
        n_batch=bp, n_chunks=tp // RWKV_CHUNK, seq_rows=RWKV_CHUNK, n_seq=1, chunk=RWKV_CHUNK)
    o_rwkv_s, wkv_s = rwkv_recurrence(
        state_wkv[l].astype(jnp.float32), [x.reshape(bs, ts, RWKV_WIDTH) for x in rwkv_in_s], gn_g[l], gn_b[l],
        n_batch=bs, n_chunks=1, seq_rows=ts, n_seq=RWKV_SEQ_PER_STEP, chunk=-(-ts // SUBLANES) * SUBLANES,
        group=RWKV_SEQ_GROUP)

    cat = lambda a, b, w: jnp.concatenate([a.reshape(-1, w), b.reshape(-1, w)], axis=0)
    h, top_w, top_idx = merge_ln_router(
        cat(o_att_p, o_att_s, ATT_OUT), cat(o_rwkv_p, o_rwkv_s, RWKV_WIDTH), cat(o_mem_p, o_mem_s, MEM_WIDTH),
        gate, u_all, w_branch[l].astype(bf16), w_out[l].astype(bf16), ln1_g[l], ln1_b[l], w_router[l], b_router[l])
    y_all = moe_ffn_ln(h, top_w, top_idx, w_gate_up[l], b_gate_up[l], w_down[l], b_down[l], ln2_g[l], ln2_b[l])
    yp = y_all[:n_prompt].reshape(bp, tp, d)
    ys = y_all[n_prompt:].reshape(bs, ts, d)
    mem_kv_out = mem_kv_p.reshape(bp, MEM_TOKENS, 2, MEM_HEADS, MEM_HEAD_DIM)
    return (yp, ys,
            win_p[0][None], win_p[1][None], win_p[2][None], wkv_p.astype(xp.dtype)[None],
            xp[:, -1][None], mem_kv_out[None],
            win_s[0].astype(cache_win0_kv.dtype)[None], win_s[1].astype(cache_win0_kv.dtype)[None],
            win_s[2].astype(cache_win0_kv.dtype)[None], wkv_s.astype(state_wkv.dtype)[None],
            xs[:, -1].astype(state_shift.dtype)[None])
```

```python
import functools

import jax
import jax.numpy as jnp
from jax import lax
from jax.experimental import pallas as pl
from jax.experimental.pallas import tpu as pltpu

D_MODEL = 2048
DEPTH = 1
HEAD_DIM = 64
DIL_GROUPS = ((128, 1), (512, 4), (2048, 16))
HEADS_PER_GROUP = 4
N_ATT_HEADS = HEADS_PER_GROUP * len(DIL_GROUPS)
ATT_WIDTH = N_ATT_HEADS * HEAD_DIM
ATT_OUT = HEADS_PER_GROUP * HEAD_DIM
BAND_BLOCK = 128
N_BACK = 128
GROUP_WIDTH = HEADS_PER_GROUP * HEAD_DIM
RWKV_HEADS = 12
RWKV_HEAD_DIM = 64
RWKV_WIDTH = RWKV_HEADS * RWKV_HEAD_DIM
GN_EPS = 64e-5
MEM_TOKENS = 256
MEM_HEADS = 4
MEM_HEAD_DIM = 128
MEM_WIDTH = MEM_HEADS * MEM_HEAD_DIM
PROJ_WIDTH = 3 * ATT_WIDTH + 3 * RWKV_WIDTH + MEM_WIDTH
N_BRANCHES = 3
N_EXPERTS = 32
TOP_K = 4
D_EXPERT = 2048
SWIGLU_LIMIT = 7.0
SWIGLU_ALPHA = 1.702
LN_EPS = 1e-5
DEEPNORM_ALPHA = (2.0 * DEPTH) ** 0.25
NEG_INF = -1e30

VMEM_LIMIT_BYTES = 56 * 1024 * 1024
SUBLANES = 8
LANES = 128
MOE_TILE_M = 256
MOE_F_CHUNK = 1024
MOE_D_CHUNK = 1024
MOE_ROUTE_TILE = 64
MOE_VMEM_LIMIT_BYTES = 58 * 1024 * 1024
SCHED_COLS = 7
RWKV_CHUNK = 64
RWKV_SEQ_PER_STEP = 8
RWKV_SEQ_GROUP = 4
RWKV_PRE_TILE_M = 256
PROJ_TILE_M = 1088
PROJ_PROMPT_TILE_M = 1024
MEM_ATTN_STEP_SEQS = 8
MATMUL_TILE_N = 512
MERGE_TILE_M = 256
ATT_COMBINE_TILE = 512
MEM_ATTN_TILE_Q = 512
ATTN_STEP_SEQS = 2


def _matmul_kernel(x_ref, w_ref, b_ref, o_ref, *, sigmoid):
    y = jnp.dot(x_ref[...], w_ref[...], preferred_element_type=jnp.float32) + b_ref[...]
    o_ref[...] = jax.nn.sigmoid(y) if sigmoid else y


def matmul_bias(x, w, bias=None, *, tile_m, sigmoid=False, name="matmul"):
    m, k = x.shape
    n = w.shape[1]
    tn = MATMUL_TILE_N if n % MATMUL_TILE_N == 0 else n
    if bias is None:
        bias = jnp.zeros((n,), jnp.float32)
    return pl.pallas_call(
        functools.partial(_matmul_kernel, sigmoid=sigmoid),
        out_shape=jax.ShapeDtypeStruct((m, n), jnp.float32),
        grid=(n // tn, m // tile_m),
        in_specs=[pl.BlockSpec((tile_m, k), lambda j, i: (i, 0)),
                  pl.BlockSpec((k, tn), lambda j, i: (0, j)),
                  pl.BlockSpec((1, tn), lambda j, i: (0, j))],
        out_specs=pl.BlockSpec((tile_m, tn), lambda j, i: (i, j)),
        compiler_params=pltpu.CompilerParams(
            dimension_semantics=("arbitrary", "arbitrary"), vmem_limit_bytes=VMEM_LIMIT_BYTES),
        name=name,
    )(x, w, bias.reshape(1, n))


def _merge_kernel(att_ref, rwkv_ref, mem_ref, gate_ref, x_ref, wb_ref, wo_ref, g_ref, b_ref,
                  wr_hi_ref, wr_lo_ref, br_ref, h_ref, topw_ref, topi_ref):
    bf16, f32 = jnp.bfloat16, jnp.float32
    d = x_ref.shape[1]
    dot = functools.partial(jnp.dot, preferred_element_type=f32)
    y = dot(att_ref[...].astype(bf16), wb_ref[0:ATT_OUT]) * gate_ref[:, 0:d]
    y = y + dot(rwkv_ref[...].astype(bf16), wb_ref[ATT_OUT:ATT_OUT + RWKV_WIDTH]) * gate_ref[:, d:2 * d]
    y = y + dot(mem_ref[...].astype(bf16), wb_ref[ATT_OUT + RWKV_WIDTH:]) * gate_ref[:, 2 * d:]
    z = DEEPNORM_ALPHA * x_ref[...] + dot(y.astype(bf16), wo_ref[...])
    mu = jnp.mean(z, -1, keepdims=True)
    zc = z - mu
    var = jnp.mean(zc * zc, -1, keepdims=True)
    h = zc * lax.rsqrt(var + LN_EPS) * g_ref[...] + b_ref[...]
    h_ref[...] = h

    h_hi = h.astype(bf16)
    h_lo = (h - h_hi.astype(f32)).astype(bf16)
    logits = (dot(h_hi, wr_hi_ref[...]) + dot(h_hi, wr_lo_ref[...]) + dot(h_lo, wr_hi_ref[...])) + br_ref[...]
    lane = lax.broadcasted_iota(jnp.int32, logits.shape, 1)
    work = jnp.where(lane < N_EXPERTS, logits, -jnp.inf)
    tops, ids = [], []
    for _ in range(TOP_K):
        m = jnp.max(work, -1, keepdims=True)
        idx = jnp.min(jnp.where(work == m, lane, LANES), -1, keepdims=True)
        tops.append(m)
        ids.append(idx)
        work = jnp.where(lane == idx, -jnp.inf, work)
    e = [jnp.exp(t - tops[0]) for t in tops]
    den = e[0] + e[1] + e[2] + e[3]
    tw = jnp.zeros(logits.shape, f32)
    ti = jnp.zeros(logits.shape, jnp.int32)
    for k in range(TOP_K):
        tw = jnp.where(lane == k, e[k] / den, tw)
        ti = jnp.where(lane == k, ids[k], ti)
    topw_ref[...] = tw
    topi_ref[...] = ti


def merge_ln_router(o_att, o_rwkv, o_mem, gate, x, w_branch, w_out, ln_g, ln_b, w_router, b_router):
    n, d = x.shape
    tm = MERGE_TILE_M
    wr = jnp.zeros((d, LANES), jnp.float32).at[:, :N_EXPERTS].set(w_router)
    wr_hi = wr.astype(jnp.bfloat16)
    wr_lo = (wr - wr_hi.astype(jnp.float32)).astype(jnp.bfloat16)
    br = jnp.zeros((1, LANES), jnp.float32).at[0, :N_EXPERTS].set(b_router)
    rows = lambda w: pl.BlockSpec((tm, w), lambda i: (i, 0))
    full = lambda a: pl.BlockSpec(a.shape, lambda i: (0, 0))
    h, topw, topi = pl.pallas_call(
        _merge_kernel,
        out_shape=(jax.ShapeDtypeStruct((n, d), jnp.float32),
                   jax.ShapeDtypeStruct((n, LANES), jnp.float32),
                   jax.ShapeDtypeStruct((n, LANES), jnp.int32)),
        grid=(n // tm,),
        in_specs=[rows(ATT_OUT), rows(RWKV_WIDTH), rows(MEM_WIDTH), rows(3 * d), rows(d),
                  full(w_branch), full(w_out), pl.BlockSpec((1, d), lambda i: (0, 0)),
                  pl.BlockSpec((1, d), lambda i: (0, 0)), full(wr_hi), full(wr_lo), full(br)],
        out_specs=(rows(d), rows(LANES), rows(LANES)),
        compiler_params=pltpu.CompilerParams(
            dimension_semantics=("arbitrary",), vmem_limit_bytes=VMEM_LIMIT_BYTES),
        name="merge_ln_router",
    )(o_att, o_rwkv, o_mem, gate, x, w_branch, w_out, ln_g.reshape(1, d), ln_b.reshape(1, d), wr_hi, wr_lo, br)
    return h, topw[:, :TOP_K], topi[:, :TOP_K]


def _alibi_slope(head):
    return 2.0 ** (-8.0 * (head + 1) / N_ATT_HEADS)


def _dot_nt(a, b):
    return lax.dot_general(a, b, (((1,), (1,)), ((), ())), preferred_element_type=jnp.float32)


def _dil_attn_prompt_kernel(*refs, group, dil):
    bf16 = jnp.bfloat16
    blk = BAND_BLOCK
    halves = GROUP_WIDTH // LANES
    ins = [refs[5 * a:5 * a + 5] for a in range(halves)]
    o_refs = refs[5 * halves:5 * halves + halves]
    lse_refs = refs[6 * halves:]
    i = pl.program_id(1)
    qi = lax.broadcasted_iota(jnp.int32, (blk, blk), 0)
    kj = lax.broadcasted_iota(jnp.int32, (blk, blk), 1)
    dist_c = qi - kj
    dist_p = dist_c + blk
    valid_c = dist_c >= 0
    valid_p = (dist_p <= blk) & (i > 0)
    scale = HEAD_DIM ** -0.5
    per_half = LANES // HEAD_DIM
    rows = lambda ref, r: ref[pl.ds(r, blk, stride=dil), :] if dil > 1 else ref[...]
    for r in range(dil):
        x = [[rows(ref, r).astype(bf16) for ref in ins[a]] for a in range(halves)]
        part = lambda a, j, hh: x[a][j][:, hh * HEAD_DIM:(hh + 1) * HEAD_DIM]
        units = [(a, hh) for a in range(halves) for hh in range(per_half)]
        st = []
        for a, hh in units:
            slope = _alibi_slope(group * HEADS_PER_GROUP + a * per_half + hh) * dil
            s_c = _dot_nt(part(a, 0, hh), part(a, 1, hh)) * scale - slope * dist_c.astype(jnp.float32)
            s_p = _dot_nt(part(a, 0, hh), part(a, 2, hh)) * scale - slope * dist_p.astype(jnp.float32)
            st.append((jnp.where(valid_c, s_c, NEG_INF), jnp.where(valid_p, s_p, NEG_INF)))
        st = [(s_c, s_p, jnp.maximum(jnp.max(s_c, -1, keepdims=True), jnp.max(s_p, -1, keepdims=True))) for s_c, s_p in st]
        st = [(jnp.exp(s_c - m), jnp.exp(s_p - m), m) for s_c, s_p, m in st]
        st = [(p_c, p_p, m, jnp.sum(p_c, -1, keepdims=True) + jnp.sum(p_p, -1, keepdims=True)) for p_c, p_p, m in st]
        outs, lses = [], []
        for (a, hh), (p_c, p_p, m, den) in zip(units, st):
            pv = (jnp.dot(p_c.astype(bf16), part(a, 3, hh), preferred_element_type=jnp.float32)
                  + jnp.dot(p_p.astype(bf16), part(a, 4, hh), preferred_element_type=jnp.float32))
            outs.append(pv / den)
            lses.append(jnp.broadcast_to(m + jnp.log(den), (blk, HEAD_DIM)))
        for a in range(halves):
            o_a = jnp.concatenate(outs[a * per_half:(a + 1) * per_half], axis=1)
            l_a = jnp.concatenate(lses[a * per_half:(a + 1) * per_half], axis=1)
            if dil > 1:
                o_refs[a][pl.ds(r, blk, stride=dil), :] = o_a
                lse_refs[a][pl.ds(r, blk, stride=dil), :] = l_a
            else:
                o_refs[a][...] = o_a
                lse_refs[a][...] = l_a


def dilated_attention_prompt(proj, b, t, group, dil):
    rows = BAND_BLOCK * dil
    halves = GROUP_WIDTH // LANES
    assert t % rows == 0
    nblk = t // rows
    col = lambda part, a: (part * ATT_WIDTH + group * GROUP_WIDTH) // LANES + a
    cur = lambda c: pl.BlockSpec((rows, LANES), lambda n, i: (n * nblk + i, c))
    prev = lambda c: pl.BlockSpec((rows, LANES), lambda n, i: (n * nblk + jnp.maximum(i - 1, 0), c))
    in_specs = []
    for a in range(halves):
        in_specs += [cur(col(0, a)), cur(col(1, a)), prev(col(1, a)), cur(col(2, a)), prev(col(2, a))]
    out_shape = jax.ShapeDtypeStruct((b * t, LANES), jnp.float32)
    res = pl.pallas_call(
        functools.partial(_dil_attn_prompt_kernel, group=group, dil=dil),
        out_shape=(out_shape,) * (2 * halves),
        grid=(b, nblk),
        in_specs=in_specs,
        out_specs=(cur(0),) * (2 * halves),
        compiler_params=pltpu.CompilerParams(
            dimension_semantics=("arbitrary", "arbitrary"), vmem_limit_bytes=VMEM_LIMIT_BYTES),
        name="dil_attn_prompt_g%d" % group,
    )(*([proj] * (5 * halves)))
    return jnp.concatenate(res[:halves], axis=1), jnp.concatenate(res[halves:], axis=1)


def _att_combine_kernel(o0, o1, o2, l0, l1, l2, out_ref):
    a, b, c = l0[...], l1[...], l2[...]
    m = jnp.maximum(jnp.maximum(a, b), c)
    ea, eb, ec = jnp.exp(a - m), jnp.exp(b - m), jnp.exp(c - m)
    out_ref[...] = (ea * o0[...] + eb * o1[...] + ec * o2[...]) / (ea + eb + ec)


def att_combine(outs, lses):
    n, w = outs[0].shape
    spec = pl.BlockSpec((ATT_COMBINE_TILE, w), lambda i: (i, 0))
    return pl.pallas_call(
        _att_combine_kernel,
        out_shape=jax.ShapeDtypeStruct((n, w), jnp.float32),
        grid=(n // ATT_COMBINE_TILE,),
        in_specs=[spec] * 6,
        out_specs=spec,
        compiler_params=pltpu.CompilerParams(dimension_semantics=("arbitrary",)),
        name="att_combine",
    )(*outs, *lses)


def _mem_attn_kernel(q_ref, kv_ref, o_ref):
    bf16 = jnp.bfloat16
    e = MEM_HEAD_DIM
    scale = e ** -0.5
    units = [(n, h) for n in range(q_ref.shape[0]) for h in range(MEM_HEADS)]
    if len(kv_ref.shape) == 3:
        key = lambda n, h: kv_ref[n, :, h * e:(h + 1) * e]
        val = lambda n, h: kv_ref[n, :, (MEM_HEADS + h) * e:(MEM_HEADS + h + 1) * e]
    else:
        key = lambda n, h: kv_ref[n, :, 0, h, :]
        val = lambda n, h: kv_ref[n, :, 1, h, :]
    s = [_dot_nt(q_ref[n, :, h * e:(h + 1) * e].astype(bf16), key(n, h).astype(bf16)) * scale for n, h in units]
    m = [jnp.max(x, -1, keepdims=True) for x in s]
    p = [jnp.exp(x - mx) for x, mx in zip(s, m)]
    p = [x / jnp.sum(x, -1, keepdims=True) for x in p]
    for (n, h), x in zip(units, p):
        o_ref[n, :, h * e:(h + 1) * e] = jnp.dot(x.astype(bf16), val(n, h).astype(bf16),
                                                 preferred_element_type=jnp.float32)


def memory_attention(q, q_col_block, mem_kv, blocks_per_seq, n_seq=1):
    b = mem_kv.shape[0]
    tq = q.shape[1]
    w = MEM_WIDTH
    assert n_seq == 1 or blocks_per_seq == 1
    kv_zeros = (0,) * (mem_kv.ndim - 1)
    return pl.pallas_call(
        _mem_attn_kernel,
        out_shape=jax.ShapeDtypeStruct((b * blocks_per_seq, tq, w), jnp.float32),
        grid=(b // n_seq, blocks_per_seq),
        in_specs=[pl.BlockSpec((n_seq, tq, w), lambda n, i: (n * blocks_per_seq + i, 0, q_col_block)),
                  pl.BlockSpec((n_seq,) + mem_kv.shape[1:], lambda n, i: (n,) + kv_zeros)],
        out_specs=pl.BlockSpec((n_seq, tq, w), lambda n, i: (n * blocks_per_seq + i, 0, 0)),
        compiler_params=pltpu.CompilerParams(
            dimension_semantics=("arbitrary", "arbitrary"), vmem_limit_bytes=VMEM_LIMIT_BYTES),
        name="mem_attn",
    )(q, mem_kv)


def _dil_attn_step_kernel(qkv_ref, c0_ref, c1_ref, c2_ref, o_ref, *, n_new):
    bf16, f32 = jnp.bfloat16, jnp.float32
    scale = HEAD_DIM ** -0.5
    caches = (c0_ref, c1_ref, c2_ref)
    n_groups = len(DIL_GROUPS)
    tq = lax.broadcasted_iota(jnp.int32, (n_new, n_new), 0)
    tk = lax.broadcasted_iota(jnp.int32, (n_new, n_new), 1)
    slots = [(b, h) for b in range(qkv_ref.shape[0]) for h in range(HEADS_PER_GROUP)]
    units = [(b, h, g) for b, h in slots for g in range(n_groups)]
    new_cols = lambda b, part, head: qkv_ref[b, :, part * ATT_WIDTH + head * HEAD_DIM:
                                             part * ATT_WIDTH + (head + 1) * HEAD_DIM].astype(bf16)
    st = {}
    for b, h, g in units:
        win, dil = DIL_GROUPS[g]
        head = g * HEADS_PER_GROUP + h
        slope = _alibi_slope(head)
        q = new_cols(b, 0, head)
        s = jnp.dot(q, caches[g][b, 0, h].astype(bf16), preferred_element_type=f32) * scale
        dist = (win + lax.broadcasted_iota(jnp.int32, (n_new, win), 0)
                - lax.broadcasted_iota(jnp.int32, (n_new, win), 1))
        valid = ((dist & (dil - 1)) == 0) & (dist <= N_BACK * dil)
        s = jnp.where(valid, s - slope * dist.astype(f32), NEG_INF)
        sn = _dot_nt(q, new_cols(b, 1, head)) * scale
        dn = tq - tk
        sn = jnp.where((dn >= 0) & ((dn & (dil - 1)) == 0), sn - slope * dn.astype(f32), NEG_INF)
        st[b, h, g] = (s, sn)
    for u in units:
        s, sn = st[u]
        st[u] = (s, sn, jnp.maximum(jnp.max(s, axis=1, keepdims=True), jnp.max(sn, axis=1, keepdims=True)))
    for u in units:
        s, sn, m = st[u]
        p = jnp.exp(s - m)
        pn = jnp.exp(sn - m)
        st[u] = (p, pn, m, jnp.sum(p, axis=1, keepdims=True) + jnp.sum(pn, axis=1, keepdims=True))
    for b, h, g in units:
        p, pn, m, den = st[b, h, g]
        head = g * HEADS_PER_GROUP + h
        pv = (_dot_nt(p.astype(bf16), caches[g][b, 1, h].astype(bf16))
              + jnp.dot(pn.astype(bf16), new_cols(b, 2, head), preferred_element_type=f32))
        st[b, h, g] = (pv / den, m + jnp.log(den))
    for b, h in slots:
        outs = [st[b, h, g][0] for g in range(n_groups)]
        lses = [st[b, h, g][1] for g in range(n_groups)]
        mm = jnp.maximum(jnp.maximum(lses[0], lses[1]), lses[2])
        w = [jnp.exp(l - mm) for l in lses]
        o_ref[b, :, h * HEAD_DIM:(h + 1) * HEAD_DIM] = (
            (w[0] * outs[0] + w[1] * outs[1] + w[2] * outs[2]) / (w[0] + w[1] + w[2]))


def dilated_attention_step(p_s, cache0, cache1, cache2):
    b, t, _ = p_s.shape
    feat_major = lambda c: jnp.transpose(c, (0, 2, 3, 4, 1))
    ns = ATTN_STEP_SEQS
    win_spec = lambda w: pl.BlockSpec((ns, 2, HEADS_PER_GROUP, HEAD_DIM, w), lambda n: (n, 0, 0, 0, 0))
    return pl.pallas_call(
        functools.partial(_dil_attn_step_kernel, n_new=t),
        out_shape=jax.ShapeDtypeStruct((b, t, GROUP_WIDTH), jnp.float32),
        grid=(b // ns,),
        in_specs=[pl.BlockSpec((ns, t, 3 * ATT_WIDTH), lambda n: (n, 0, 0))] + [win_spec(w) for w, _ in DIL_GROUPS],
        out_specs=pl.BlockSpec((ns, t, GROUP_WIDTH), lambda n: (n, 0, 0)),
        compiler_params=pltpu.CompilerParams(
            dimension_semantics=("arbitrary",), vmem_limit_bytes=VMEM_LIMIT_BYTES),
        name="dil_attn_step",
    )(p_s, feat_major(cache0), feat_major(cache1), feat_major(cache2))


def _rwkv_pre_kernel(u_ref, up_ref, p_ref, pp_ref, mu_rkv_ref, mu_wag_ref, w0_ref, w1_ref, w2_ref,
                     a0_ref, a1_ref, a2_ref, g1_ref, g2_ref, kk_ref, ka_ref, rk_ref,
                     r_ref, lw_ref, k_ref, v_ref, an_ref, b_ref, g_ref, bonus_ref, *, rolled):
    bf16, f32 = jnp.bfloat16, jnp.float32
    dot = functools.partial(jnp.dot, preferred_element_type=f32)
    w = RWKV_WIDTH
    u = u_ref[...]
    p = p_ref[...]
    if rolled:
        first = lax.broadcasted_iota(jnp.int32, (u.shape[0], 1), 0) == 0
        u_prev = jnp.where(first, up_ref[0], pltpu.roll(u, 1, 0))
        p_prev = jnp.where(first, pp_ref[0], pltpu.roll(p, 1, 0))
    else:
        u_prev, p_prev = up_ref[...], pp_ref[...]
    du = u_prev - u
    rkv = p + (p_prev - p) * mu_rkv_ref[...]
    r, k, v = rkv[:, :w], rkv[:, w:2 * w], rkv[:, 2 * w:]
    xw = (u + du * mu_wag_ref[0:1]).astype(bf16)
    xa = (u + du * mu_wag_ref[1:2]).astype(bf16)
    xg = (u + du * mu_wag_ref[2:3]).astype(bf16)
    y = -(w0_ref[...] + dot(jnp.tanh(dot(xw, w1_ref[...])).astype(bf16), w2_ref[...]))
    softplus = jnp.maximum(y, 0.0) + jnp.log1p(jnp.exp(-jnp.abs(y)))
    lw_ref[...] = -jnp.exp(-softplus - 0.5)
    a = jax.nn.sigmoid(a0_ref[...] + dot(dot(xa, a1_ref[...]).astype(bf16), a2_ref[...]))
    g_ref[...] = dot(jax.nn.sigmoid(dot(xg, g1_ref[...])).astype(bf16), g2_ref[...])

    rr = lax.broadcasted_iota(jnp.int32, (LANES, LANES), 0) // RWKV_HEAD_DIM
    cc = lax.broadcasted_iota(jnp.int32, (LANES, LANES), 1) // RWKV_HEAD_DIM
    seg_ones = jnp.where(rr == cc, 1.0, 0.0).astype(bf16)

    def head_sum(x):
        hi = x.astype(bf16)
        lo = (x - hi.astype(f32)).astype(bf16)
        cols = [dot(hi[:, j:j + LANES], seg_ones) + dot(lo[:, j:j + LANES], seg_ones)
                for j in range(0, w, LANES)]
        return jnp.concatenate(cols, axis=1)

    kk = k * kk_ref[...]
    kk = kk / jnp.maximum(jnp.sqrt(head_sum(kk * kk)), 1e-12)
    k2 = k * (1.0 + (a - 1.0) * ka_ref[...])
    r_ref[...] = r
    k_ref[...] = k2
    v_ref[...] = v
    an_ref[...] = -kk
    b_ref[...] = kk * a
    bonus_ref[...] = head_sum(r * k2 * rk_ref[...]) * v


def rwkv_pre(u, u_prev, proj, p_prev, lp, *, rolled):
    n, d = u.shape
    tm = RWKV_PRE_TILE_M
    w3 = 3 * RWKV_WIDTH
    if rolled:
        prev_specs = [pl.BlockSpec((1, 1, d), lambda i: (i, 0, 0)), pl.BlockSpec((1, 1, w3), lambda i: (i, 0, 0))]
    else:
        prev_specs = [pl.BlockSpec((tm, d), lambda i: (i, 0)), pl.BlockSpec((tm, w3), lambda i: (i, 0))]
    bf16 = jnp.bfloat16
    row = lambda x: x.reshape(1, -1)
    consts = [row(lp["mu_rkv"]), lp["mu_wag"], row(lp["w0"]), lp["w1"].astype(bf16), lp["w2"].astype(bf16),
              row(lp["a0"]), lp["a1"].astype(bf16), lp["a2"].astype(bf16), lp["g1"].astype(bf16),
              lp["g2"].astype(bf16), row(lp["k_k"]), row(lp["k_a"]), row(lp["r_k"])]
    full = lambda a: pl.BlockSpec(a.shape, lambda i: (0, 0))
    out_spec = pl.BlockSpec((tm, RWKV_WIDTH), lambda i: (i, 0))
    return pl.pallas_call(
        functools.partial(_rwkv_pre_kernel, rolled=rolled),
        out_shape=tuple(jax.ShapeDtypeStruct((n, RWKV_WIDTH), jnp.float32) for _ in range(8)),
        grid=(n // tm,),
        in_specs=[pl.BlockSpec((tm, d), lambda i: (i, 0)), prev_specs[0],
                  pl.BlockSpec((tm, w3), lambda i: (i, 1)), prev_specs[1]]
                 + [full(c) for c in consts],
        out_specs=tuple(out_spec for _ in range(8)),
        compiler_params=pltpu.CompilerParams(
            dimension_semantics=("arbitrary",), vmem_limit_bytes=VMEM_LIMIT_BYTES),
        name="rwkv_pre",
    )(u, u_prev, proj, p_prev, *consts)


def _bmm(a, b):
    return lax.dot_general(a, b, (((2,), (1,)), ((0,), (0,))), preferred_element_type=jnp.float32)


def _bmm_nt(a, b):
    return lax.dot_general(a, b, (((2,), (2,)), ((0,), (0,))), preferred_element_type=jnp.float32)


def _bmm_tn(a, b):
    return lax.dot_general(a, b, (((1,), (1,)), ((0,), (0,))), preferred_element_type=jnp.float32)


def _rwkv_chunk_kernel(r_ref, lw_ref, k_ref, v_ref, a_ref, b_ref, g_ref, bonus_ref, gng_ref, gnb_ref, s0_ref,
                       o_ref, sT_ref, s_scr, *, n_seq, seq_rows, chunk, group):
    bf16, f32 = jnp.bfloat16, jnp.float32
    c = pl.program_id(1)
    C = chunk
    H, N = RWKV_HEADS, RWKV_HEAD_DIM

    @pl.when(c == 0)
    def _():
        s_scr[...] = s0_ref[...]

    row = lax.broadcasted_iota(jnp.int32, (C, C), 0)
    col = lax.broadcasted_iota(jnp.int32, (C, C), 1)
    strict = (row > col)[None]
    incl = (row >= col)[None]
    G = group
    tri = jnp.broadcast_to(jnp.where(incl, 1.0, 0.0).astype(bf16), (G * H, C, C))

    def by_head(ref, i):
        x = ref[i]
        if seq_rows < C:
            x = jnp.concatenate([x, jnp.zeros((C - seq_rows, x.shape[1]), f32)], axis=0)
        return jnp.stack([x[:, h * N:(h + 1) * N] for h in range(H)], axis=0)

    def one_seq(j, carry):
        seqs = [j * G + q for q in range(G)]
        stack = lambda ref: jnp.concatenate([by_head(ref, i) for i in seqs], axis=0)
        r, lw, k, v, a, b = (stack(ref) for ref in (r_ref, lw_ref, k_ref, v_ref, a_ref, b_ref))
        s0 = jnp.concatenate([s_scr[i] for i in seqs], axis=0)
        hi = lw.astype(bf16)
        r1 = lw - hi.astype(f32)
        mid = r1.astype(bf16)
        lo = (r1 - mid.astype(f32)).astype(bf16)
        lc = _bmm(tri, hi) + _bmm(tri, mid) + _bmm(tri, lo)
        e_pos = jnp.exp(lc)
        e_neg = jnp.exp(-lc)
        a_t = (a * jnp.exp(lc - lw)).astype(bf16)
        r_t = (r * e_pos).astype(bf16)
        b_t = (b * e_neg).astype(bf16)
        k_t = (k * e_neg).astype(bf16)
        v_b = v.astype(bf16)
        s0_b = s0.astype(bf16)

        a_ab = jnp.where(strict, _bmm_nt(a_t, b_t), 0.0)
        a_ak = jnp.where(strict, _bmm_nt(a_t, k_t), 0.0)
        a_rb = jnp.where(incl, _bmm_nt(r_t, b_t), 0.0)
        a_rk = jnp.where(incl, _bmm_nt(r_t, k_t), 0.0)

        x = _bmm_nt(a_t, s0_b) + _bmm(a_ak.astype(bf16), v_b)
        p = a_ab
        n = 1
        while n < C:
            p_b = p.astype(bf16)
            x = x + _bmm(p_b, x.astype(bf16))
            n *= 2
            if n < C:
                p = _bmm(p_b, p_b)
        u_b = x.astype(bf16)
        o = _bmm_nt(r_t, s0_b) + _bmm(a_rb.astype(bf16), u_b) + _bmm(a_rk.astype(bf16), v_b)
        s1 = s0 + _bmm_tn(u_b, b_t) + _bmm_tn(v_b, k_t)
        s1 = s1 * e_pos[:, C - 1:C, :]
        mean = jnp.mean(o, -1, keepdims=True)
        oc = o - mean
        var = jnp.mean(oc * oc, -1, keepdims=True)
        on = oc * lax.rsqrt(var + GN_EPS)
        for q, i in enumerate(seqs):
            s_scr[i] = s1[q * H:(q + 1) * H]
            on_i = jnp.concatenate([on[q * H + h] for h in range(H)], axis=1)[:seq_rows]
            o_ref[i] = (on_i * gng_ref[...] + gnb_ref[...] + bonus_ref[i]) * g_ref[i]
        return carry

    lax.fori_loop(0, n_seq // G, one_seq, 0)

    @pl.when(c == pl.num_programs(1) - 1)
    def _():
        sT_ref[...] = s_scr[...]


def rwkv_recurrence(s0, seqs, gn_g, gn_b, *, n_batch, n_chunks, seq_rows, n_seq, chunk, group=1):
    h, n = RWKV_HEADS, RWKV_HEAD_DIM
    w = RWKV_WIDTH
    seq_spec = pl.BlockSpec((n_seq, seq_rows, w), lambda i, c: (i * n_chunks + c, 0, 0))
    st_spec = pl.BlockSpec((n_seq, h, n, n), lambda i, c: (i, 0, 0, 0))
    vec_spec = pl.BlockSpec((1, w), lambda i, c: (0, 0))
    o, s_t = pl.pallas_call(
        functools.partial(_rwkv_chunk_kernel, n_seq=n_seq, seq_rows=seq_rows, chunk=chunk, group=group),
        out_shape=(jax.ShapeDtypeStruct((n_batch * n_chunks, seq_rows, w), jnp.float32),
                   jax.ShapeDtypeStruct((n_batch, h, n, n), jnp.float32)),
        grid=(n_batch // n_seq, n_chunks),
        in_specs=[seq_spec] * 8 + [vec_spec, vec_spec, st_spec],
        out_specs=(seq_spec, st_spec),
        scratch_shapes=[pltpu.VMEM((n_seq, h, n, n), jnp.float32)],
        compiler_params=pltpu.CompilerParams(
            dimension_semantics=("arbitrary", "arbitrary"), vmem_limit_bytes=VMEM_LIMIT_BYTES),
        name="rwkv_chunk",
    )(*seqs, gn_g.reshape(1, w), gn_b.reshape(1, w), s0)
    return o, s_t


def _moe_scatter_kernel(slot_ref, h_ref, init_hbm, out_hbm, stage, sems):
    del init_hbm
    i = pl.program_id(0)
    nt = pl.num_programs(0)
    tt = h_ref.shape[0]
    slot = i % 2

    def wait_buffer(s):
        for _ in range(TOP_K):
            pltpu.make_async_copy(stage.at[s], out_hbm.at[pl.ds(0, tt)], sems.at[s]).wait()

    @pl.when(i >= 2)
    def _():
        wait_buffer(slot)

    stage[slot] = h_ref[...]
    for j in range(tt):
        for k in range(TOP_K):
            dst = slot_ref[(i * tt + j) * TOP_K + k]
            pltpu.make_async_copy(stage.at[slot, pl.ds(j, 1)], out_hbm.at[pl.ds(dst, 1)], sems.at[slot]).start()

    @pl.when(i == nt - 1)
    def _():
        wait_buffer(slot)

        @pl.when(nt >= 2)
        def _():
            wait_buffer(1 - slot)


def _moe_scatter_rows(h, pair_slot, p_pad):
    n, d = h.shape
    tt = MOE_ROUTE_TILE
    return pl.pallas_call(
        _moe_scatter_kernel,
        out_shape=jax.ShapeDtypeStruct((p_pad, d), jnp.float32),
        grid_spec=pltpu.PrefetchScalarGridSpec(
            num_scalar_prefetch=1,
            grid=(n // tt,),
            in_specs=[pl.BlockSpec((tt, d), lambda i, ps: (i, 0)),
                      pl.BlockSpec(memory_space=pl.ANY)],
            out_specs=pl.BlockSpec(memory_space=pl.ANY),
            scratch_shapes=[pltpu.VMEM((2, tt, d), jnp.float32), pltpu.SemaphoreType.DMA((2,))],
        ),
        input_output_aliases={2: 0},
        compiler_params=pltpu.CompilerParams(
            dimension_semantics=("arbitrary",), vmem_limit_bytes=VMEM_LIMIT_BYTES),
        name="moe_scatter",
    )(pair_slot.reshape(-1), h, jnp.zeros((p_pad, d), jnp.float32))


def _weight_copies(w_hbm, stage, sem, expert, col0, width, slot, n_parts, part_stride):
    return [pltpu.make_async_copy(w_hbm.at[expert, :, pl.ds(pl.multiple_of(col0 + p * part_stride, LANES), width)],
                                  stage.at[slot, p], sem.at[slot]) for p in range(n_parts)]


def _stream_expert_weights(sched_ref, w_hbm, stage, sem, w_bf, *, width, n_parts, part_stride):
    s = pl.program_id(0) * pl.num_programs(1) + pl.program_id(1)
    field = lambda j: sched_ref[s * SCHED_COLS + j]

    @pl.when(field(0) != 0)
    def _():
        slot = field(1)
        cur = _weight_copies(w_hbm, stage, sem, field(2), field(3), width, slot, n_parts, part_stride)

        @pl.when(s == 0)
        def _():
            for cp in cur:
                cp.start()

        for cp in cur:
            cp.wait()
        for p in range(n_parts):
            w_bf[p] = stage[slot, p].astype(jnp.bfloat16)

        @pl.when(field(4) != 0)
        def _():
            for cp in _weight_copies(w_hbm, stage, sem, field(5), field(6), width, 1 - slot, n_parts, part_stride):
                cp.start()


def _weight_schedule(tile_expert, n_chunks, chunk_width):
    nt = tile_expert.shape[0]
    n = n_chunks * nt
    expert = jnp.tile(tile_expert, n_chunks)
    col0 = jnp.repeat(jnp.arange(n_chunks, dtype=jnp.int32) * chunk_width, nt)
    step = jnp.arange(n, dtype=jnp.int32)
    start = (step % nt == 0) | (expert != jnp.roll(expert, 1))
    slot = (jnp.cumsum(start.astype(jnp.int32)) - 1) % 2
    nxt = lax.cummin(jnp.where(start, step, n), axis=0, reverse=True)
    nxt = jnp.concatenate([nxt[1:], jnp.full((1,), n, jnp.int32)])
    has_next = nxt < n
    nxt_c = jnp.minimum(nxt, n - 1)
    return jnp.stack([start.astype(jnp.int32), slot, expert, col0, has_next.astype(jnp.int32),
                      expert[nxt_c], col0[nxt_c]], axis=1).astype(jnp.int32).reshape(-1)


def _moe_up_kernel(sched_ref, tile_valid_ref, x_ref, bg_ref, bu_ref, w_hbm, h_ref, stage, sem, w_bf):
    f = w_hbm.shape[2] // 2
    _stream_expert_weights(sched_ref, w_hbm, stage, sem, w_bf, width=h_ref.shape[1], n_parts=2, part_stride=f)
    i = pl.program_id(1)

    @pl.when(tile_valid_ref[i] != 0)
    def _():
        x = x_ref[...].astype(jnp.bfloat16)
        hg = jnp.dot(x, w_bf[0], preferred_element_type=jnp.float32) + bg_ref[0]
        hu = jnp.dot(x, w_bf[1], preferred_element_type=jnp.float32) + bu_ref[0]
        glu = jnp.minimum(hg, SWIGLU_LIMIT)
        up = jnp.clip(hu, -SWIGLU_LIMIT, SWIGLU_LIMIT)
        act = (up + 1.0) * glu * jax.nn.sigmoid(SWIGLU_ALPHA * glu)
        h_ref[...] = act.astype(h_ref.dtype)

    @pl.when(tile_valid_ref[i] == 0)
    def _():
        h_ref[...] = jnp.zeros_like(h_ref)


def _moe_down_kernel(sched_ref, tile_valid_ref, h_ref, bd_ref, w_hbm, y_ref, stage, sem, w_bf):
    _stream_expert_weights(sched_ref, w_hbm, stage, sem, w_bf, width=y_ref.shape[1], n_parts=1, part_stride=0)
    i = pl.program_id(1)

    @pl.when(tile_valid_ref[i] != 0)
    def _():
        y_ref[...] = jnp.dot(h_ref[...], w_bf[0], preferred_element_type=jnp.float32) + bd_ref[0]

    @pl.when(tile_valid_ref[i] == 0)
    def _():
        y_ref[...] = jnp.zeros_like(y_ref)


def _moe_grouped(x_sorted, tile_expert, tile_valid, w_gate_up, b_gate_up, w_down, b_down):
    p_pad, d = x_sorted.shape
    n_exp, _, two_f = w_gate_up.shape
    f = two_f // 2
    tm, fc, dc = MOE_TILE_M, MOE_F_CHUNK, MOE_D_CHUNK
    nt = p_pad // tm
    nfc = f // fc
    ndc = d // dc
    bgu = b_gate_up.reshape(n_exp, 1, two_f)
    bdn = b_down.reshape(n_exp, 1, d)
    params = pltpu.CompilerParams(dimension_semantics=("arbitrary", "arbitrary"),
                                  vmem_limit_bytes=MOE_VMEM_LIMIT_BYTES)
    te_of = lambda sched, c, i: sched[(c * nt + i) * SCHED_COLS + 2]

    h = pl.pallas_call(
        _moe_up_kernel,
        out_shape=jax.ShapeDtypeStruct((p_pad, f), jnp.bfloat16),
        grid_spec=pltpu.PrefetchScalarGridSpec(
            num_scalar_prefetch=2,
            grid=(nfc, nt),
            in_specs=[
                pl.BlockSpec((tm, d), lambda c, i, sc, tv: (i, 0)),
                pl.BlockSpec((1, 1, fc), lambda c, i, sc, tv: (te_of(sc, c, i), 0, c)),
                pl.BlockSpec((1, 1, fc), lambda c, i, sc, tv: (te_of(sc, c, i), 0, nfc + c)),
                pl.BlockSpec(memory_space=pl.ANY),
            ],
            out_specs=pl.BlockSpec((tm, fc), lambda c, i, sc, tv: (i, c)),
            scratch_shapes=[pltpu.VMEM((2, 2, d, fc), jnp.float32), pltpu.SemaphoreType.DMA((2,)),
                            pltpu.VMEM((2, d, fc), jnp.bfloat16)],
        ),
        compiler_params=params,
        name="moe_up",
    )(_weight_schedule(tile_expert, nfc, fc), tile_valid, x_sorted, bgu, bgu, w_gate_up)

    return pl.pallas_call(
        _moe_down_kernel,
        out_shape=jax.ShapeDtypeStruct((p_pad, d), jnp.float32),
        grid_spec=pltpu.PrefetchScalarGridSpec(
            num_scalar_prefetch=2,
            grid=(ndc, nt),
            in_specs=[
                pl.BlockSpec((tm, f), lambda c, i, sc, tv: (i, 0)),
                pl.BlockSpec((1, 1, dc), lambda c, i, sc, tv: (te_of(sc, c, i), 0, c)),
                pl.BlockSpec(memory_space=pl.ANY),
            ],
            out_specs=pl.BlockSpec((tm, dc), lambda c, i, sc, tv: (i, c)),
            scratch_shapes=[pltpu.VMEM((2, 1, f, dc), jnp.float32), pltpu.SemaphoreType.DMA((2,)),
                            pltpu.VMEM((1, f, dc), jnp.bfloat16)],
        ),
        compiler_params=params,
        name="moe_down",
    )(_weight_schedule(tile_expert, ndc, dc), tile_valid, h, bdn, w_down)


def _moe_combine_kernel(slot_ref, h_ref, w_ref, g_ref, b_ref, y_hbm, out_ref, buf, sems):
    i = pl.program_id(0)
    nt = pl.num_programs(0)
    tt = h_ref.shape[0]

    def start_tile(t, s):
        for j in range(tt):
            for k in range(TOP_K):
                src = slot_ref[(t * tt + j) * TOP_K + k]
                pltpu.make_async_copy(y_hbm.at[pl.ds(src, 1)], buf.at[s, pl.ds(k * tt + j, 1)], sems.at[s]).start()

    @pl.when(i == 0)
    def _():
        start_tile(0, 0)

    @pl.when(i + 1 < nt)
    def _():
        start_tile(i + 1, (i + 1) % 2)

    slot = i % 2
    pltpu.make_async_copy(y_hbm.at[pl.ds(0, TOP_K * tt)], buf.at[slot], sems.at[slot]).wait()
    w = w_ref[...]
    ffn = buf[slot, 0:tt] * w[:, 0:1]
    for k in range(1, TOP_K):
        ffn = ffn + buf[slot, k * tt:(k + 1) * tt] * w[:, k:k + 1]
    x = DEEPNORM_ALPHA * h_ref[...] + ffn
    mu = jnp.mean(x, -1, keepdims=True)
    xc = x - mu
    var = jnp.mean(xc * xc, -1, keepdims=True)
    out_ref[...] = xc * lax.rsqrt(var + LN_EPS) * g_ref[...] + b_ref[...]


def _moe_combine_ln(h, y, pair_slot, top_w, ln_g, ln_b):
    n, d = h.shape
    tt = MOE_ROUTE_TILE
    return pl.pallas_call(
        _moe_combine_kernel,
        out_shape=jax.ShapeDtypeStruct((n, d), jnp.float32),
        grid_spec=pltpu.PrefetchScalarGridSpec(
            num_scalar_prefetch=1,
            grid=(n // tt,),
            in_specs=[pl.BlockSpec((tt, d), lambda i, ps: (i, 0)),
                      pl.BlockSpec((tt, TOP_K), lambda i, ps: (i, 0)),
                      pl.BlockSpec((1, d), lambda i, ps: (0, 0)),
                      pl.BlockSpec((1, d), lambda i, ps: (0, 0)),
                      pl.BlockSpec(memory_space=pl.ANY)],
            out_specs=pl.BlockSpec((tt, d), lambda i, ps: (i, 0)),
            scratch_shapes=[pltpu.VMEM((2, TOP_K * tt, d), jnp.float32), pltpu.SemaphoreType.DMA((2,))],
        ),
        compiler_params=pltpu.CompilerParams(
            dimension_semantics=("arbitrary",), vmem_limit_bytes=VMEM_LIMIT_BYTES),
        name="moe_combine",
    )(pair_slot.reshape(-1), h, top_w, ln_g.reshape(1, d), ln_b.reshape(1, d), y)


def moe_ffn_ln(x, top_w, top_idx, w_gate_up, b_gate_up, w_down, b_down, ln_g, ln_b):
    n, d = x.shape
    tm = MOE_TILE_M
    nt = n * TOP_K // tm + N_EXPERTS
    onehot = (top_idx[:, :, None] == jnp.arange(N_EXPERTS, dtype=top_idx.dtype)).astype(jnp.int32)
    routed = jnp.sum(onehot, axis=1)
    before = jnp.cumsum(routed, axis=0) - routed
    counts = before[-1] + routed[-1]
    tiles_per = (counts + tm - 1) // tm
    tile_end = jnp.cumsum(tiles_per)
    first_row = (tile_end - tiles_per) * tm
    pair_slot = jnp.sum(onehot * (before + first_row)[:, None, :], axis=2)
    tile_ids = jnp.arange(nt, dtype=jnp.int32)
    tile_expert = jnp.minimum(jnp.sum((tile_ids[:, None] >= tile_end[None, :]).astype(jnp.int32), axis=1),
                              N_EXPERTS - 1).astype(jnp.int32)
    tile_valid = (tile_ids < tile_end[-1]).astype(jnp.int32)

    x_sorted = _moe_scatter_rows(x, pair_slot, nt * tm)
    y = _moe_grouped(x_sorted, tile_expert, tile_valid, w_gate_up, b_gate_up, w_down, b_down)
    return _moe_combine_ln(x, y, pair_slot, top_w, ln_g, ln_b)


def kernel(x_prompt, x_sample, cache_win0_kv, cache_win1_kv, cache_win2_kv, cache_mem_kv,
           state_wkv, state_shift, mem_prompt,
           w_in, w_gate, b_gate, w_branch, w_out, w_mem_kv,
           mu_rkv, mu_wag, w0, w1, w2, a0, a1, a2, g1, g2, k_k, k_a, r_k, gn_g, gn_b,
           ln1_g, ln1_b, w_router, b_router, w_gate_up, b_gate_up, w_down, b_down, ln2_g, ln2_b):
    bf16 = jnp.bfloat16
    xp, xs = x_prompt, x_sample
    bp, tp, d = xp.shape
    bs, ts, _ = xs.shape
    n_prompt = bp * tp
    l = 0
    lp = {"mu_rkv": mu_rkv[l], "mu_wag": mu_wag[l],
          "w0": w0[l], "w1": w1[l], "w2": w2[l], "a0": a0[l], "a1": a1[l], "a2": a2[l],
          "g1": g1[l], "g2": g2[l], "k_k": k_k[l], "k_a": k_a[l], "r_k": r_k[l]}

    xp2, xs2 = xp.reshape(-1, d), xs.reshape(-1, d)
    u_all = jnp.concatenate([xp2, xs2], axis=0)
    w_in_bf = w_in[l].astype(bf16)
    proj_p2 = matmul_bias(xp2.astype(bf16), w_in_bf, tile_m=PROJ_PROMPT_TILE_M, name="proj_in")
    proj_s2 = matmul_bias(xs2.astype(bf16), w_in_bf, tile_m=xs2.shape[0], name="proj_in_step")
    gate = matmul_bias(u_all.astype(bf16), w_gate[l].astype(bf16), b_gate[l], tile_m=PROJ_TILE_M, sigmoid=True,
                       name="proj_gate")
    mem_rows = mem_prompt.reshape(-1, d)
    mem_kv_p = matmul_bias(mem_rows.astype(bf16), w_mem_kv[l].astype(bf16), tile_m=mem_rows.shape[0], name="proj_mem")
    proj_p = proj_p2.reshape(bp, tp, PROJ_WIDTH)
    proj_s = proj_s2.reshape(bs, ts, PROJ_WIDTH)
    rkv0, qm0 = 3 * ATT_WIDTH, 3 * ATT_WIDTH + 3 * RWKV_WIDTH

    heads = lambda t: t.reshape(t.shape[0], t.shape[1], N_ATT_HEADS, HEAD_DIM)
    ka_p, va_p = (heads(proj_p[..., i * ATT_WIDTH:(i + 1) * ATT_WIDTH]) for i in (1, 2))
    ka_s, va_s = (heads(proj_s[..., i * ATT_WIDTH:(i + 1) * ATT_WIDTH]) for i in (1, 2))
    outs, lses, win_p, win_s = [], [], [], []
    for g, (win, dil) in enumerate(DIL_GROUPS):
        hs = slice(g * HEADS_PER_GROUP, (g + 1) * HEADS_PER_GROUP)
        o, lse = dilated_attention_prompt(proj_p2, bp, tp, g, dil)
        outs.append(o)
        lses.append(lse)
        keep = min(win, tp)
        win_p.append(jnp.stack([ka_p[:, tp - keep:, hs], va_p[:, tp - keep:, hs]], axis=2))
        win_s.append(jnp.stack([ka_s[:, :, hs], va_s[:, :, hs]], axis=2))
    o_att_p = att_combine(outs, lses)
    o_att_s = dilated_attention_step(proj_s, cache_win0_kv[l], cache_win1_kv[l], cache_win2_kv[l])

    tq = MEM_ATTN_TILE_Q
    o_mem_p = memory_attention(proj_p2.reshape(-1, tq, PROJ_WIDTH), qm0 // MEM_WIDTH,
                               mem_kv_p.reshape(bp, MEM_TOKENS, 2 * MEM_WIDTH), tp // tq)
    o_mem_s = memory_attention(proj_s, qm0 // MEM_WIDTH, cache_mem_kv[l], 1, n_seq=MEM_ATTN_STEP_SEQS)

    tm = RWKV_PRE_TILE_M

    def tile_prev_rows(x):
        last = x[:, tm - 1::tm][:, :-1]
        return jnp.concatenate([jnp.zeros_like(last[:, :1]), last], axis=1).reshape(-1, 1, x.shape[-1])

    rwkv_in_p = rwkv_pre(xp2, tile_prev_rows(xp), proj_p2, tile_prev_rows(proj_p[..., rkv0:qm0]), lp, rolled=True)
    u_last_s = state_shift[l].astype(xs.dtype)
    p_last_s = matmul_bias(u_last_s.astype(bf16), w_in_bf[:, rkv0:qm0], tile_m=bs, name="proj_shift")
    shift = lambda first, rest: jnp.concatenate([first[:, None], rest[:, :-1]], axis=1).reshape(-1, rest.shape[-1])
    rwkv_in_s = rwkv_pre(xs2, shift(u_last_s, xs), proj_s2, shift(p_last_s, proj_s[..., rkv0:qm0]), lp, rolled=False)
    s0 = jnp.zeros((bp, RWKV_HEADS, RWKV_HEAD_DIM, RWKV_HEAD_DIM), jnp.float32)
    o_rwkv_p, wkv_p = rwkv_recurrence(
        s0, [x.reshape(-1, RWKV_CHUNK, RWKV_WIDTH) for x in rwkv_in_p], gn_g[l], gn_b[l],
        n_batch=bp, n_chunks=tp // RWKV_CHUNK, seq_rows=RWKV_CHUNK, n_seq=1, chunk=RWKV_CHUNK)
    o_rwkv_s, wkv_s = rwkv_recurrence(
        state_wkv[l].astype(jnp.float32), [x.reshape(bs, ts, RWKV_WIDTH) for x in rwkv_in_s], gn_g[l], gn_b[l],
        n_batch=bs, n_chunks=1, seq_rows=ts, n_seq=RWKV_SEQ_PER_STEP, chunk=-(-ts // SUBLANES) * SUBLANES,
        group=RWKV_SEQ_GROUP)

    cat = lambda a, b, w: jnp.concatenate([a.reshape(-1, w), b.reshape(-1, w)], axis=0)
    h, top_w, top_idx = merge_ln_router(
        cat(o_att_p, o_att_s, ATT_OUT), cat(o_rwkv_p, o_rwkv_s, RWKV_WIDTH), cat(o_mem_p, o_mem_s, MEM_WIDTH),
        gate, u_all, w_branch[l].astype(bf16), w_out[l].astype(bf16), ln1_g[l], ln1_b[l], w_router[l], b_router[l])
    y_all = moe_ffn_ln(h, top_w, top_idx, w_gate_up[l], b_gate_up[l], w_down[l], b_down[l], ln2_g[l], ln2_b[l])
    yp = y_all[:n_prompt].reshape(bp, tp, d)
    ys = y_all[n_prompt:].reshape(bs, ts, d)
    mem_kv_out = mem_kv_p.reshape(bp, MEM_TOKENS, 2, MEM_HEADS, MEM_HEAD_DIM)
    return (yp, ys,
            win_p[0][None], win_p[1][None], win_p[2][None], wkv_p.astype(xp.dtype)[None],
            xp[:, -1][None], mem_kv_out[None],
            win_s[0].astype(cache_win0_kv.dtype)[None], win_s[1].astype(cache_win0_kv.dtype)[None],
            win_s[2].astype(cache_win0_kv.dtype)[None], wkv_s.astype(state_wkv.dtype)[None],
            xs[:, -1].astype(state_shift.dtype)[None])
```

```python
import functools

import jax
import jax.numpy as jnp
from jax import lax
from jax.experimental import pallas as pl
from jax.experimental.pallas import tpu as pltpu

D_MODEL = 2048
DEPTH = 1
HEAD_DIM = 64
DIL_GROUPS = ((128, 1), (512, 4), (2048, 16))
HEADS_PER_GROUP = 4
N_ATT_HEADS = HEADS_PER_GROUP * len(DIL_GROUPS)
ATT_WIDTH = N_ATT_HEADS * HEAD_DIM
ATT_OUT = HEADS_PER_GROUP * HEAD_DIM
BAND_BLOCK = 128
N_BACK = 128
GROUP_WIDTH = HEADS_PER_GROUP * HEAD_DIM
RWKV_HEADS = 12
RWKV_HEAD_DIM = 64
RWKV_WIDTH = RWKV_HEADS * RWKV_HEAD_DIM
GN_EPS = 64e-5
MEM_TOKENS = 256
MEM_HEADS = 4
MEM_HEAD_DIM = 128
MEM_WIDTH = MEM_HEADS * MEM_HEAD_DIM
PROJ_WIDTH = 3 * ATT_WIDTH + 3 * RWKV_WIDTH + MEM_WIDTH
N_BRANCHES = 3
N_EXPERTS = 32
TOP_K = 4
D_EXPERT = 2048
SWIGLU_LIMIT = 7.0
SWIGLU_ALPHA = 1.702
LN_EPS = 1e-5
DEEPNORM_ALPHA = (2.0 * DEPTH) ** 0.25
NEG_INF = -1e30

VMEM_LIMIT_BYTES = 56 * 1024 * 1024
SUBLANES = 8
LANES = 128
MOE_TILE_M = 256
MOE_F_CHUNK = 1024
MOE_D_CHUNK = 1024
MOE_ROUTE_TILE = 64
MOE_VMEM_LIMIT_BYTES = 58 * 1024 * 1024
SCHED_COLS = 7
RWKV_CHUNK = 64
RWKV_SEQ_PER_STEP = 8
RWKV_SEQ_GROUP = 4
RWKV_PROMPT_SEQS = 2
RWKV_PRE_TILE_M = 256
PROJ_TILE_M = 1088
PROJ_PROMPT_TILE_M = 1024
MEM_ATTN_STEP_SEQS = 8
MATMUL_TILE_N = 512
MERGE_TILE_M = 256
ATT_COMBINE_TILE = 512
MEM_ATTN_TILE_Q = 512
ATTN_STEP_SEQS = 2


def _matmul_kernel(x_ref, w_ref, b_ref, o_ref, *, sigmoid):
    y = jnp.dot(x_ref[...], w_ref[...], preferred_element_type=jnp.float32) + b_ref[...]
    o_ref[...] = jax.nn.sigmoid(y) if sigmoid else y


def matmul_bias(x, w, bias=None, *, tile_m, sigmoid=False, name="matmul"):
    m, k = x.shape
    n = w.shape[1]
    tn = MATMUL_TILE_N if n % MATMUL_TILE_N == 0 else n
    if bias is None:
        bias = jnp.zeros((n,), jnp.float32)
    return pl.pallas_call(
        functools.partial(_matmul_kernel, sigmoid=sigmoid),
        out_shape=jax.ShapeDtypeStruct((m, n), jnp.float32),
        grid=(n // tn, m // tile_m),
        in_specs=[pl.BlockSpec((tile_m, k), lambda j, i: (i, 0)),
                  pl.BlockSpec((k, tn), lambda j, i: (0, j)),
                  pl.BlockSpec((1, tn), lambda j, i: (0, j))],
        out_specs=pl.BlockSpec((tile_m, tn), lambda j, i: (i, j)),
        compiler_params=pltpu.CompilerParams(
            dimension_semantics=("arbitrary", "arbitrary"), vmem_limit_bytes=VMEM_LIMIT_BYTES),
        name=name,
    )(x, w, bias.reshape(1, n))


def _merge_kernel(att_ref, rwkv_ref, mem_ref, gate_ref, x_ref, wb_ref, wo_ref, g_ref, b_ref,
                  wr_hi_ref, wr_lo_ref, br_ref, h_ref, topw_ref, topi_ref):
    bf16, f32 = jnp.bfloat16, jnp.float32
    d = x_ref.shape[1]
    dot = functools.partial(jnp.dot, preferred_element_type=f32)
    y = dot(att_ref[...].astype(bf16), wb_ref[0:ATT_OUT]) * gate_ref[:, 0:d]
    y = y + dot(rwkv_ref[...].astype(bf16), wb_ref[ATT_OUT:ATT_OUT + RWKV_WIDTH]) * gate_ref[:, d:2 * d]
    y = y + dot(mem_ref[...].astype(bf16), wb_ref[ATT_OUT + RWKV_WIDTH:]) * gate_ref[:, 2 * d:]
    z = DEEPNORM_ALPHA * x_ref[...] + dot(y.astype(bf16), wo_ref[...])
    mu = jnp.mean(z, -1, keepdims=True)
    zc = z - mu
    var = jnp.mean(zc * zc, -1, keepdims=True)
    h = zc * lax.rsqrt(var + LN_EPS) * g_ref[...] + b_ref[...]
    h_ref[...] = h

    h_hi = h.astype(bf16)
    h_lo = (h - h_hi.astype(f32)).astype(bf16)
    logits = (dot(h_hi, wr_hi_ref[...]) + dot(h_hi, wr_lo_ref[...]) + dot(h_lo, wr_hi_ref[...])) + br_ref[...]
    lane = lax.broadcasted_iota(jnp.int32, logits.shape, 1)
    work = jnp.where(lane < N_EXPERTS, logits, -jnp.inf)
    tops, ids = [], []
    for _ in range(TOP_K):
        m = jnp.max(work, -1, keepdims=True)
        idx = jnp.min(jnp.where(work == m, lane, LANES), -1, keepdims=True)
        tops.append(m)
        ids.append(idx)
        work = jnp.where(lane == idx, -jnp.inf, work)
    e = [jnp.exp(t - tops[0]) for t in tops]
    den = e[0] + e[1] + e[2] + e[3]
    tw = jnp.zeros(logits.shape, f32)
    ti = jnp.zeros(logits.shape, jnp.int32)
    for k in range(TOP_K):
        tw = jnp.where(lane == k, e[k] / den, tw)
        ti = jnp.where(lane == k, ids[k], ti)
    topw_ref[...] = tw
    topi_ref[...] = ti


def merge_ln_router(o_att, o_rwkv, o_mem, gate, x, w_branch, w_out, ln_g, ln_b, w_router, b_router):
    n, d = x.shape
    tm = MERGE_TILE_M
    wr = jnp.zeros((d, LANES), jnp.float32).at[:, :N_EXPERTS].set(w_router)
    wr_hi = wr.astype(jnp.bfloat16)
    wr_lo = (wr - wr_hi.astype(jnp.float32)).astype(jnp.bfloat16)
    br = jnp.zeros((1, LANES), jnp.float32).at[0, :N_EXPERTS].set(b_router)
    rows = lambda w: pl.BlockSpec((tm, w), lambda i: (i, 0))
    full = lambda a: pl.BlockSpec(a.shape, lambda i: (0, 0))
    h, topw, topi = pl.pallas_call(
        _merge_kernel,
        out_shape=(jax.ShapeDtypeStruct((n, d), jnp.float32),
                   jax.ShapeDtypeStruct((n, LANES), jnp.float32),
                   jax.ShapeDtypeStruct((n, LANES), jnp.int32)),
        grid=(n // tm,),
        in_specs=[rows(ATT_OUT), rows(RWKV_WIDTH), rows(MEM_WIDTH), rows(3 * d), rows(d),
                  full(w_branch), full(w_out), pl.BlockSpec((1, d), lambda i: (0, 0)),
                  pl.BlockSpec((1, d), lambda i: (0, 0)), full(wr_hi), full(wr_lo), full(br)],
        out_specs=(rows(d), rows(LANES), rows(LANES)),
        compiler_params=pltpu.CompilerParams(
            dimension_semantics=("arbitrary",), vmem_limit_bytes=VMEM_LIMIT_BYTES),
        name="merge_ln_router",
    )(o_att, o_rwkv, o_mem, gate, x, w_branch, w_out, ln_g.reshape(1, d), ln_b.reshape(1, d), wr_hi, wr_lo, br)
    return h, topw[:, :TOP_K], topi[:, :TOP_K]


def _alibi_slope(head):
    return 2.0 ** (-8.0 * (head + 1) / N_ATT_HEADS)


def _dot_nt(a, b):
    return lax.dot_general(a, b, (((1,), (1,)), ((), ())), preferred_element_type=jnp.float32)


def _dil_attn_prompt_kernel(*refs, group, dil):
    bf16 = jnp.bfloat16
    blk = BAND_BLOCK
    halves = GROUP_WIDTH // LANES
    ins = [refs[5 * a:5 * a + 5] for a in range(halves)]
    o_refs = refs[5 * halves:5 * halves + halves]
    lse_refs = refs[6 * halves:]
    i = pl.program_id(1)
    qi = lax.broadcasted_iota(jnp.int32, (blk, blk), 0)
    kj = lax.broadcasted_iota(jnp.int32, (blk, blk), 1)
    dist_c = qi - kj
    dist_p = dist_c + blk
    valid_c = dist_c >= 0
    valid_p = (dist_p <= blk) & (i > 0)
    scale = HEAD_DIM ** -0.5
    per_half = LANES // HEAD_DIM
    rows = lambda ref, r: ref[pl.ds(r, blk, stride=dil), :] if dil > 1 else ref[...]
    for r in range(dil):
        x = [[rows(ref, r).astype(bf16) for ref in ins[a]] for a in range(halves)]
        part = lambda a, j, hh: x[a][j][:, hh * HEAD_DIM:(hh + 1) * HEAD_DIM]
        units = [(a, hh) for a in range(halves) for hh in range(per_half)]
        st = []
        for a, hh in units:
            slope = _alibi_slope(group * HEADS_PER_GROUP + a * per_half + hh) * dil
            s_c = _dot_nt(part(a, 0, hh), part(a, 1, hh)) * scale - slope * dist_c.astype(jnp.float32)
            s_p = _dot_nt(part(a, 0, hh), part(a, 2, hh)) * scale - slope * dist_p.astype(jnp.float32)
            st.append((jnp.where(valid_c, s_c, NEG_INF), jnp.where(valid_p, s_p, NEG_INF)))
        st = [(s_c, s_p, jnp.maximum(jnp.max(s_c, -1, keepdims=True), jnp.max(s_p, -1, keepdims=True))) for s_c, s_p in st]
        st = [(jnp.exp(s_c - m), jnp.exp(s_p - m), m) for s_c, s_p, m in st]
        st = [(p_c, p_p, m, jnp.sum(p_c, -1, keepdims=True) + jnp.sum(p_p, -1, keepdims=True)) for p_c, p_p, m in st]
        outs, lses = [], []
        for (a, hh), (p_c, p_p, m, den) in zip(units, st):
            pv = (jnp.dot(p_c.astype(bf16), part(a, 3, hh), preferred_element_type=jnp.float32)
                  + jnp.dot(p_p.astype(bf16), part(a, 4, hh), preferred_element_type=jnp.float32))
            outs.append(pv / den)
            lses.append(jnp.broadcast_to(m + jnp.log(den), (blk, HEAD_DIM)))
        for a in range(halves):
            o_a = jnp.concatenate(outs[a * per_half:(a + 1) * per_half], axis=1)
            l_a = jnp.concatenate(lses[a * per_half:(a + 1) * per_half], axis=1)
            if dil > 1:
                o_refs[a][pl.ds(r, blk, stride=dil), :] = o_a
                lse_refs[a][pl.ds(r, blk, stride=dil), :] = l_a
            else:
                o_refs[a][...] = o_a
                lse_refs[a][...] = l_a


def dilated_attention_prompt(proj, b, t, group, dil):
    rows = BAND_BLOCK * dil
    halves = GROUP_WIDTH // LANES
    assert t % rows == 0
    nblk = t // rows
    col = lambda part, a: (part * ATT_WIDTH + group * GROUP_WIDTH) // LANES + a
    cur = lambda c: pl.BlockSpec((rows, LANES), lambda n, i: (n * nblk + i, c))
    prev = lambda c: pl.BlockSpec((rows, LANES), lambda n, i: (n * nblk + jnp.maximum(i - 1, 0), c))
    in_specs = []
    for a in range(halves):
        in_specs += [cur(col(0, a)), cur(col(1, a)), prev(col(1, a)), cur(col(2, a)), prev(col(2, a))]
    out_shape = jax.ShapeDtypeStruct((b * t, LANES), jnp.float32)
    res = pl.pallas_call(
        functools.partial(_dil_attn_prompt_kernel, group=group, dil=dil),
        out_shape=(out_shape,) * (2 * halves),
        grid=(b, nblk),
        in_specs=in_specs,
        out_specs=(cur(0),) * (2 * halves),
        compiler_params=pltpu.CompilerParams(
            dimension_semantics=("arbitrary", "arbitrary"), vmem_limit_bytes=VMEM_LIMIT_BYTES),
        name="dil_attn_prompt_g%d" % group,
    )(*([proj] * (5 * halves)))
    return jnp.concatenate(res[:halves], axis=1), jnp.concatenate(res[halves:], axis=1)


def _att_combine_kernel(o0, o1, o2, l0, l1, l2, out_ref):
    a, b, c = l0[...], l1[...], l2[...]
    m = jnp.maximum(jnp.maximum(a, b), c)
    ea, eb, ec = jnp.exp(a - m), jnp.exp(b - m), jnp.exp(c - m)
    out_ref[...] = (ea * o0[...] + eb * o1[...] + ec * o2[...]) / (ea + eb + ec)


def att_combine(outs, lses):
    n, w = outs[0].shape
    spec = pl.BlockSpec((ATT_COMBINE_TILE, w), lambda i: (i, 0))
    return pl.pallas_call(
        _att_combine_kernel,
        out_shape=jax.ShapeDtypeStruct((n, w), jnp.float32),
        grid=(n // ATT_COMBINE_TILE,),
        in_specs=[spec] * 6,
        out_specs=spec,
        compiler_params=pltpu.CompilerParams(dimension_semantics=("arbitrary",)),
        name="att_combine",
    )(*outs, *lses)


def _mem_attn_kernel(q_ref, kv_ref, o_ref):
    bf16 = jnp.bfloat16
    e = MEM_HEAD_DIM
    scale = e ** -0.5
    units = [(n, h) for n in range(q_ref.shape[0]) for h in range(MEM_HEADS)]
    if len(kv_ref.shape) == 3:
        key = lambda n, h: kv_ref[n, :, h * e:(h + 1) * e]
        val = lambda n, h: kv_ref[n, :, (MEM_HEADS + h) * e:(MEM_HEADS + h + 1) * e]
    else:
        key = lambda n, h: kv_ref[n, :, 0, h, :]
        val = lambda n, h: kv_ref[n, :, 1, h, :]
    s = [_dot_nt(q_ref[n, :, h * e:(h + 1) * e].astype(bf16), key(n, h).astype(bf16)) * scale for n, h in units]
    m = [jnp.max(x, -1, keepdims=True) for x in s]
    p = [jnp.exp(x - mx) for x, mx in zip(s, m)]
    p = [x / jnp.sum(x, -1, keepdims=True) for x in p]
    for (n, h), x in zip(units, p):
        o_ref[n, :, h * e:(h + 1) * e] = jnp.dot(x.astype(bf16), val(n, h).astype(bf16),
                                                 preferred_element_type=jnp.float32)


def memory_attention(q, q_col_block, mem_kv, blocks_per_seq, n_seq=1):
    b = mem_kv.shape[0]
    tq = q.shape[1]
    w = MEM_WIDTH
    assert n_seq == 1 or blocks_per_seq == 1
    kv_zeros = (0,) * (mem_kv.ndim - 1)
    return pl.pallas_call(
        _mem_attn_kernel,
        out_shape=jax.ShapeDtypeStruct((b * blocks_per_seq, tq, w), jnp.float32),
        grid=(b // n_seq, blocks_per_seq),
        in_specs=[pl.BlockSpec((n_seq, tq, w), lambda n, i: (n * blocks_per_seq + i, 0, q_col_block)),
                  pl.BlockSpec((n_seq,) + mem_kv.shape[1:], lambda n, i: (n,) + kv_zeros)],
        out_specs=pl.BlockSpec((n_seq, tq, w), lambda n, i: (n * blocks_per_seq + i, 0, 0)),
        compiler_params=pltpu.CompilerParams(
            dimension_semantics=("arbitrary", "arbitrary"), vmem_limit_bytes=VMEM_LIMIT_BYTES),
        name="mem_attn",
    )(q, mem_kv)


def _dil_attn_step_kernel(qkv_ref, c0_ref, c1_ref, c2_ref, o_ref, *, n_new):
    bf16, f32 = jnp.bfloat16, jnp.float32
    scale = HEAD_DIM ** -0.5
    caches = (c0_ref, c1_ref, c2_ref)
    n_groups = len(DIL_GROUPS)
    tq = lax.broadcasted_iota(jnp.int32, (n_new, n_new), 0)
    tk = lax.broadcasted_iota(jnp.int32, (n_new, n_new), 1)
    slots = [(b, h) for b in range(qkv_ref.shape[0]) for h in range(HEADS_PER_GROUP)]
    units = [(b, h, g) for b, h in slots for g in range(n_groups)]
    new_cols = lambda b, part, head: qkv_ref[b, :, part * ATT_WIDTH + head * HEAD_DIM:
                                             part * ATT_WIDTH + (head + 1) * HEAD_DIM].astype(bf16)
    st = {}
    for b, h, g in units:
        win, dil = DIL_GROUPS[g]
        head = g * HEADS_PER_GROUP + h
        slope = _alibi_slope(head)
        q = new_cols(b, 0, head)
        s = jnp.dot(q, caches[g][b, 0, h].astype(bf16), preferred_element_type=f32) * scale
        dist = (win + lax.broadcasted_iota(jnp.int32, (n_new, win), 0)
                - lax.broadcasted_iota(jnp.int32, (n_new, win), 1))
        valid = ((dist & (dil - 1)) == 0) & (dist <= N_BACK * dil)
        s = jnp.where(valid, s - slope * dist.astype(f32), NEG_INF)
        sn = _dot_nt(q, new_cols(b, 1, head)) * scale
        dn = tq - tk
        sn = jnp.where((dn >= 0) & ((dn & (dil - 1)) == 0), sn - slope * dn.astype(f32), NEG_INF)
        st[b, h, g] = (s, sn)
    for u in units:
        s, sn = st[u]
        st[u] = (s, sn, jnp.maximum(jnp.max(s, axis=1, keepdims=True), jnp.max(sn, axis=1, keepdims=True)))
    for u in units:
        s, sn, m = st[u]
        p = jnp.exp(s - m)
        pn = jnp.exp(sn - m)
        st[u] = (p, pn, m, jnp.sum(p, axis=1, keepdims=True) + jnp.sum(pn, axis=1, keepdims=True))
    for b, h, g in units:
        p, pn, m, den = st[b, h, g]
        head = g * HEADS_PER_GROUP + h
        pv = (_dot_nt(p.astype(bf16), caches[g][b, 1, h].astype(bf16))
              + jnp.dot(pn.astype(bf16), new_cols(b, 2, head), preferred_element_type=f32))
        st[b, h, g] = (pv / den, m + jnp.log(den))
    for b, h in slots:
        outs = [st[b, h, g][0] for g in range(n_groups)]
        lses = [st[b, h, g][1] for g in range(n_groups)]
        mm = jnp.maximum(jnp.maximum(lses[0], lses[1]), lses[2])
        w = [jnp.exp(l - mm) for l in lses]
        o_ref[b, :, h * HEAD_DIM:(h + 1) * HEAD_DIM] = (
            (w[0] * outs[0] + w[1] * outs[1] + w[2] * outs[2]) / (w[0] + w[1] + w[2]))


def dilated_attention_step(p_s, cache0, cache1, cache2):
    b, t, _ = p_s.shape
    feat_major = lambda c: jnp.transpose(c, (0, 2, 3, 4, 1))
    ns = ATTN_STEP_SEQS
    win_spec = lambda w: pl.BlockSpec((ns, 2, HEADS_PER_GROUP, HEAD_DIM, w), lambda n: (n, 0, 0, 0, 0))
    return pl.pallas_call(
        functools.partial(_dil_attn_step_kernel, n_new=t),
        out_shape=jax.ShapeDtypeStruct((b, t, GROUP_WIDTH), jnp.float32),
        grid=(b // ns,),
        in_specs=[pl.BlockSpec((ns, t, 3 * ATT_WIDTH), lambda n: (n, 0, 0))] + [win_spec(w) for w, _ in DIL_GROUPS],
        out_specs=pl.BlockSpec((ns, t, GROUP_WIDTH), lambda n: (n, 0, 0)),
        compiler_params=pltpu.CompilerParams(
            dimension_semantics=("arbitrary",), vmem_limit_bytes=VMEM_LIMIT_BYTES),
        name="dil_attn_step",
    )(p_s, feat_major(cache0), feat_major(cache1), feat_major(cache2))


def _rwkv_pre_kernel(u_ref, up_ref, p_ref, pp_ref, mu_rkv_ref, mu_wag_ref, w0_ref, w1_ref, w2_ref,
                     a0_ref, a1_ref, a2_ref, g1_ref, g2_ref, kk_ref, ka_ref, rk_ref,
                     r_ref, lw_ref, k_ref, v_ref, an_ref, b_ref, g_ref, bonus_ref, *, rolled):
    bf16, f32 = jnp.bfloat16, jnp.float32
    dot = functools.partial(jnp.dot, preferred_element_type=f32)
    w = RWKV_WIDTH
    u = u_ref[...]
    p = p_ref[...]
    if rolled:
        first = lax.broadcasted_iota(jnp.int32, (u.shape[0], 1), 0) == 0
        u_prev = jnp.where(first, up_ref[0], pltpu.roll(u, 1, 0))
        p_prev = jnp.where(first, pp_ref[0], pltpu.roll(p, 1, 0))
    else:
        u_prev, p_prev = up_ref[...], pp_ref[...]
    du = u_prev - u
    rkv = p + (p_prev - p) * mu_rkv_ref[...]
    r, k, v = rkv[:, :w], rkv[:, w:2 * w], rkv[:, 2 * w:]
    xw = (u + du * mu_wag_ref[0:1]).astype(bf16)
    xa = (u + du * mu_wag_ref[1:2]).astype(bf16)
    xg = (u + du * mu_wag_ref[2:3]).astype(bf16)
    y = -(w0_ref[...] + dot(jnp.tanh(dot(xw, w1_ref[...])).astype(bf16), w2_ref[...]))
    softplus = jnp.maximum(y, 0.0) + jnp.log1p(jnp.exp(-jnp.abs(y)))
    lw_ref[...] = -jnp.exp(-softplus - 0.5)
    a = jax.nn.sigmoid(a0_ref[...] + dot(dot(xa, a1_ref[...]).astype(bf16), a2_ref[...]))
    g_ref[...] = dot(jax.nn.sigmoid(dot(xg, g1_ref[...])).astype(bf16), g2_ref[...])

    rr = lax.broadcasted_iota(jnp.int32, (LANES, LANES), 0) // RWKV_HEAD_DIM
    cc = lax.broadcasted_iota(jnp.int32, (LANES, LANES), 1) // RWKV_HEAD_DIM
    seg_ones = jnp.where(rr == cc, 1.0, 0.0).astype(bf16)

    def head_sum(x):
        hi = x.astype(bf16)
        lo = (x - hi.astype(f32)).astype(bf16)
        cols = [dot(hi[:, j:j + LANES], seg_ones) + dot(lo[:, j:j + LANES], seg_ones)
                for j in range(0, w, LANES)]
        return jnp.concatenate(cols, axis=1)

    kk = k * kk_ref[...]
    kk = kk / jnp.maximum(jnp.sqrt(head_sum(kk * kk)), 1e-12)
    k2 = k * (1.0 + (a - 1.0) * ka_ref[...])
    r_ref[...] = r
    k_ref[...] = k2
    v_ref[...] = v
    an_ref[...] = -kk
    b_ref[...] = kk * a
    bonus_ref[...] = head_sum(r * k2 * rk_ref[...]) * v


def rwkv_pre(u, u_prev, proj, p_prev, lp, *, rolled):
    n, d = u.shape
    tm = RWKV_PRE_TILE_M
    w3 = 3 * RWKV_WIDTH
    if rolled:
        prev_specs = [pl.BlockSpec((1, 1, d), lambda i: (i, 0, 0)), pl.BlockSpec((1, 1, w3), lambda i: (i, 0, 0))]
    else:
        prev_specs = [pl.BlockSpec((tm, d), lambda i: (i, 0)), pl.BlockSpec((tm, w3), lambda i: (i, 0))]
    bf16 = jnp.bfloat16
    row = lambda x: x.reshape(1, -1)
    consts = [row(lp["mu_rkv"]), lp["mu_wag"], row(lp["w0"]), lp["w1"].astype(bf16), lp["w2"].astype(bf16),
              row(lp["a0"]), lp["a1"].astype(bf16), lp["a2"].astype(bf16), lp["g1"].astype(bf16),
              lp["g2"].astype(bf16), row(lp["k_k"]), row(lp["k_a"]), row(lp["r_k"])]
    full = lambda a: pl.BlockSpec(a.shape, lambda i: (0, 0))
    out_spec = pl.BlockSpec((tm, RWKV_WIDTH), lambda i: (i, 0))
    return pl.pallas_call(
        functools.partial(_rwkv_pre_kernel, rolled=rolled),
        out_shape=tuple(jax.ShapeDtypeStruct((n, RWKV_WIDTH), jnp.float32) for _ in range(8)),
        grid=(n // tm,),
        in_specs=[pl.BlockSpec((tm, d), lambda i: (i, 0)), prev_specs[0],
                  pl.BlockSpec((tm, w3), lambda i: (i, 1)), prev_specs[1]]
                 + [full(c) for c in consts],
        out_specs=tuple(out_spec for _ in range(8)),
        compiler_params=pltpu.CompilerParams(
            dimension_semantics=("arbitrary",), vmem_limit_bytes=VMEM_LIMIT_BYTES),
        name="rwkv_pre",
    )(u, u_prev, proj, p_prev, *consts)


def _bmm(a, b):
    return lax.dot_general(a, b, (((2,), (1,)), ((0,), (0,))), preferred_element_type=jnp.float32)


def _bmm_nt(a, b):
    return lax.dot_general(a, b, (((2,), (2,)), ((0,), (0,))), preferred_element_type=jnp.float32)


def _bmm_tn(a, b):
    return lax.dot_general(a, b, (((1,), (1,)), ((0,), (0,))), preferred_element_type=jnp.float32)


def _rwkv_chunk_kernel(r_ref, lw_ref, k_ref, v_ref, a_ref, b_ref, g_ref, bonus_ref, gng_ref, gnb_ref, s0_ref,
                       o_ref, sT_ref, s_scr, *, n_seq, seq_rows, chunk, group):
    bf16, f32 = jnp.bfloat16, jnp.float32
    c = pl.program_id(1)
    C = chunk
    H, N = RWKV_HEADS, RWKV_HEAD_DIM

    @pl.when(c == 0)
    def _():
        s_scr[...] = s0_ref[...]

    row = lax.broadcasted_iota(jnp.int32, (C, C), 0)
    col = lax.broadcasted_iota(jnp.int32, (C, C), 1)
    strict = (row > col)[None]
    incl = (row >= col)[None]
    G = group
    tri = jnp.broadcast_to(jnp.where(incl, 1.0, 0.0).astype(bf16), (G * H, C, C))

    rows_of = lambda ref, i: ref[i] if len(ref.shape) == 3 else ref[i, 0]

    def by_head(ref, i):
        x = rows_of(ref, i)
        if seq_rows < C:
            x = jnp.concatenate([x, jnp.zeros((C - seq_rows, x.shape[1]), f32)], axis=0)
        return jnp.stack([x[:, h * N:(h + 1) * N] for h in range(H)], axis=0)

    def one_seq(j, carry):
        seqs = [j * G + q for q in range(G)]
        stack = lambda ref: jnp.concatenate([by_head(ref, i) for i in seqs], axis=0)
        r, lw, k, v, a, b = (stack(ref) for ref in (r_ref, lw_ref, k_ref, v_ref, a_ref, b_ref))
        s0 = jnp.concatenate([s_scr[i] for i in seqs], axis=0)
        hi = lw.astype(bf16)
        r1 = lw - hi.astype(f32)
        mid = r1.astype(bf16)
        lo = (r1 - mid.astype(f32)).astype(bf16)
        lc = _bmm(tri, hi) + _bmm(tri, mid) + _bmm(tri, lo)
        e_pos = jnp.exp(lc)
        e_neg = jnp.exp(-lc)
        a_t = (a * jnp.exp(lc - lw)).astype(bf16)
        r_t = (r * e_pos).astype(bf16)
        b_t = (b * e_neg).astype(bf16)
        k_t = (k * e_neg).astype(bf16)
        v_b = v.astype(bf16)
        s0_b = s0.astype(bf16)

        a_ab = jnp.where(strict, _bmm_nt(a_t, b_t), 0.0)
        a_ak = jnp.where(strict, _bmm_nt(a_t, k_t), 0.0)
        a_rb = jnp.where(incl, _bmm_nt(r_t, b_t), 0.0)
        a_rk = jnp.where(incl, _bmm_nt(r_t, k_t), 0.0)

        x = _bmm_nt(a_t, s0_b) + _bmm(a_ak.astype(bf16), v_b)
        p = a_ab
        n = 1
        while n < C:
            p_b = p.astype(bf16)
            x = x + _bmm(p_b, x.astype(bf16))
            n *= 2
            if n < C:
                p = _bmm(p_b, p_b)
        u_b = x.astype(bf16)
        o = _bmm_nt(r_t, s0_b) + _bmm(a_rb.astype(bf16), u_b) + _bmm(a_rk.astype(bf16), v_b)
        s1 = s0 + _bmm_tn(u_b, b_t) + _bmm_tn(v_b, k_t)
        s1 = s1 * e_pos[:, C - 1:C, :]
        mean = jnp.mean(o, -1, keepdims=True)
        oc = o - mean
        var = jnp.mean(oc * oc, -1, keepdims=True)
        on = oc * lax.rsqrt(var + GN_EPS)
        for q, i in enumerate(seqs):
            s_scr[i] = s1[q * H:(q + 1) * H]
            on_i = jnp.concatenate([on[q * H + h] for h in range(H)], axis=1)[:seq_rows]
            res = (on_i * gng_ref[...] + gnb_ref[...] + rows_of(bonus_ref, i)) * rows_of(g_ref, i)
            if len(o_ref.shape) == 3:
                o_ref[i] = res
            else:
                o_ref[i, 0] = res
        return carry

    lax.fori_loop(0, n_seq // G, one_seq, 0)

    @pl.when(c == pl.num_programs(1) - 1)
    def _():
        sT_ref[...] = s_scr[...]


def rwkv_recurrence(s0, seqs, gn_g, gn_b, *, n_batch, n_chunks, seq_rows, n_seq, chunk, group=1):
    h, n = RWKV_HEADS, RWKV_HEAD_DIM
    w = RWKV_WIDTH
    if seqs[0].ndim == 4:
        seq_spec = pl.BlockSpec((n_seq, 1, seq_rows, w), lambda i, c: (i, c, 0, 0))
    else:
        seq_spec = pl.BlockSpec((n_seq, seq_rows, w), lambda i, c: (i, 0, 0))
    st_spec = pl.BlockSpec((n_seq, h, n, n), lambda i, c: (i, 0, 0, 0))
    vec_spec = pl.BlockSpec((1, w), lambda i, c: (0, 0))
    o, s_t = pl.pallas_call(
        functools.partial(_rwkv_chunk_kernel, n_seq=n_seq, seq_rows=seq_rows, chunk=chunk, group=group),
        out_shape=(jax.ShapeDtypeStruct(seqs[0].shape, jnp.float32),
                   jax.ShapeDtypeStruct((n_batch, h, n, n), jnp.float32)),
        grid=(n_batch // n_seq, n_chunks),
        in_specs=[seq_spec] * 8 + [vec_spec, vec_spec, st_spec],
        out_specs=(seq_spec, st_spec),
        scratch_shapes=[pltpu.VMEM((n_seq, h, n, n), jnp.float32)],
        compiler_params=pltpu.CompilerParams(
            dimension_semantics=("arbitrary", "arbitrary"), vmem_limit_bytes=VMEM_LIMIT_BYTES),
        name="rwkv_chunk",
    )(*seqs, gn_g.reshape(1, w), gn_b.reshape(1, w), s0)
    return o, s_t


def _moe_scatter_kernel(slot_ref, h_ref, init_hbm, out_hbm, stage, sems):
    del init_hbm
    i = pl.program_id(0)
    nt = pl.num_programs(0)
    tt = h_ref.shape[0]
    slot = i % 2

    def wait_buffer(s):
        for _ in range(TOP_K):
            pltpu.make_async_copy(stage.at[s], out_hbm.at[pl.ds(0, tt)], sems.at[s]).wait()

    @pl.when(i >= 2)
    def _():
        wait_buffer(slot)

    stage[slot] = h_ref[...]
    for j in range(tt):
        for k in range(TOP_K):
            dst = slot_ref[(i * tt + j) * TOP_K + k]
            pltpu.make_async_copy(stage.at[slot, pl.ds(j, 1)], out_hbm.at[pl.ds(dst, 1)], sems.at[slot]).start()

    @pl.when(i == nt - 1)
    def _():
        wait_buffer(slot)

        @pl.when(nt >= 2)
        def _():
            wait_buffer(1 - slot)


def _moe_scatter_rows(h, pair_slot, p_pad):
    n, d = h.shape
    tt = MOE_ROUTE_TILE
    return pl.pallas_call(
        _moe_scatter_kernel,
        out_shape=jax.ShapeDtypeStruct((p_pad, d), jnp.float32),
        grid_spec=pltpu.PrefetchScalarGridSpec(
            num_scalar_prefetch=1,
            grid=(n // tt,),
            in_specs=[pl.BlockSpec((tt, d), lambda i, ps: (i, 0)),
                      pl.BlockSpec(memory_space=pl.ANY)],
            out_specs=pl.BlockSpec(memory_space=pl.ANY),
            scratch_shapes=[pltpu.VMEM((2, tt, d), jnp.float32), pltpu.SemaphoreType.DMA((2,))],
        ),
        input_output_aliases={2: 0},
        compiler_params=pltpu.CompilerParams(
            dimension_semantics=("arbitrary",), vmem_limit_bytes=VMEM_LIMIT_BYTES),
        name="moe_scatter",
    )(pair_slot.reshape(-1), h, jnp.zeros((p_pad, d), jnp.float32))


def _weight_copies(w_hbm, stage, sem, expert, col0, width, slot, n_parts, part_stride):
    return [pltpu.make_async_copy(w_hbm.at[expert, :, pl.ds(pl.multiple_of(col0 + p * part_stride, LANES), width)],
                                  stage.at[slot, p], sem.at[slot]) for p in range(n_parts)]


def _stream_expert_weights(sched_ref, w_hbm, stage, sem, w_bf, *, width, n_parts, part_stride):
    s = pl.program_id(0) * pl.num_programs(1) + pl.program_id(1)
    field = lambda j: sched_ref[s * SCHED_COLS + j]

    @pl.when(field(0) != 0)
    def _():
        slot = field(1)
        cur = _weight_copies(w_hbm, stage, sem, field(2), field(3), width, slot, n_parts, part_stride)

        @pl.when(s == 0)
        def _():
            for cp in cur:
                cp.start()

        for cp in cur:
            cp.wait()
        for p in range(n_parts):
            w_bf[p] = stage[slot, p].astype(jnp.bfloat16)

        @pl.when(field(4) != 0)
        def _():
            for cp in _weight_copies(w_hbm, stage, sem, field(5), field(6), width, 1 - slot, n_parts, part_stride):
                cp.start()


def _weight_schedule(tile_expert, n_chunks, chunk_width):
    nt = tile_expert.shape[0]
    n = n_chunks * nt
    expert = jnp.tile(tile_expert, n_chunks)
    col0 = jnp.repeat(jnp.arange(n_chunks, dtype=jnp.int32) * chunk_width, nt)
    step = jnp.arange(n, dtype=jnp.int32)
    start = (step % nt == 0) | (expert != jnp.roll(expert, 1))
    slot = (jnp.cumsum(start.astype(jnp.int32)) - 1) % 2
    nxt = lax.cummin(jnp.where(start, step, n), axis=0, reverse=True)
    nxt = jnp.concatenate([nxt[1:], jnp.full((1,), n, jnp.int32)])
    has_next = nxt < n
    nxt_c = jnp.minimum(nxt, n - 1)
    return jnp.stack([start.astype(jnp.int32), slot, expert, col0, has_next.astype(jnp.int32),
                      expert[nxt_c], col0[nxt_c]], axis=1).astype(jnp.int32).reshape(-1)


def _moe_up_kernel(sched_ref, tile_valid_ref, x_ref, bg_ref, bu_ref, w_hbm, h_ref, stage, sem, w_bf):
    f = w_hbm.shape[2] // 2
    _stream_expert_weights(sched_ref, w_hbm, stage, sem, w_bf, width=h_ref.shape[1], n_parts=2, part_stride=f)
    i = pl.program_id(1)

    @pl.when(tile_valid_ref[i] != 0)
    def _():
        x = x_ref[...].astype(jnp.bfloat16)
        hg = jnp.dot(x, w_bf[0], preferred_element_type=jnp.float32) + bg_ref[0]
        hu = jnp.dot(x, w_bf[1], preferred_element_type=jnp.float32) + bu_ref[0]
        glu = jnp.minimum(hg, SWIGLU_LIMIT)
        up = jnp.clip(hu, -SWIGLU_LIMIT, SWIGLU_LIMIT)
        act = (up + 1.0) * glu * jax.nn.sigmoid(SWIGLU_ALPHA * glu)
        h_ref[...] = act.astype(h_ref.dtype)

    @pl.when(tile_valid_ref[i] == 0)
    def _():
        h_ref[...] = jnp.zeros_like(h_ref)


def _moe_down_kernel(sched_ref, tile_valid_ref, h_ref, bd_ref, w_hbm, y_ref, stage, sem, w_bf):
    _stream_expert_weights(sched_ref, w_hbm, stage, sem, w_bf, width=y_ref.shape[1], n_parts=1, part_stride=0)
    i = pl.program_id(1)

    @pl.when(tile_valid_ref[i] != 0)
    def _():
        y_ref[...] = jnp.dot(h_ref[...], w_bf[0], preferred_element_type=jnp.float32) + bd_ref[0]

    @pl.when(tile_valid_ref[i] == 0)
    def _():
        y_ref[...] = jnp.zeros_like(y_ref)


def _moe_grouped(x_sorted, tile_expert, tile_valid, w_gate_up, b_gate_up, w_down, b_down):
    p_pad, d = x_sorted.shape
    n_exp, _, two_f = w_gate_up.shape
    f = two_f // 2
    tm, fc, dc = MOE_TILE_M, MOE_F_CHUNK, MOE_D_CHUNK
    nt = p_pad // tm
    nfc = f // fc
    ndc = d // dc
    bgu = b_gate_up.reshape(n_exp, 1, two_f)
    bdn = b_down.reshape(n_exp, 1, d)
    params = pltpu.CompilerParams(dimension_semantics=("arbitrary", "arbitrary"),
                                  vmem_limit_bytes=MOE_VMEM_LIMIT_BYTES)
    te_of = lambda sched, c, i: sched[(c * nt + i) * SCHED_COLS + 2]

    h = pl.pallas_call(
        _moe_up_kernel,
        out_shape=jax.ShapeDtypeStruct((p_pad, f), jnp.bfloat16),
        grid_spec=pltpu.PrefetchScalarGridSpec(
            num_scalar_prefetch=2,
            grid=(nfc, nt),
            in_specs=[
                pl.BlockSpec((tm, d), lambda c, i, sc, tv: (i, 0)),
                pl.BlockSpec((1, 1, fc), lambda c, i, sc, tv: (te_of(sc, c, i), 0, c)),
                pl.BlockSpec((1, 1, fc), lambda c, i, sc, tv: (te_of(sc, c, i), 0, nfc + c)),
                pl.BlockSpec(memory_space=pl.ANY),
            ],
            out_specs=pl.BlockSpec((tm, fc), lambda c, i, sc, tv: (i, c)),
            scratch_shapes=[pltpu.VMEM((2, 2, d, fc), jnp.float32), pltpu.SemaphoreType.DMA((2,)),
                            pltpu.VMEM((2, d, fc), jnp.bfloat16)],
        ),
        compiler_params=params,
        name="moe_up",
    )(_weight_schedule(tile_expert, nfc, fc), tile_valid, x_sorted, bgu, bgu, w_gate_up)

    return pl.pallas_call(
        _moe_down_kernel,
        out_shape=jax.ShapeDtypeStruct((p_pad, d), jnp.float32),
        grid_spec=pltpu.PrefetchScalarGridSpec(
            num_scalar_prefetch=2,
            grid=(ndc, nt),
            in_specs=[
                pl.BlockSpec((tm, f), lambda c, i, sc, tv: (i, 0)),
                pl.BlockSpec((1, 1, dc), lambda c, i, sc, tv: (te_of(sc, c, i), 0, c)),
                pl.BlockSpec(memory_space=pl.ANY),
            ],
            out_specs=pl.BlockSpec((tm, dc), lambda c, i, sc, tv: (i, c)),
            scratch_shapes=[pltpu.VMEM((2, 1, f, dc), jnp.float32), pltpu.SemaphoreType.DMA((2,)),
                            pltpu.VMEM((1, f, dc), jnp.bfloat16)],
        ),
        compiler_params=params,
        name="moe_down",
    )(_weight_schedule(tile_expert, ndc, dc), tile_valid, h, bdn, w_down)


def _moe_combine_kernel(slot_ref, h_ref, w_ref, g_ref, b_ref, y_hbm, out_ref, buf, sems):
    i = pl.program_id(0)
    nt = pl.num_programs(0)
    tt = h_ref.shape[0]

    def start_tile(t, s):
        for j in range(tt):
            for k in range(TOP_K):
                src = slot_ref[(t * tt + j) * TOP_K + k]
                pltpu.make_async_copy(y_hbm.at[pl.ds(src, 1)], buf.at[s, pl.ds(k * tt + j, 1)], sems.at[s]).start()

    @pl.when(i == 0)
    def _():
        start_tile(0, 0)

    @pl.when(i + 1 < nt)
    def _():
        start_tile(i + 1, (i + 1) % 2)

    slot = i % 2
    pltpu.make_async_copy(y_hbm.at[pl.ds(0, TOP_K * tt)], buf.at[slot], sems.at[slot]).wait()
    w = w_ref[...]
    ffn = buf[slot, 0:tt] * w[:, 0:1]
    for k in range(1, TOP_K):
        ffn = ffn + buf[slot, k * tt:(k + 1) * tt] * w[:, k:k + 1]
    x = DEEPNORM_ALPHA * h_ref[...] + ffn
    mu = jnp.mean(x, -1, keepdims=True)
    xc = x - mu
    var = jnp.mean(xc * xc, -1, keepdims=True)
    out_ref[...] = xc * lax.rsqrt(var + LN_EPS) * g_ref[...] + b_ref[...]


def _moe_combine_ln(h, y, pair_slot, top_w, ln_g, ln_b):
    n, d = h.shape
    tt = MOE_ROUTE_TILE
    return pl.pallas_call(
        _moe_combine_kernel,
        out_shape=jax.ShapeDtypeStruct((n, d), jnp.float32),
        grid_spec=pltpu.PrefetchScalarGridSpec(
            num_scalar_prefetch=1,
            grid=(n // tt,),
            in_specs=[pl.BlockSpec((tt, d), lambda i, ps: (i, 0)),
                      pl.BlockSpec((tt, TOP_K), lambda i, ps: (i, 0)),
                      pl.BlockSpec((1, d), lambda i, ps: (0, 0)),
                      pl.BlockSpec((1, d), lambda i, ps: (0, 0)),
                      pl.BlockSpec(memory_space=pl.ANY)],
            out_specs=pl.BlockSpec((tt, d), lambda i, ps: (i, 0)),
            scratch_shapes=[pltpu.VMEM((2, TOP_K * tt, d), jnp.float32), pltpu.SemaphoreType.DMA((2,))],
        ),
        compiler_params=pltpu.CompilerParams(
            dimension_semantics=("arbitrary",), vmem_limit_bytes=VMEM_LIMIT_BYTES),
        name="moe_combine",
    )(pair_slot.reshape(-1), h, top_w, ln_g.reshape(1, d), ln_b.reshape(1, d), y)


def moe_ffn_ln(x, top_w, top_idx, w_gate_up, b_gate_up, w_down, b_down, ln_g, ln_b):
    n, d = x.shape
    tm = MOE_TILE_M
    nt = n * TOP_K // tm + N_EXPERTS
    onehot = (top_idx[:, :, None] == jnp.arange(N_EXPERTS, dtype=top_idx.dtype)).astype(jnp.int32)
    routed = jnp.sum(onehot, axis=1)
    before = jnp.cumsum(routed, axis=0) - routed
    counts = before[-1] + routed[-1]
    tiles_per = (counts + tm - 1) // tm
    tile_end = jnp.cumsum(tiles_per)
    first_row = (tile_end - tiles_per) * tm
    pair_slot = jnp.sum(onehot * (before + first_row)[:, None, :], axis=2)
    tile_ids = jnp.arange(nt, dtype=jnp.int32)
    tile_expert = jnp.minimum(jnp.sum((tile_ids[:, None] >= tile_end[None, :]).astype(jnp.int32), axis=1),
                              N_EXPERTS - 1).astype(jnp.int32)
    tile_valid = (tile_ids < tile_end[-1]).astype(jnp.int32)

    x_sorted = _moe_scatter_rows(x, pair_slot, nt * tm)
    y = _moe_grouped(x_sorted, tile_expert, tile_valid, w_gate_up, b_gate_up, w_down, b_down)
    return _moe_combine_ln(x, y, pair_slot, top_w, ln_g, ln_b)


def kernel(x_prompt, x_sample, cache_win0_kv, cache_win1_kv, cache_win2_kv, cache_mem_kv,
           state_wkv, state_shift, mem_prompt,
           w_in, w_gate, b_gate, w_branch, w_out, w_mem_kv,
           mu_rkv, mu_wag, w0, w1, w2, a0, a1, a2, g1, g2, k_k, k_a, r_k, gn_g, gn_b,
           ln1_g, ln1_b, w_router, b_router, w_gate_up, b_gate_up, w_down, b_down, ln2_g, ln2_b):
    bf16 = jnp.bfloat16
    xp, xs = x_prompt, x_sample
    bp, tp, d = xp.shape
    bs, ts, _ = xs.shape
    n_prompt = bp * tp
    l = 0
    lp = {"mu_rkv": mu_rkv[l], "mu_wag": mu_wag[l],
          "w0": w0[l], "w1": w1[l], "w2": w2[l], "a0": a0[l], "a1": a1[l], "a2": a2[l],
          "g1": g1[l], "g2": g2[l], "k_k": k_k[l], "k_a": k_a[l], "r_k": r_k[l]}

    xp2, xs2 = xp.reshape(-1, d), xs.reshape(-1, d)
    u_all = jnp.concatenate([xp2, xs2], axis=0)
    w_in_bf = w_in[l].astype(bf16)
    proj_p2 = matmul_bias(xp2.astype(bf16), w_in_bf, tile_m=PROJ_PROMPT_TILE_M, name="proj_in")
    proj_s2 = matmul_bias(xs2.astype(bf16), w_in_bf, tile_m=xs2.shape[0], name="proj_in_step")
    gate = matmul_bias(u_all.astype(bf16), w_gate[l].astype(bf16), b_gate[l], tile_m=PROJ_TILE_M, sigmoid=True,
                       name="proj_gate")
    mem_rows = mem_prompt.reshape(-1, d)
    mem_kv_p = matmul_bias(mem_rows.astype(bf16), w_mem_kv[l].astype(bf16), tile_m=mem_rows.shape[0], name="proj_mem")
    proj_p = proj_p2.reshape(bp, tp, PROJ_WIDTH)
    proj_s = proj_s2.reshape(bs, ts, PROJ_WIDTH)
    rkv0, qm0 = 3 * ATT_WIDTH, 3 * ATT_WIDTH + 3 * RWKV_WIDTH

    heads = lambda t: t.reshape(t.shape[0], t.shape[1], N_ATT_HEADS, HEAD_DIM)
    ka_p, va_p = (heads(proj_p[..., i * ATT_WIDTH:(i + 1) * ATT_WIDTH]) for i in (1, 2))
    ka_s, va_s = (heads(proj_s[..., i * ATT_WIDTH:(i + 1) * ATT_WIDTH]) for i in (1, 2))
    outs, lses, win_p, win_s = [], [], [], []
    for g, (win, dil) in enumerate(DIL_GROUPS):
        hs = slice(g * HEADS_PER_GROUP, (g + 1) * HEADS_PER_GROUP)
        o, lse = dilated_attention_prompt(proj_p2, bp, tp, g, dil)
        outs.append(o)
        lses.append(lse)
        keep = min(win, tp)
        win_p.append(jnp.stack([ka_p[:, tp - keep:, hs], va_p[:, tp - keep:, hs]], axis=2))
        win_s.append(jnp.stack([ka_s[:, :, hs], va_s[:, :, hs]], axis=2))
    o_att_p = att_combine(outs, lses)
    o_att_s = dilated_attention_step(proj_s, cache_win0_kv[l], cache_win1_kv[l], cache_win2_kv[l])

    tq = MEM_ATTN_TILE_Q
    o_mem_p = memory_attention(proj_p2.reshape(-1, tq, PROJ_WIDTH), qm0 // MEM_WIDTH,
                               mem_kv_p.reshape(bp, MEM_TOKENS, 2 * MEM_WIDTH), tp // tq)
    o_mem_s = memory_attention(proj_s, qm0 // MEM_WIDTH, cache_mem_kv[l], 1, n_seq=MEM_ATTN_STEP_SEQS)

    tm = RWKV_PRE_TILE_M

    def tile_prev_rows(x):
        last = x[:, tm - 1::tm][:, :-1]
        return jnp.concatenate([jnp.zeros_like(last[:, :1]), last], axis=1).reshape(-1, 1, x.shape[-1])

    rwkv_in_p = rwkv_pre(xp2, tile_prev_rows(xp), proj_p2, tile_prev_rows(proj_p[..., rkv0:qm0]), lp, rolled=True)
    u_last_s = state_shift[l].astype(xs.dtype)
    p_last_s = matmul_bias(u_last_s.astype(bf16), w_in_bf[:, rkv0:qm0], tile_m=bs, name="proj_shift")
    shift = lambda first, rest: jnp.concatenate([first[:, None], rest[:, :-1]], axis=1).reshape(-1, rest.shape[-1])
    rwkv_in_s = rwkv_pre(xs2, shift(u_last_s, xs), proj_s2, shift(p_last_s, proj_s[..., rkv0:qm0]), lp, rolled=False)
    s0 = jnp.zeros((bp, RWKV_HEADS, RWKV_HEAD_DIM, RWKV_HEAD_DIM), jnp.float32)
    o_rwkv_p, wkv_p = rwkv_recurrence(
        s0, [x.reshape(bp, tp // RWKV_CHUNK, RWKV_CHUNK, RWKV_WIDTH) for x in rwkv_in_p], gn_g[l], gn_b[l],
        n_batch=bp, n_chunks=tp // RWKV_CHUNK, seq_rows=RWKV_CHUNK, n_seq=RWKV_PROMPT_SEQS, chunk=RWKV_CHUNK,
        group=RWKV_PROMPT_SEQS)
    o_rwkv_s, wkv_s = rwkv_recurrence(
        state_wkv[l].astype(jnp.float32), [x.reshape(bs, ts, RWKV_WIDTH) for x in rwkv_in_s], gn_g[l], gn_b[l],
        n_batch=bs, n_chunks=1, seq_rows=ts, n_seq=RWKV_SEQ_PER_STEP, chunk=-(-ts // SUBLANES) * SUBLANES,
        group=RWKV_SEQ_GROUP)

    cat = lambda a, b, w: jnp.concatenate([a.reshape(-1, w), b.reshape(-1, w)], axis=0)
    h, top_w, top_idx = merge_ln_router(
        cat(o_att_p, o_att_s, ATT_OUT), cat(o_rwkv_p, o_rwkv_s, RWKV_WIDTH), cat(o_mem_p, o_mem_s, MEM_WIDTH),
        gate, u_all, w_branch[l].astype(bf16), w_out[l].astype(bf16), ln1_g[l], ln1_b[l], w_router[l], b_router[l])
    y_all = moe_ffn_ln(h, top_w, top_idx, w_gate_up[l], b_gate_up[l], w_down[l], b_down[l], ln2_g[l], ln2_b[l])
    yp = y_all[:n_prompt].reshape(bp, tp, d)
    ys = y_all[n_prompt:].reshape(bs, ts, d)
    mem_kv_out = mem_kv_p.reshape(bp, MEM_TOKENS, 2, MEM_HEADS, MEM_HEAD_DIM)
    return (yp, ys,
            win_p[0][None], win_p[1][None], win_p[2][None], wkv_p.astype(xp.dtype)[None],
            xp[:, -1][None], mem_kv_out[None],
            win_s[0].astype(cache_win0_kv.dtype)[None], win_s[1].astype(cache_win0_kv.dtype)[None],
            win_s[2].astype(cache_win0_kv.dtype)[None], wkv_s.astype(state_wkv.dtype)[None],
            xs[:, -1].astype(state_shift.dtype)[None])
```

```python
import functools

import jax
import jax.numpy as jnp
from jax import lax
from jax.experimental import pallas as pl
from jax.experimental.pallas import tpu as pltpu

D_MODEL = 2048
DEPTH = 1
HEAD_DIM = 64
DIL_GROUPS = ((128, 1), (512, 4), (2048, 16))
HEADS_PER_GROUP = 4
N_ATT_HEADS = HEADS_PER_GROUP * len(DIL_GROUPS)
ATT_WIDTH = N_ATT_HEADS * HEAD_DIM
ATT_OUT = HEADS_PER_GROUP * HEAD_DIM
BAND_BLOCK = 128
N_BACK = 128
GROUP_WIDTH = HEADS_PER_GROUP * HEAD_DIM
RWKV_HEADS = 12
RWKV_HEAD_DIM = 64
RWKV_WIDTH = RWKV_HEADS * RWKV_HEAD_DIM
GN_EPS = 64e-5
MEM_TOKENS = 256
MEM_HEADS = 4
MEM_HEAD_DIM = 128
MEM_WIDTH = MEM_HEADS * MEM_HEAD_DIM
PROJ_WIDTH = 3 * ATT_WIDTH + 3 * RWKV_WIDTH + MEM_WIDTH
N_BRANCHES = 3
N_EXPERTS = 32
TOP_K = 4
D_EXPERT = 2048
SWIGLU_LIMIT = 7.0
SWIGLU_ALPHA = 1.702
LN_EPS = 1e-5
DEEPNORM_ALPHA = (2.0 * DEPTH) ** 0.25
NEG_INF = -1e30

VMEM_LIMIT_BYTES = 56 * 1024 * 1024
SUBLANES = 8
LANES = 128
MOE_TILE_M = 256
MOE_F_CHUNK = 1024
MOE_D_CHUNK = 1024
MOE_ROUTE_TILE = 64
MOE_VMEM_LIMIT_BYTES = 58 * 1024 * 1024
SCHED_COLS = 7
DMA_PRIORITIES = 2
RWKV_CHUNK = 64
RWKV_SEQ_PER_STEP = 8
RWKV_SEQ_GROUP = 4
RWKV_PROMPT_SEQS = 2
RWKV_PRE_TILE_M = 256
PROJ_TILE_M = 1088
PROJ_PROMPT_TILE_M = 1024
MEM_ATTN_STEP_SEQS = 8
MATMUL_TILE_N = 512
MERGE_TILE_M = 256
ATT_COMBINE_TILE = 512
MEM_ATTN_TILE_Q = 512
ATTN_STEP_SEQS = 2


def _matmul_kernel(x_ref, w_ref, b_ref, o_ref, *, sigmoid):
    y = jnp.dot(x_ref[...], w_ref[...], preferred_element_type=jnp.float32) + b_ref[...]
    o_ref[...] = jax.nn.sigmoid(y) if sigmoid else y


def matmul_bias(x, w, bias=None, *, tile_m, sigmoid=False, name="matmul"):
    m, k = x.shape
    n = w.shape[1]
    tn = MATMUL_TILE_N if n % MATMUL_TILE_N == 0 else n
    if bias is None:
        bias = jnp.zeros((n,), jnp.float32)
    return pl.pallas_call(
        functools.partial(_matmul_kernel, sigmoid=sigmoid),
        out_shape=jax.ShapeDtypeStruct((m, n), jnp.float32),
        grid=(n // tn, m // tile_m),
        in_specs=[pl.BlockSpec((tile_m, k), lambda j, i: (i, 0)),
                  pl.BlockSpec((k, tn), lambda j, i: (0, j)),
                  pl.BlockSpec((1, tn), lambda j, i: (0, j))],
        out_specs=pl.BlockSpec((tile_m, tn), lambda j, i: (i, j)),
        compiler_params=pltpu.CompilerParams(
            dimension_semantics=("arbitrary", "arbitrary"), vmem_limit_bytes=VMEM_LIMIT_BYTES),
        name=name,
    )(x, w, bias.reshape(1, n))


def _merge_kernel(att_ref, rwkv_ref, mem_ref, gate_ref, x_ref, wb_ref, wo_ref, g_ref, b_ref,
                  wr_hi_ref, wr_lo_ref, br_ref, h_ref, topw_ref, topi_ref):
    bf16, f32 = jnp.bfloat16, jnp.float32
    d = x_ref.shape[1]
    dot = functools.partial(jnp.dot, preferred_element_type=f32)
    y = dot(att_ref[...].astype(bf16), wb_ref[0:ATT_OUT]) * gate_ref[:, 0:d]
    y = y + dot(rwkv_ref[...].astype(bf16), wb_ref[ATT_OUT:ATT_OUT + RWKV_WIDTH]) * gate_ref[:, d:2 * d]
    y = y + dot(mem_ref[...].astype(bf16), wb_ref[ATT_OUT + RWKV_WIDTH:]) * gate_ref[:, 2 * d:]
    z = DEEPNORM_ALPHA * x_ref[...] + dot(y.astype(bf16), wo_ref[...])
    mu = jnp.mean(z, -1, keepdims=True)
    zc = z - mu
    var = jnp.mean(zc * zc, -1, keepdims=True)
    h = zc * lax.rsqrt(var + LN_EPS) * g_ref[...] + b_ref[...]
    h_ref[...] = h

    h_hi = h.astype(bf16)
    h_lo = (h - h_hi.astype(f32)).astype(bf16)
    logits = (dot(h_hi, wr_hi_ref[...]) + dot(h_hi, wr_lo_ref[...]) + dot(h_lo, wr_hi_ref[...])) + br_ref[...]
    lane = lax.broadcasted_iota(jnp.int32, logits.shape, 1)
    work = jnp.where(lane < N_EXPERTS, logits, -jnp.inf)
    tops, ids = [], []
    for _ in range(TOP_K):
        m = jnp.max(work, -1, keepdims=True)
        idx = jnp.min(jnp.where(work == m, lane, LANES), -1, keepdims=True)
        tops.append(m)
        ids.append(idx)
        work = jnp.where(lane == idx, -jnp.inf, work)
    e = [jnp.exp(t - tops[0]) for t in tops]
    den = e[0] + e[1] + e[2] + e[3]
    tw = jnp.zeros(logits.shape, f32)
    ti = jnp.zeros(logits.shape, jnp.int32)
    for k in range(TOP_K):
        tw = jnp.where(lane == k, e[k] / den, tw)
        ti = jnp.where(lane == k, ids[k], ti)
    topw_ref[...] = tw
    topi_ref[...] = ti


def merge_ln_router(o_att, o_rwkv, o_mem, gate, x, w_branch, w_out, ln_g, ln_b, w_router, b_router):
    n, d = x.shape
    tm = MERGE_TILE_M
    wr = jnp.zeros((d, LANES), jnp.float32).at[:, :N_EXPERTS].set(w_router)
    wr_hi = wr.astype(jnp.bfloat16)
    wr_lo = (wr - wr_hi.astype(jnp.float32)).astype(jnp.bfloat16)
    br = jnp.zeros((1, LANES), jnp.float32).at[0, :N_EXPERTS].set(b_router)
    rows = lambda w: pl.BlockSpec((tm, w), lambda i: (i, 0))
    full = lambda a: pl.BlockSpec(a.shape, lambda i: (0, 0))
    h, topw, topi = pl.pallas_call(
        _merge_kernel,
        out_shape=(jax.ShapeDtypeStruct((n, d), jnp.float32),
                   jax.ShapeDtypeStruct((n, LANES), jnp.float32),
                   jax.ShapeDtypeStruct((n, LANES), jnp.int32)),
        grid=(n // tm,),
        in_specs=[rows(ATT_OUT), rows(RWKV_WIDTH), rows(MEM_WIDTH), rows(3 * d), rows(d),
                  full(w_branch), full(w_out), pl.BlockSpec((1, d), lambda i: (0, 0)),
                  pl.BlockSpec((1, d), lambda i: (0, 0)), full(wr_hi), full(wr_lo), full(br)],
        out_specs=(rows(d), rows(LANES), rows(LANES)),
        compiler_params=pltpu.CompilerParams(
            dimension_semantics=("arbitrary",), vmem_limit_bytes=VMEM_LIMIT_BYTES),
        name="merge_ln_router",
    )(o_att, o_rwkv, o_mem, gate, x, w_branch, w_out, ln_g.reshape(1, d), ln_b.reshape(1, d), wr_hi, wr_lo, br)
    return h, topw[:, :TOP_K], topi[:, :TOP_K]


def _alibi_slope(head):
    return 2.0 ** (-8.0 * (head + 1) / N_ATT_HEADS)


def _dot_nt(a, b):
    return lax.dot_general(a, b, (((1,), (1,)), ((), ())), preferred_element_type=jnp.float32)


def _dil_attn_prompt_kernel(*refs, group, dil):
    bf16 = jnp.bfloat16
    blk = BAND_BLOCK
    halves = GROUP_WIDTH // LANES
    ins = [refs[5 * a:5 * a + 5] for a in range(halves)]
    o_refs = refs[5 * halves:5 * halves + halves]
    lse_refs = refs[6 * halves:]
    i = pl.program_id(1)
    qi = lax.broadcasted_iota(jnp.int32, (blk, blk), 0)
    kj = lax.broadcasted_iota(jnp.int32, (blk, blk), 1)
    dist_c = qi - kj
    dist_p = dist_c + blk
    valid_c = dist_c >= 0
    valid_p = (dist_p <= blk) & (i > 0)
    scale = HEAD_DIM ** -0.5
    per_half = LANES // HEAD_DIM
    rows = lambda ref, r: ref[pl.ds(r, blk, stride=dil), :] if dil > 1 else ref[...]
    for r in range(dil):
        x = [[rows(ref, r).astype(bf16) for ref in ins[a]] for a in range(halves)]
        part = lambda a, j, hh: x[a][j][:, hh * HEAD_DIM:(hh + 1) * HEAD_DIM]
        units = [(a, hh) for a in range(halves) for hh in range(per_half)]
        st = []
        for a, hh in units:
            slope = _alibi_slope(group * HEADS_PER_GROUP + a * per_half + hh) * dil
            s_c = _dot_nt(part(a, 0, hh), part(a, 1, hh)) * scale - slope * dist_c.astype(jnp.float32)
            s_p = _dot_nt(part(a, 0, hh), part(a, 2, hh)) * scale - slope * dist_p.astype(jnp.float32)
            st.append((jnp.where(valid_c, s_c, NEG_INF), jnp.where(valid_p, s_p, NEG_INF)))
        st = [(s_c, s_p, jnp.maximum(jnp.max(s_c, -1, keepdims=True), jnp.max(s_p, -1, keepdims=True))) for s_c, s_p in st]
        st = [(jnp.exp(s_c - m), jnp.exp(s_p - m), m) for s_c, s_p, m in st]
        st = [(p_c, p_p, m, jnp.sum(p_c, -1, keepdims=True) + jnp.sum(p_p, -1, keepdims=True)) for p_c, p_p, m in st]
        outs, lses = [], []
        for (a, hh), (p_c, p_p, m, den) in zip(units, st):
            pv = (jnp.dot(p_c.astype(bf16), part(a, 3, hh), preferred_element_type=jnp.float32)
                  + jnp.dot(p_p.astype(bf16), part(a, 4, hh), preferred_element_type=jnp.float32))
            outs.append(pv / den)
            lses.append(jnp.broadcast_to(m + jnp.log(den), (blk, HEAD_DIM)))
        for a in range(halves):
            o_a = jnp.concatenate(outs[a * per_half:(a + 1) * per_half], axis=1)
            l_a = jnp.concatenate(lses[a * per_half:(a + 1) * per_half], axis=1)
            if dil > 1:
                o_refs[a][pl.ds(r, blk, stride=dil), :] = o_a
                lse_refs[a][pl.ds(r, blk, stride=dil), :] = l_a
            else:
                o_refs[a][...] = o_a
                lse_refs[a][...] = l_a


def dilated_attention_prompt(proj, b, t, group, dil):
    rows = BAND_BLOCK * dil
    halves = GROUP_WIDTH // LANES
    assert t % rows == 0
    nblk = t // rows
    col = lambda part, a: (part * ATT_WIDTH + group * GROUP_WIDTH) // LANES + a
    cur = lambda c: pl.BlockSpec((rows, LANES), lambda n, i: (n * nblk + i, c))
    prev = lambda c: pl.BlockSpec((rows, LANES), lambda n, i: (n * nblk + jnp.maximum(i - 1, 0), c))
    in_specs = []
    for a in range(halves):
        in_specs += [cur(col(0, a)), cur(col(1, a)), prev(col(1, a)), cur(col(2, a)), prev(col(2, a))]
    out_shape = jax.ShapeDtypeStruct((b * t, LANES), jnp.float32)
    res = pl.pallas_call(
        functools.partial(_dil_attn_prompt_kernel, group=group, dil=dil),
        out_shape=(out_shape,) * (2 * halves),
        grid=(b, nblk),
        in_specs=in_specs,
        out_specs=(cur(0),) * (2 * halves),
        compiler_params=pltpu.CompilerParams(
            dimension_semantics=("arbitrary", "arbitrary"), vmem_limit_bytes=VMEM_LIMIT_BYTES),
        name="dil_attn_prompt_g%d" % group,
    )(*([proj] * (5 * halves)))
    return jnp.concatenate(res[:halves], axis=1), jnp.concatenate(res[halves:], axis=1)


def _att_combine_kernel(o0, o1, o2, l0, l1, l2, out_ref):
    a, b, c = l0[...], l1[...], l2[...]
    m = jnp.maximum(jnp.maximum(a, b), c)
    ea, eb, ec = jnp.exp(a - m), jnp.exp(b - m), jnp.exp(c - m)
    out_ref[...] = (ea * o0[...] + eb * o1[...] + ec * o2[...]) / (ea + eb + ec)


def att_combine(outs, lses):
    n, w = outs[0].shape
    spec = pl.BlockSpec((ATT_COMBINE_TILE, w), lambda i: (i, 0))
    return pl.pallas_call(
        _att_combine_kernel,
        out_shape=jax.ShapeDtypeStruct((n, w), jnp.float32),
        grid=(n // ATT_COMBINE_TILE,),
        in_specs=[spec] * 6,
        out_specs=spec,
        compiler_params=pltpu.CompilerParams(dimension_semantics=("arbitrary",)),
        name="att_combine",
    )(*outs, *lses)


def _mem_attn_kernel(q_ref, kv_ref, o_ref):
    bf16 = jnp.bfloat16
    e = MEM_HEAD_DIM
    scale = e ** -0.5
    units = [(n, h) for n in range(q_ref.shape[0]) for h in range(MEM_HEADS)]
    if len(kv_ref.shape) == 3:
        key = lambda n, h: kv_ref[n, :, h * e:(h + 1) * e]
        val = lambda n, h: kv_ref[n, :, (MEM_HEADS + h) * e:(MEM_HEADS + h + 1) * e]
    else:
        key = lambda n, h: kv_ref[n, :, 0, h, :]
        val = lambda n, h: kv_ref[n, :, 1, h, :]
    s = [_dot_nt(q_ref[n, :, h * e:(h + 1) * e].astype(bf16), key(n, h).astype(bf16)) * scale for n, h in units]
    m = [jnp.max(x, -1, keepdims=True) for x in s]
    p = [jnp.exp(x - mx) for x, mx in zip(s, m)]
    p = [x / jnp.sum(x, -1, keepdims=True) for x in p]
    for (n, h), x in zip(units, p):
        o_ref[n, :, h * e:(h + 1) * e] = jnp.dot(x.astype(bf16), val(n, h).astype(bf16),
                                                 preferred_element_type=jnp.float32)


def memory_attention(q, q_col_block, mem_kv, blocks_per_seq, n_seq=1):
    b = mem_kv.shape[0]
    tq = q.shape[1]
    w = MEM_WIDTH
    assert n_seq == 1 or blocks_per_seq == 1
    kv_zeros = (0,) * (mem_kv.ndim - 1)
    return pl.pallas_call(
        _mem_attn_kernel,
        out_shape=jax.ShapeDtypeStruct((b * blocks_per_seq, tq, w), jnp.float32),
        grid=(b // n_seq, blocks_per_seq),
        in_specs=[pl.BlockSpec((n_seq, tq, w), lambda n, i: (n * blocks_per_seq + i, 0, q_col_block)),
                  pl.BlockSpec((n_seq,) + mem_kv.shape[1:], lambda n, i: (n,) + kv_zeros)],
        out_specs=pl.BlockSpec((n_seq, tq, w), lambda n, i: (n * blocks_per_seq + i, 0, 0)),
        compiler_params=pltpu.CompilerParams(
            dimension_semantics=("arbitrary", "arbitrary"), vmem_limit_bytes=VMEM_LIMIT_BYTES),
        name="mem_attn",
    )(q, mem_kv)


def _dil_attn_step_kernel(qkv_ref, c0_ref, c1_ref, c2_ref, o_ref, *, n_new):
    bf16, f32 = jnp.bfloat16, jnp.float32
    scale = HEAD_DIM ** -0.5
    caches = (c0_ref, c1_ref, c2_ref)
    n_groups = len(DIL_GROUPS)
    tq = lax.broadcasted_iota(jnp.int32, (n_new, n_new), 0)
    tk = lax.broadcasted_iota(jnp.int32, (n_new, n_new), 1)
    slots = [(b, h) for b in range(qkv_ref.shape[0]) for h in range(HEADS_PER_GROUP)]
    units = [(b, h, g) for b, h in slots for g in range(n_groups)]
    new_cols = lambda b, part, head: qkv_ref[b, :, part * ATT_WIDTH + head * HEAD_DIM:
                                             part * ATT_WIDTH + (head + 1) * HEAD_DIM].astype(bf16)
    st = {}
    for b, h, g in units:
        win, dil = DIL_GROUPS[g]
        head = g * HEADS_PER_GROUP + h
        slope = _alibi_slope(head)
        q = new_cols(b, 0, head)
        s = jnp.dot(q, caches[g][b, 0, h].astype(bf16), preferred_element_type=f32) * scale
        dist = (win + lax.broadcasted_iota(jnp.int32, (n_new, win), 0)
                - lax.broadcasted_iota(jnp.int32, (n_new, win), 1))
        valid = ((dist & (dil - 1)) == 0) & (dist <= N_BACK * dil)
        s = jnp.where(valid, s - slope * dist.astype(f32), NEG_INF)
        sn = _dot_nt(q, new_cols(b, 1, head)) * scale
        dn = tq - tk
        sn = jnp.where((dn >= 0) & ((dn & (dil - 1)) == 0), sn - slope * dn.astype(f32), NEG_INF)
        st[b, h, g] = (s, sn)
    for u in units:
        s, sn = st[u]
        st[u] = (s, sn, jnp.maximum(jnp.max(s, axis=1, keepdims=True), jnp.max(sn, axis=1, keepdims=True)))
    for u in units:
        s, sn, m = st[u]
        p = jnp.exp(s - m)
        pn = jnp.exp(sn - m)
        st[u] = (p, pn, m, jnp.sum(p, axis=1, keepdims=True) + jnp.sum(pn, axis=1, keepdims=True))
    for b, h, g in units:
        p, pn, m, den = st[b, h, g]
        head = g * HEADS_PER_GROUP + h
        pv = (_dot_nt(p.astype(bf16), caches[g][b, 1, h].astype(bf16))
              + jnp.dot(pn.astype(bf16), new_cols(b, 2, head), preferred_element_type=f32))
        st[b, h, g] = (pv / den, m + jnp.log(den))
    for b, h in slots:
        outs = [st[b, h, g][0] for g in range(n_groups)]
        lses = [st[b, h, g][1] for g in range(n_groups)]
        mm = jnp.maximum(jnp.maximum(lses[0], lses[1]), lses[2])
        w = [jnp.exp(l - mm) for l in lses]
        o_ref[b, :, h * HEAD_DIM:(h + 1) * HEAD_DIM] = (
            (w[0] * outs[0] + w[1] * outs[1] + w[2] * outs[2]) / (w[0] + w[1] + w[2]))


def dilated_attention_step(p_s, cache0, cache1, cache2):
    b, t, _ = p_s.shape
    feat_major = lambda c: jnp.transpose(c, (0, 2, 3, 4, 1))
    ns = ATTN_STEP_SEQS
    win_spec = lambda w: pl.BlockSpec((ns, 2, HEADS_PER_GROUP, HEAD_DIM, w), lambda n: (n, 0, 0, 0, 0))
    return pl.pallas_call(
        functools.partial(_dil_attn_step_kernel, n_new=t),
        out_shape=jax.ShapeDtypeStruct((b, t, GROUP_WIDTH), jnp.float32),
        grid=(b // ns,),
        in_specs=[pl.BlockSpec((ns, t, 3 * ATT_WIDTH), lambda n: (n, 0, 0))] + [win_spec(w) for w, _ in DIL_GROUPS],
        out_specs=pl.BlockSpec((ns, t, GROUP_WIDTH), lambda n: (n, 0, 0)),
        compiler_params=pltpu.CompilerParams(
            dimension_semantics=("arbitrary",), vmem_limit_bytes=VMEM_LIMIT_BYTES),
        name="dil_attn_step",
    )(p_s, feat_major(cache0), feat_major(cache1), feat_major(cache2))


def _rwkv_pre_kernel(u_ref, up_ref, p_ref, pp_ref, mu_rkv_ref, mu_wag_ref, w0_ref, w1_ref, w2_ref,
                     a0_ref, a1_ref, a2_ref, g1_ref, g2_ref, kk_ref, ka_ref, rk_ref,
                     r_ref, lw_ref, k_ref, v_ref, an_ref, b_ref, g_ref, bonus_ref, *, rolled):
    bf16, f32 = jnp.bfloat16, jnp.float32
    dot = functools.partial(jnp.dot, preferred_element_type=f32)
    w = RWKV_WIDTH
    u = u_ref[...]
    p = p_ref[...]
    if rolled:
        first = lax.broadcasted_iota(jnp.int32, (u.shape[0], 1), 0) == 0
        u_prev = jnp.where(first, up_ref[0], pltpu.roll(u, 1, 0))
        p_prev = jnp.where(first, pp_ref[0], pltpu.roll(p, 1, 0))
    else:
        u_prev, p_prev = up_ref[...], pp_ref[...]
    du = u_prev - u
    rkv = p + (p_prev - p) * mu_rkv_ref[...]
    r, k, v = rkv[:, :w], rkv[:, w:2 * w], rkv[:, 2 * w:]
    xw = (u + du * mu_wag_ref[0:1]).astype(bf16)
    xa = (u + du * mu_wag_ref[1:2]).astype(bf16)
    xg = (u + du * mu_wag_ref[2:3]).astype(bf16)
    y = -(w0_ref[...] + dot(jnp.tanh(dot(xw, w1_ref[...])).astype(bf16), w2_ref[...]))
    softplus = jnp.maximum(y, 0.0) + jnp.log1p(jnp.exp(-jnp.abs(y)))
    lw_ref[...] = -jnp.exp(-softplus - 0.5)
    a = jax.nn.sigmoid(a0_ref[...] + dot(dot(xa, a1_ref[...]).astype(bf16), a2_ref[...]))
    g_ref[...] = dot(jax.nn.sigmoid(dot(xg, g1_ref[...])).astype(bf16), g2_ref[...])

    rr = lax.broadcasted_iota(jnp.int32, (LANES, LANES), 0) // RWKV_HEAD_DIM
    cc = lax.broadcasted_iota(jnp.int32, (LANES, LANES), 1) // RWKV_HEAD_DIM
    seg_ones = jnp.where(rr == cc, 1.0, 0.0).astype(bf16)

    def head_sum(x):
        hi = x.astype(bf16)
        lo = (x - hi.astype(f32)).astype(bf16)
        cols = [dot(hi[:, j:j + LANES], seg_ones) + dot(lo[:, j:j + LANES], seg_ones)
                for j in range(0, w, LANES)]
        return jnp.concatenate(cols, axis=1)

    kk = k * kk_ref[...]
    kk = kk / jnp.maximum(jnp.sqrt(head_sum(kk * kk)), 1e-12)
    k2 = k * (1.0 + (a - 1.0) * ka_ref[...])
    r_ref[...] = r
    k_ref[...] = k2
    v_ref[...] = v
    an_ref[...] = -kk
    b_ref[...] = kk * a
    bonus_ref[...] = head_sum(r * k2 * rk_ref[...]) * v


def rwkv_pre(u, u_prev, proj, p_prev, lp, *, rolled):
    n, d = u.shape
    tm = RWKV_PRE_TILE_M
    w3 = 3 * RWKV_WIDTH
    if rolled:
        prev_specs = [pl.BlockSpec((1, 1, d), lambda i: (i, 0, 0)), pl.BlockSpec((1, 1, w3), lambda i: (i, 0, 0))]
    else:
        prev_specs = [pl.BlockSpec((tm, d), lambda i: (i, 0)), pl.BlockSpec((tm, w3), lambda i: (i, 0))]
    bf16 = jnp.bfloat16
    row = lambda x: x.reshape(1, -1)
    consts = [row(lp["mu_rkv"]), lp["mu_wag"], row(lp["w0"]), lp["w1"].astype(bf16), lp["w2"].astype(bf16),
              row(lp["a0"]), lp["a1"].astype(bf16), lp["a2"].astype(bf16), lp["g1"].astype(bf16),
              lp["g2"].astype(bf16), row(lp["k_k"]), row(lp["k_a"]), row(lp["r_k"])]
    full = lambda a: pl.BlockSpec(a.shape, lambda i: (0, 0))
    out_spec = pl.BlockSpec((tm, RWKV_WIDTH), lambda i: (i, 0))
    return pl.pallas_call(
        functools.partial(_rwkv_pre_kernel, rolled=rolled),
        out_shape=tuple(jax.ShapeDtypeStruct((n, RWKV_WIDTH), jnp.float32) for _ in range(8)),
        grid=(n // tm,),
        in_specs=[pl.BlockSpec((tm, d), lambda i: (i, 0)), prev_specs[0],
                  pl.BlockSpec((tm, w3), lambda i: (i, 1)), prev_specs[1]]
                 + [full(c) for c in consts],
        out_specs=tuple(out_spec for _ in range(8)),
        compiler_params=pltpu.CompilerParams(
            dimension_semantics=("arbitrary",), vmem_limit_bytes=VMEM_LIMIT_BYTES),
        name="rwkv_pre",
    )(u, u_prev, proj, p_prev, *consts)


def _bmm(a, b):
    return lax.dot_general(a, b, (((2,), (1,)), ((0,), (0,))), preferred_element_type=jnp.float32)


def _bmm_nt(a, b):
    return lax.dot_general(a, b, (((2,), (2,)), ((0,), (0,))), preferred_element_type=jnp.float32)


def _bmm_tn(a, b):
    return lax.dot_general(a, b, (((1,), (1,)), ((0,), (0,))), preferred_element_type=jnp.float32)


def _rwkv_chunk_kernel(r_ref, lw_ref, k_ref, v_ref, a_ref, b_ref, g_ref, bonus_ref, gng_ref, gnb_ref, s0_ref,
                       o_ref, sT_ref, s_scr, *, n_seq, seq_rows, chunk, group):
    bf16, f32 = jnp.bfloat16, jnp.float32
    c = pl.program_id(1)
    C = chunk
    H, N = RWKV_HEADS, RWKV_HEAD_DIM

    @pl.when(c == 0)
    def _():
        s_scr[...] = s0_ref[...]

    row = lax.broadcasted_iota(jnp.int32, (C, C), 0)
    col = lax.broadcasted_iota(jnp.int32, (C, C), 1)
    strict = (row > col)[None]
    incl = (row >= col)[None]
    G = group
    tri = jnp.broadcast_to(jnp.where(incl, 1.0, 0.0).astype(bf16), (G * H, C, C))

    rows_of = lambda ref, i: ref[i] if len(ref.shape) == 3 else ref[i, 0]

    def by_head(ref, i):
        x = rows_of(ref, i)
        if seq_rows < C:
            x = jnp.concatenate([x, jnp.zeros((C - seq_rows, x.shape[1]), f32)], axis=0)
        return jnp.stack([x[:, h * N:(h + 1) * N] for h in range(H)], axis=0)

    def one_seq(j, carry):
        seqs = [j * G + q for q in range(G)]
        stack = lambda ref: jnp.concatenate([by_head(ref, i) for i in seqs], axis=0)
        r, lw, k, v, a, b = (stack(ref) for ref in (r_ref, lw_ref, k_ref, v_ref, a_ref, b_ref))
        s0 = jnp.concatenate([s_scr[i] for i in seqs], axis=0)
        hi = lw.astype(bf16)
        r1 = lw - hi.astype(f32)
        mid = r1.astype(bf16)
        lo = (r1 - mid.astype(f32)).astype(bf16)
        lc = _bmm(tri, hi) + _bmm(tri, mid) + _bmm(tri, lo)
        e_pos = jnp.exp(lc)
        e_neg = jnp.exp(-lc)
        a_t = (a * jnp.exp(lc - lw)).astype(bf16)
        r_t = (r * e_pos).astype(bf16)
        b_t = (b * e_neg).astype(bf16)
        k_t = (k * e_neg).astype(bf16)
        v_b = v.astype(bf16)
        s0_b = s0.astype(bf16)

        a_ab = jnp.where(strict, _bmm_nt(a_t, b_t), 0.0)
        a_ak = jnp.where(strict, _bmm_nt(a_t, k_t), 0.0)
        a_rb = jnp.where(incl, _bmm_nt(r_t, b_t), 0.0)
        a_rk = jnp.where(incl, _bmm_nt(r_t, k_t), 0.0)

        x = _bmm_nt(a_t, s0_b) + _bmm(a_ak.astype(bf16), v_b)
        p = a_ab
        n = 1
        while n < C:
            p_b = p.astype(bf16)
            x = x + _bmm(p_b, x.astype(bf16))
            n *= 2
            if n < C:
                p = _bmm(p_b, p_b)
        u_b = x.astype(bf16)
        o = _bmm_nt(r_t, s0_b) + _bmm(a_rb.astype(bf16), u_b) + _bmm(a_rk.astype(bf16), v_b)
        s1 = s0 + _bmm_tn(u_b, b_t) + _bmm_tn(v_b, k_t)
        s1 = s1 * e_pos[:, C - 1:C, :]
        mean = jnp.mean(o, -1, keepdims=True)
        oc = o - mean
        var = jnp.mean(oc * oc, -1, keepdims=True)
        on = oc * lax.rsqrt(var + GN_EPS)
        for q, i in enumerate(seqs):
            s_scr[i] = s1[q * H:(q + 1) * H]
            on_i = jnp.concatenate([on[q * H + h] for h in range(H)], axis=1)[:seq_rows]
            res = (on_i * gng_ref[...] + gnb_ref[...] + rows_of(bonus_ref, i)) * rows_of(g_ref, i)
            if len(o_ref.shape) == 3:
                o_ref[i] = res
            else:
                o_ref[i, 0] = res
        return carry

    lax.fori_loop(0, n_seq // G, one_seq, 0)

    @pl.when(c == pl.num_programs(1) - 1)
    def _():
        sT_ref[...] = s_scr[...]


def rwkv_recurrence(s0, seqs, gn_g, gn_b, *, n_batch, n_chunks, seq_rows, n_seq, chunk, group=1):
    h, n = RWKV_HEADS, RWKV_HEAD_DIM
    w = RWKV_WIDTH
    if seqs[0].ndim == 4:
        seq_spec = pl.BlockSpec((n_seq, 1, seq_rows, w), lambda i, c: (i, c, 0, 0))
    else:
        seq_spec = pl.BlockSpec((n_seq, seq_rows, w), lambda i, c: (i, 0, 0))
    st_spec = pl.BlockSpec((n_seq, h, n, n), lambda i, c: (i, 0, 0, 0))
    vec_spec = pl.BlockSpec((1, w), lambda i, c: (0, 0))
    o, s_t = pl.pallas_call(
        functools.partial(_rwkv_chunk_kernel, n_seq=n_seq, seq_rows=seq_rows, chunk=chunk, group=group),
        out_shape=(jax.ShapeDtypeStruct(seqs[0].shape, jnp.float32),
                   jax.ShapeDtypeStruct((n_batch, h, n, n), jnp.float32)),
        grid=(n_batch // n_seq, n_chunks),
        in_specs=[seq_spec] * 8 + [vec_spec, vec_spec, st_spec],
        out_specs=(seq_spec, st_spec),
        scratch_shapes=[pltpu.VMEM((n_seq, h, n, n), jnp.float32)],
        compiler_params=pltpu.CompilerParams(
            dimension_semantics=("arbitrary", "arbitrary"), vmem_limit_bytes=VMEM_LIMIT_BYTES),
        name="rwkv_chunk",
    )(*seqs, gn_g.reshape(1, w), gn_b.reshape(1, w), s0)
    return o, s_t


def _moe_scatter_kernel(slot_ref, h_ref, init_hbm, out_hbm, stage, sems):
    del init_hbm
    i = pl.program_id(0)
    nt = pl.num_programs(0)
    tt = h_ref.shape[0]
    slot = i % 2

    def wait_buffer(s):
        for _ in range(TOP_K):
            pltpu.make_async_copy(stage.at[s], out_hbm.at[pl.ds(0, tt)], sems.at[s]).wait()

    @pl.when(i >= 2)
    def _():
        wait_buffer(slot)

    stage[slot] = h_ref[...]
    for j in range(tt):
        for k in range(TOP_K):
            dst = slot_ref[(i * tt + j) * TOP_K + k]
            pltpu.make_async_copy(stage.at[slot, pl.ds(j, 1)], out_hbm.at[pl.ds(dst, 1)],
                                  sems.at[slot]).start(priority=(j * TOP_K + k) % DMA_PRIORITIES)

    @pl.when(i == nt - 1)
    def _():
        wait_buffer(slot)

        @pl.when(nt >= 2)
        def _():
            wait_buffer(1 - slot)


def _moe_scatter_rows(h, pair_slot, p_pad):
    n, d = h.shape
    tt = MOE_ROUTE_TILE
    return pl.pallas_call(
        _moe_scatter_kernel,
        out_shape=jax.ShapeDtypeStruct((p_pad, d), jnp.float32),
        grid_spec=pltpu.PrefetchScalarGridSpec(
            num_scalar_prefetch=1,
            grid=(n // tt,),
            in_specs=[pl.BlockSpec((tt, d), lambda i, ps: (i, 0)),
                      pl.BlockSpec(memory_space=pl.ANY)],
            out_specs=pl.BlockSpec(memory_space=pl.ANY),
            scratch_shapes=[pltpu.VMEM((2, tt, d), jnp.float32), pltpu.SemaphoreType.DMA((2,))],
        ),
        input_output_aliases={2: 0},
        compiler_params=pltpu.CompilerParams(
            dimension_semantics=("arbitrary",), vmem_limit_bytes=VMEM_LIMIT_BYTES),
        name="moe_scatter",
    )(pair_slot.reshape(-1), h, jnp.zeros((p_pad, d), jnp.float32))


def _weight_copies(w_hbm, stage, sem, expert, col0, width, slot, n_parts, part_stride):
    return [pltpu.make_async_copy(w_hbm.at[expert, :, pl.ds(pl.multiple_of(col0 + p * part_stride, LANES), width)],
                                  stage.at[slot, p], sem.at[slot]) for p in range(n_parts)]


def _stream_expert_weights(sched_ref, w_hbm, stage, sem, w_bf, *, width, n_parts, part_stride):
    s = pl.program_id(0) * pl.num_programs(1) + pl.program_id(1)
    field = lambda j: sched_ref[s * SCHED_COLS + j]

    @pl.when(field(0) != 0)
    def _():
        slot = field(1)
        cur = _weight_copies(w_hbm, stage, sem, field(2), field(3), width, slot, n_parts, part_stride)

        @pl.when(s == 0)
        def _():
            for cp in cur:
                cp.start()

        for cp in cur:
            cp.wait()
        for p in range(n_parts):
            w_bf[p] = stage[slot, p].astype(jnp.bfloat16)

        @pl.when(field(4) != 0)
        def _():
            for cp in _weight_copies(w_hbm, stage, sem, field(5), field(6), width, 1 - slot, n_parts, part_stride):
                cp.start()


def _weight_schedule(tile_expert, n_chunks, chunk_width):
    nt = tile_expert.shape[0]
    n = n_chunks * nt
    expert = jnp.tile(tile_expert, n_chunks)
    col0 = jnp.repeat(jnp.arange(n_chunks, dtype=jnp.int32) * chunk_width, nt)
    step = jnp.arange(n, dtype=jnp.int32)
    start = (step % nt == 0) | (expert != jnp.roll(expert, 1))
    slot = (jnp.cumsum(start.astype(jnp.int32)) - 1) % 2
    nxt = lax.cummin(jnp.where(start, step, n), axis=0, reverse=True)
    nxt = jnp.concatenate([nxt[1:], jnp.full((1,), n, jnp.int32)])
    has_next = nxt < n
    nxt_c = jnp.minimum(nxt, n - 1)
    return jnp.stack([start.astype(jnp.int32), slot, expert, col0, has_next.astype(jnp.int32),
                      expert[nxt_c], col0[nxt_c]], axis=1).astype(jnp.int32).reshape(-1)


def _moe_up_kernel(sched_ref, tile_valid_ref, x_ref, bg_ref, bu_ref, w_hbm, h_ref, stage, sem, w_bf):
    f = w_hbm.shape[2] // 2
    _stream_expert_weights(sched_ref, w_hbm, stage, sem, w_bf, width=h_ref.shape[1], n_parts=2, part_stride=f)
    i = pl.program_id(1)

    @pl.when(tile_valid_ref[i] != 0)
    def _():
        x = x_ref[...].astype(jnp.bfloat16)
        hg = jnp.dot(x, w_bf[0], preferred_element_type=jnp.float32) + bg_ref[0]
        hu = jnp.dot(x, w_bf[1], preferred_element_type=jnp.float32) + bu_ref[0]
        glu = jnp.minimum(hg, SWIGLU_LIMIT)
        up = jnp.clip(hu, -SWIGLU_LIMIT, SWIGLU_LIMIT)
        act = (up + 1.0) * glu * jax.nn.sigmoid(SWIGLU_ALPHA * glu)
        h_ref[...] = act.astype(h_ref.dtype)

    @pl.when(tile_valid_ref[i] == 0)
    def _():
        h_ref[...] = jnp.zeros_like(h_ref)


def _moe_down_kernel(sched_ref, tile_valid_ref, h_ref, bd_ref, w_hbm, y_ref, stage, sem, w_bf):
    _stream_expert_weights(sched_ref, w_hbm, stage, sem, w_bf, width=y_ref.shape[1], n_parts=1, part_stride=0)
    i = pl.program_id(1)

    @pl.when(tile_valid_ref[i] != 0)
    def _():
        y_ref[...] = jnp.dot(h_ref[...], w_bf[0], preferred_element_type=jnp.float32) + bd_ref[0]

    @pl.when(tile_valid_ref[i] == 0)
    def _():
        y_ref[...] = jnp.zeros_like(y_ref)


def _moe_grouped(x_sorted, tile_expert, tile_valid, w_gate_up, b_gate_up, w_down, b_down):
    p_pad, d = x_sorted.shape
    n_exp, _, two_f = w_gate_up.shape
    f = two_f // 2
    tm, fc, dc = MOE_TILE_M, MOE_F_CHUNK, MOE_D_CHUNK
    nt = p_pad // tm
    nfc = f // fc
    ndc = d // dc
    bgu = b_gate_up.reshape(n_exp, 1, two_f)
    bdn = b_down.reshape(n_exp, 1, d)
    params = pltpu.CompilerParams(dimension_semantics=("arbitrary", "arbitrary"),
                                  vmem_limit_bytes=MOE_VMEM_LIMIT_BYTES)
    te_of = lambda sched, c, i: sched[(c * nt + i) * SCHED_COLS + 2]

    h = pl.pallas_call(
        _moe_up_kernel,
        out_shape=jax.ShapeDtypeStruct((p_pad, f), jnp.bfloat16),
        grid_spec=pltpu.PrefetchScalarGridSpec(
            num_scalar_prefetch=2,
            grid=(nfc, nt),
            in_specs=[
                pl.BlockSpec((tm, d), lambda c, i, sc, tv: (i, 0)),
                pl.BlockSpec((1, 1, fc), lambda c, i, sc, tv: (te_of(sc, c, i), 0, c)),
                pl.BlockSpec((1, 1, fc), lambda c, i, sc, tv: (te_of(sc, c, i), 0, nfc + c)),
                pl.BlockSpec(memory_space=pl.ANY),
            ],
            out_specs=pl.BlockSpec((tm, fc), lambda c, i, sc, tv: (i, c)),
            scratch_shapes=[pltpu.VMEM((2, 2, d, fc), jnp.float32), pltpu.SemaphoreType.DMA((2,)),
                            pltpu.VMEM((2, d, fc), jnp.bfloat16)],
        ),
        compiler_params=params,
        name="moe_up",
    )(_weight_schedule(tile_expert, nfc, fc), tile_valid, x_sorted, bgu, bgu, w_gate_up)

    return pl.pallas_call(
        _moe_down_kernel,
        out_shape=jax.ShapeDtypeStruct((p_pad, d), jnp.float32),
        grid_spec=pltpu.PrefetchScalarGridSpec(
            num_scalar_prefetch=2,
            grid=(ndc, nt),
            in_specs=[
                pl.BlockSpec((tm, f), lambda c, i, sc, tv: (i, 0)),
                pl.BlockSpec((1, 1, dc), lambda c, i, sc, tv: (te_of(sc, c, i), 0, c)),
                pl.BlockSpec(memory_space=pl.ANY),
            ],
            out_specs=pl.BlockSpec((tm, dc), lambda c, i, sc, tv: (i, c)),
            scratch_shapes=[pltpu.VMEM((2, 1, f, dc), jnp.float32), pltpu.SemaphoreType.DMA((2,)),
                            pltpu.VMEM((1, f, dc), jnp.bfloat16)],
        ),
        compiler_params=params,
        name="moe_down",
    )(_weight_schedule(tile_expert, ndc, dc), tile_valid, h, bdn, w_down)


def _moe_combine_kernel(slot_ref, h_ref, w_ref, g_ref, b_ref, y_hbm, out_ref, buf, sems):
    i = pl.program_id(0)
    nt = pl.num_programs(0)
    tt = h_ref.shape[0]

    def start_tile(t, s):
        for j in range(tt):
            for k in range(TOP_K):
                src = slot_ref[(t * tt + j) * TOP_K + k]
                pltpu.make_async_copy(y_hbm.at[pl.ds(src, 1)], buf.at[s, pl.ds(k * tt + j, 1)],
                                      sems.at[s]).start(priority=(j * TOP_K + k) % DMA_PRIORITIES)

    @pl.when(i == 0)
    def _():
        start_tile(0, 0)

    @pl.when(i + 1 < nt)
    def _():
        start_tile(i + 1, (i + 1) % 2)

    slot = i % 2
    pltpu.make_async_copy(y_hbm.at[pl.ds(0, TOP_K * tt)], buf.at[slot], sems.at[slot]).wait()
    w = w_ref[...]
    ffn = buf[slot, 0:tt] * w[:, 0:1]
    for k in range(1, TOP_K):
        ffn = ffn + buf[slot, k * tt:(k + 1) * tt] * w[:, k:k + 1]
    x = DEEPNORM_ALPHA * h_ref[...] + ffn
    mu = jnp.mean(x, -1, keepdims=True)
    xc = x - mu
    var = jnp.mean(xc * xc, -1, keepdims=True)
    out_ref[...] = xc * lax.rsqrt(var + LN_EPS) * g_ref[...] + b_ref[...]


def _moe_combine_ln(h, y, pair_slot, top_w, ln_g, ln_b):
    n, d = h.shape
    tt = MOE_ROUTE_TILE
    return pl.pallas_call(
        _moe_combine_kernel,
        out_shape=jax.ShapeDtypeStruct((n, d), jnp.float32),
        grid_spec=pltpu.PrefetchScalarGridSpec(
            num_scalar_prefetch=1,
            grid=(n // tt,),
            in_specs=[pl.BlockSpec((tt, d), lambda i, ps: (i, 0)),
                      pl.BlockSpec((tt, TOP_K), lambda i, ps: (i, 0)),
                      pl.BlockSpec((1, d), lambda i, ps: (0, 0)),
                      pl.BlockSpec((1, d), lambda i, ps: (0, 0)),
                      pl.BlockSpec(memory_space=pl.ANY)],
            out_specs=pl.BlockSpec((tt, d), lambda i, ps: (i, 0)),
            scratch_shapes=[pltpu.VMEM((2, TOP_K * tt, d), jnp.float32), pltpu.SemaphoreType.DMA((2,))],
        ),
        compiler_params=pltpu.CompilerParams(
            dimension_semantics=("arbitrary",), vmem_limit_bytes=VMEM_LIMIT_BYTES),
        name="moe_combine",
    )(pair_slot.reshape(-1), h, top_w, ln_g.reshape(1, d), ln_b.reshape(1, d), y)


def moe_ffn_ln(x, top_w, top_idx, w_gate_up, b_gate_up, w_down, b_down, ln_g, ln_b):
    n, d = x.shape
    tm = MOE_TILE_M
    nt = n * TOP_K // tm + N_EXPERTS
    onehot = (top_idx[:, :, None] == jnp.arange(N_EXPERTS, dtype=top_idx.dtype)).astype(jnp.int32)
    routed = jnp.sum(onehot, axis=1)
    before = jnp.cumsum(routed, axis=0) - routed
    counts = before[-1] + routed[-1]
    tiles_per = (counts + tm - 1) // tm
    tile_end = jnp.cumsum(tiles_per)
    first_row = (tile_end - tiles_per) * tm
    pair_slot = jnp.sum(onehot * (before + first_row)[:, None, :], axis=2)
    tile_ids = jnp.arange(nt, dtype=jnp.int32)
    tile_expert = jnp.minimum(jnp.sum((tile_ids[:, None] >= tile_end[None, :]).astype(jnp.int32), axis=1),
                              N_EXPERTS - 1).astype(jnp.int32)
    tile_valid = (tile_ids < tile_end[-1]).astype(jnp.int32)

    x_sorted = _moe_scatter_rows(x, pair_slot, nt * tm)
    y = _moe_grouped(x_sorted, tile_expert, tile_valid, w_gate_up, b_gate_up, w_down, b_down)
    return _moe_combine_ln(x, y, pair_slot, top_w, ln_g, ln_b)


def kernel(x_prompt, x_sample, cache_win0_kv, cache_win1_kv, cache_win2_kv, cache_mem_kv,
           state_wkv, state_shift, mem_prompt,
           w_in, w_gate, b_gate, w_branch, w_out, w_mem_kv,
           mu_rkv, mu_wag, w0, w1, w2, a0, a1, a2, g1, g2, k_k, k_a, r_k, gn_g, gn_b,
           ln1_g, ln1_b, w_router, b_router, w_gate_up, b_gate_up, w_down, b_down, ln2_g, ln2_b):
    bf16 = jnp.bfloat16
    xp, xs = x_prompt, x_sample
    bp, tp, d = xp.shape
    bs, ts, _ = xs.shape
    n_prompt = bp * tp
    l = 0
    lp = {"mu_rkv": mu_rkv[l], "mu_wag": mu_wag[l],
          "w0": w0[l], "w1": w1[l], "w2": w2[l], "a0": a0[l], "a1": a1[l], "a2": a2[l],
          "g1": g1[l], "g2": g2[l], "k_k": k_k[l], "k_a": k_a[l], "r_k": r_k[l]}

    xp2, xs2 = xp.reshape(-1, d), xs.reshape(-1, d)
    u_all = jnp.concatenate([xp2, xs2], axis=0)
    w_in_bf = w_in[l].astype(bf16)
    proj_p2 = matmul_bias(xp2.astype(bf16), w_in_bf, tile_m=PROJ_PROMPT_TILE_M, name="proj_in")
    proj_s2 = matmul_bias(xs2.astype(bf16), w_in_bf, tile_m=xs2.shape[0], name="proj_in_step")
    gate = matmul_bias(u_all.astype(bf16), w_gate[l].astype(bf16), b_gate[l], tile_m=PROJ_TILE_M, sigmoid=True,
                       name="proj_gate")
    mem_rows = mem_prompt.reshape(-1, d)
    mem_kv_p = matmul_bias(mem_rows.astype(bf16), w_mem_kv[l].astype(bf16), tile_m=mem_rows.shape[0], name="proj_mem")
    proj_p = proj_p2.reshape(bp, tp, PROJ_WIDTH)
    proj_s = proj_s2.reshape(bs, ts, PROJ_WIDTH)
    rkv0, qm0 = 3 * ATT_WIDTH, 3 * ATT_WIDTH + 3 * RWKV_WIDTH

    heads = lambda t: t.reshape(t.shape[0], t.shape[1], N_ATT_HEADS, HEAD_DIM)
    ka_p, va_p = (heads(proj_p[..., i * ATT_WIDTH:(i + 1) * ATT_WIDTH]) for i in (1, 2))
    ka_s, va_s = (heads(proj_s[..., i * ATT_WIDTH:(i + 1) * ATT_WIDTH]) for i in (1, 2))
    outs, lses, win_p, win_s = [], [], [], []
    for g, (win, dil) in enumerate(DIL_GROUPS):
        hs = slice(g * HEADS_PER_GROUP, (g + 1) * HEADS_PER_GROUP)
        o, lse = dilated_attention_prompt(proj_p2, bp, tp, g, dil)
        outs.append(o)
        lses.append(lse)
        keep = min(win, tp)
        win_p.append(jnp.stack([ka_p[:, tp - keep:, hs], va_p[:, tp - keep:, hs]], axis=2))
        win_s.append(jnp.stack([ka_s[:, :, hs], va_s[:, :, hs]], axis=2))
    o_att_p = att_combine(outs, lses)
    o_att_s = dilated_attention_step(proj_s, cache_win0_kv[l], cache_win1_kv[l], cache_win2_kv[l])

    tq = MEM_ATTN_TILE_Q
    o_mem_p = memory_attention(proj_p2.reshape(-1, tq, PROJ_WIDTH), qm0 // MEM_WIDTH,
                               mem_kv_p.reshape(bp, MEM_TOKENS, 2 * MEM_WIDTH), tp // tq)
    o_mem_s = memory_attention(proj_s, qm0 // MEM_WIDTH, cache_mem_kv[l], 1, n_seq=MEM_ATTN_STEP_SEQS)

    tm = RWKV_PRE_TILE_M

    def tile_prev_rows(x):
        last = x[:, tm - 1::tm][:, :-1]
        return jnp.concatenate([jnp.zeros_like(last[:, :1]), last], axis=1).reshape(-1, 1, x.shape[-1])

    rwkv_in_p = rwkv_pre(xp2, tile_prev_rows(xp), proj_p2, tile_prev_rows(proj_p[..., rkv0:qm0]), lp, rolled=True)
    u_last_s = state_shift[l].astype(xs.dtype)
    p_last_s = matmul_bias(u_last_s.astype(bf16), w_in_bf[:, rkv0:qm0], tile_m=bs, name="proj_shift")
    shift = lambda first, rest: jnp.concatenate([first[:, None], rest[:, :-1]], axis=1).reshape(-1, rest.shape[-1])
    rwkv_in_s = rwkv_pre(xs2, shift(u_last_s, xs), proj_s2, shift(p_last_s, proj_s[..., rkv0:qm0]), lp, rolled=False)
    s0 = jnp.zeros((bp, RWKV_HEADS, RWKV_HEAD_DIM, RWKV_HEAD_DIM), jnp.float32)
    o_rwkv_p, wkv_p = rwkv_recurrence(
        s0, [x.reshape(bp, tp // RWKV_CHUNK, RWKV_CHUNK, RWKV_WIDTH) for x in rwkv_in_p], gn_g[l], gn_b[l],
        n_batch=bp, n_chunks=tp // RWKV_CHUNK, seq_rows=RWKV_CHUNK, n_seq=RWKV_PROMPT_SEQS, chunk=RWKV_CHUNK,
        group=RWKV_PROMPT_SEQS)
    o_rwkv_s, wkv_s = rwkv_recurrence(
        state_wkv[l].astype(jnp.float32), [x.reshape(bs, ts, RWKV_WIDTH) for x in rwkv_in_s], gn_g[l], gn_b[l],
        n_batch=bs, n_chunks=1, seq_rows=ts, n_seq=RWKV_SEQ_PER_STEP, chunk=-(-ts // SUBLANES) * SUBLANES,
        group=RWKV_SEQ_GROUP)

    cat = lambda a, b, w: jnp.concatenate([a.reshape(-1, w), b.reshape(-1, w)], axis=0)
    h, top_w, top_idx = merge_ln_router(
        cat(o_att_p, o_att_s, ATT_OUT), cat(o_rwkv_p, o_rwkv_s, RWKV_WIDTH), cat(o_mem_p, o_mem_s, MEM_WIDTH),
        gate, u_all, w_branch[l].astype(bf16), w_out[l].astype(bf16), ln1_g[l], ln1_b[l], w_router[l], b_router[l])
    y_all = moe_ffn_ln(h, top_w, top_idx, w_gate_up[l], b_gate_up[l], w_down[l], b_down[l], ln2_g[l], ln2_b[l])
    yp = y_all[:n_prompt].reshape(bp, tp, d)
    ys = y_all[n_prompt:].reshape(bs, ts, d)
    mem_kv_out = mem_kv_p.reshape(bp, MEM_TOKENS, 2, MEM_HEADS, MEM_HEAD_DIM)
    return (yp, ys,
            win_p[0][None], win_p[1][None], win_p[2][None], wkv_p.astype(xp.dtype)[None],
            xp[:, -1][None], mem_kv_out[None],
            win_s[0].astype(cache_win0_kv.dtype)[None], win_s[1].astype(cache_win0_kv.dtype)[None],
            win_s[2].astype(cache_win0_kv.dtype)[None], wkv_s.astype(state_wkv.dtype)[None],
            xs[:, -1].astype(state_shift.dtype)[None])
```
